```python
import jax, jax.numpy as jnp
from jax import lax
import numpy as np

D_MODEL = 2048
BATCH = 2
SEQ = 16384
DEPTH = 1

HEAD_DIM = 128
N_ATTN_HEADS = D_MODEL // (2 * HEAD_DIM)
ATTN_W = N_ATTN_HEADS * HEAD_DIM
N_MEM_HEADS = 4
MEM_W = N_MEM_HEADS * HEAD_DIM
CONV_W = D_MODEL - ATTN_W - MEM_W
CONV_K = 3
N_MEM = 256
DILATED_BRANCHES = ((128, 1), (512, 4), (2048, 16))
ATTN_BLOCK = 128
PROJ_W = 3 * ATTN_W + 3 * CONV_W + MEM_W
SPLITS = (ATTN_W, 2 * ATTN_W, 3 * ATTN_W, 3 * ATTN_W + CONV_W,
          3 * ATTN_W + 2 * CONV_W, 3 * ATTN_W + 3 * CONV_W)
N_GROUPS = 8
EXPERTS_PER_GROUP = 8
N_EXPERTS = N_GROUPS * EXPERTS_PER_GROUP
TOP_K_IN_GROUP = 2
D_EXPERT = D_MODEL // 2
MOE_BLOCK = 128
EPS = 1e-6
NEG_INF = -1e30

kernel_name = "hymba_dilated_conv_memory_hmoe"


def rms_norm(x, g):
    xf = x.astype(jnp.float32)
    y = xf * lax.rsqrt(jnp.mean(xf * xf, axis=-1, keepdims=True) + EPS)
    return (y * g.astype(jnp.float32)).astype(x.dtype)


def alibi_slopes(n):
    return 2.0 ** (-8.0 * jnp.arange(1, n + 1, dtype=jnp.float32) / n)


def banded_causal_attention(q, k, v, span, step, slopes):
    N, L, H, hd = q.shape
    blk = ATTN_BLOCK
    nb = -(-L // blk)
    Lp = nb * blk
    pad = ((0, 0), (0, Lp - L), (0, 0), (0, 0))
    qb = jnp.pad(q, pad).reshape(N, nb, blk, H, hd).astype(jnp.float32)
    kb = jnp.pad(k, pad).reshape(N, nb, blk, H, hd).astype(jnp.float32)
    vb = jnp.pad(v, pad).reshape(N, nb, blk, H, hd).astype(jnp.float32)
    shift = ((0, 0), (1, 0), (0, 0), (0, 0), (0, 0))
    kk = jnp.concatenate([jnp.pad(kb, shift)[:, :-1], kb], axis=2)
    vv = jnp.concatenate([jnp.pad(vb, shift)[:, :-1], vb], axis=2)
    s = jnp.einsum('nbqhd,nbkhd->nbhqk', qb, kk) * (hd ** -0.5)
    qpos = jnp.arange(nb)[:, None] * blk + jnp.arange(blk)[None, :]
    kpos = jnp.arange(nb)[:, None] * blk - blk + jnp.arange(2 * blk)[None, :]
    dist = qpos[:, :, None] - kpos[:, None, :]
    valid = (dist >= 0) & (dist <= span) & (kpos[:, None, :] >= 0)
    bias = -slopes[None, :, None, None] * (dist * step).astype(jnp.float32)[:, None]
    s = jnp.where(valid[None, :, None], s + bias[None], NEG_INF)
    lse = jax.nn.logsumexp(s, axis=-1)
    p = jnp.exp(s - lse[..., None])
    o = jnp.einsum('nbhqk,nbkhd->nbqhd', p, vv).reshape(N, Lp, H, hd)[:, :L]
    lse = jnp.transpose(lse, (0, 1, 3, 2)).reshape(N, Lp, H)[:, :L]
    return o, lse


def dilated_attention(q, k, v, slopes):
    B, S, H, hd = q.shape
    outs, lses = [], []
    for window, dil in DILATED_BRANCHES:
        Ls = S // dil
        def to_res(t):
            return t.reshape(B, Ls, dil, H, hd).transpose(0, 2, 1, 3, 4).reshape(B * dil, Ls, H, hd)
        o, lse = banded_causal_attention(to_res(q), to_res(k), to_res(v), window // dil, dil, slopes)
        outs.append(o.reshape(B, dil, Ls, H, hd).transpose(0, 2, 1, 3, 4).reshape(B, S, H, hd))
        lses.append(lse.reshape(B, dil, Ls, H).transpose(0, 2, 1, 3).reshape(B, S, H))
    w = jax.nn.softmax(jnp.stack(lses, axis=0), axis=0)
    return jnp.sum(w[..., None] * jnp.stack(outs, axis=0), axis=0)


def gated_short_conv(b_gate, c_gate, u, conv_w):
    z = c_gate * u
    C = z.shape[-1]
    y = lax.conv_general_dilated(z, conv_w[:, None, :].astype(z.dtype), window_strides=(1,),
                                 padding=((CONV_K - 1, 0),),
                                 dimension_numbers=('NWC', 'WIO', 'NWC'),
                                 feature_group_count=C)
    return b_gate * y


def memory_attention(q, mk, mv):
    s = jnp.einsum('bshd,bmhd->bhsm', q.astype(jnp.float32), mk.astype(jnp.float32)) * (HEAD_DIM ** -0.5)
    p = jax.nn.softmax(s, axis=-1)
    return jnp.einsum('bhsm,bmhd->bshd', p, mv.astype(jnp.float32))


def hierarchical_moe(h, w_rg, b_rg, w_re, b_re, w_gate, w_up, w_down):
    T, D = h.shape
    hf = h.astype(jnp.float32)
    g_prob = jax.nn.softmax(hf @ w_rg.astype(jnp.float32) + b_rg.astype(jnp.float32), axis=-1)
    g_w, g_idx = lax.top_k(g_prob, 1)
    e_logits = (hf @ w_re.astype(jnp.float32) + b_re.astype(jnp.float32)).reshape(T, N_GROUPS, EXPERTS_PER_GROUP)
    e_logits = jnp.take_along_axis(e_logits, g_idx[:, :, None], axis=1)[:, 0]
    e_prob = jax.nn.softmax(e_logits, axis=-1)
    top_p, top_i = lax.top_k(e_prob, TOP_K_IN_GROUP)
    top_p = top_p / jnp.sum(top_p, axis=-1, keepdims=True)
    gates = (g_w * top_p).reshape(-1)
    experts = (g_idx * EXPERTS_PER_GROUP + top_i).reshape(-1)
    tokens = jnp.repeat(jnp.arange(T), TOP_K_IN_GROUP)
    A = T * TOP_K_IN_GROUP
    order = jnp.argsort(experts)
    e_sorted, tok_sorted, gate_sorted = experts[order], tokens[order], gates[order]
    counts = jnp.bincount(experts, length=N_EXPERTS)
    padded = (counts + MOE_BLOCK - 1) // MOE_BLOCK * MOE_BLOCK
    starts = jnp.cumsum(counts) - counts
    pends = jnp.cumsum(padded)
    pstarts = pends - padded
    dest = pstarts[e_sorted] + (jnp.arange(A) - starts[e_sorted])
    n_blocks = (A + N_EXPERTS * (MOE_BLOCK - 1) + MOE_BLOCK - 1) // MOE_BLOCK
    R = n_blocks * MOE_BLOCK
    xbuf = jnp.zeros((R, D), h.dtype).at[dest].set(h[tok_sorted])
    block_expert = jnp.minimum(
        jnp.searchsorted(pends, jnp.arange(n_blocks) * MOE_BLOCK, side='right'), N_EXPERTS - 1)

    def expert_block(args):
        xb, e = args
        a = xb @ w_gate[e]
        u = xb @ w_up[e]
        return (jax.nn.silu(a) * u) @ w_down[e]

    ybuf = lax.map(expert_block, (xbuf.reshape(n_blocks, MOE_BLOCK, D), block_expert)).reshape(R, D)
    y = ybuf[dest] * gate_sorted[:, None].astype(ybuf.dtype)
    return jax.ops.segment_sum(y, tok_sorted, num_segments=T)


def hybrid_layer(x, mem, norm1_g, w_in, q_norm_g, k_norm_g, conv_w, mem_norm_g, w_mem_kv,
                 mem_q_norm_g, mem_k_norm_g, out_norm_g, w_out, norm2_g,
                 w_rg, b_rg, w_re, b_re, w_gate, w_up, w_down):
    B, S, D = x.shape
    h = rms_norm(x, norm1_g)
    proj = jnp.einsum('bsd,de->bse', h, w_in)
    q, k, v, b_gate, c_gate, u, mq = jnp.split(proj, SPLITS, axis=-1)
    q = rms_norm(q.reshape(B, S, N_ATTN_HEADS, HEAD_DIM), q_norm_g)
    k = rms_norm(k.reshape(B, S, N_ATTN_HEADS, HEAD_DIM), k_norm_g)
    v = v.reshape(B, S, N_ATTN_HEADS, HEAD_DIM)
    attn_o = dilated_attention(q, k, v, alibi_slopes(N_ATTN_HEADS)).reshape(B, S, ATTN_W).astype(x.dtype)
    conv_o = gated_short_conv(b_gate, c_gate, u, conv_w)
    mem_h = rms_norm(mem, mem_norm_g)
    mk, mv = jnp.split(jnp.einsum('bmd,de->bme', mem_h, w_mem_kv), 2, axis=-1)
    mk = rms_norm(mk.reshape(B, N_MEM, N_MEM_HEADS, HEAD_DIM), mem_k_norm_g)
    mv = mv.reshape(B, N_MEM, N_MEM_HEADS, HEAD_DIM)
    mq = rms_norm(mq.reshape(B, S, N_MEM_HEADS, HEAD_DIM), mem_q_norm_g)
    mem_o = memory_attention(mq, mk, mv).reshape(B, S, MEM_W).astype(x.dtype)
    y = jnp.concatenate([rms_norm(attn_o, out_norm_g[:ATTN_W]),
                         rms_norm(conv_o, out_norm_g[ATTN_W:ATTN_W + CONV_W]),
                         rms_norm(mem_o, out_norm_g[ATTN_W + CONV_W:])], axis=-1)
    x = x + jnp.einsum('bse,ed->bsd', y, w_out)
    h2 = rms_norm(x, norm2_g).reshape(B * S, D)
    x = x + hierarchical_moe(h2, w_rg, b_rg, w_re, b_re, w_gate, w_up, w_down).reshape(B, S, D).astype(x.dtype)
    return x


def setup_inputs(seed: int = 0) -> dict:
    key = jax.random.key(seed)
    ks = jax.random.split(key, 24)
    f32 = jnp.float32
    nrm = lambda k, shape, scale: jax.random.normal(k, shape, f32) * scale
    gain = lambda k, shape: 1.0 + 0.05 * jax.random.normal(k, shape, f32)
    L = DEPTH
    return {
        "x": jax.random.normal(ks[0], (BATCH, SEQ, D_MODEL), f32),
        "mem": jax.random.normal(ks[1], (BATCH, N_MEM, D_MODEL), f32),
        "norm1_g": gain(ks[2], (L, D_MODEL)),
        "w_in": nrm(ks[3], (L, D_MODEL, PROJ_W), D_MODEL ** -0.5),
        "q_norm_g": gain(ks[4], (L, HEAD_DIM)),
        "k_norm_g": gain(ks[5], (L, HEAD_DIM)),
        "conv_w": nrm(ks[6], (L, CONV_K, CONV_W), CONV_K ** -0.5),
        "mem_norm_g": gain(ks[7], (L, D_MODEL)),
        "w_mem_kv": nrm(ks[8], (L, D_MODEL, 2 * MEM_W), D_MODEL ** -0.5),
        "mem_q_norm_g": gain(ks[9], (L, HEAD_DIM)),
        "mem_k_norm_g": gain(ks[10], (L, HEAD_DIM)),
        "out_norm_g": gain(ks[11], (L, D_MODEL)),
        "w_out": nrm(ks[12], (L, D_MODEL, D_MODEL), D_MODEL ** -0.5),
        "norm2_g": gain(ks[13], (L, D_MODEL)),
        "w_router_group": nrm(ks[14], (L, D_MODEL, N_GROUPS), D_MODEL ** -0.5),
        "b_router_group": nrm(ks[15], (L, N_GROUPS), 0.01),
        "w_router_expert": nrm(ks[16], (L, D_MODEL, N_EXPERTS), D_MODEL ** -0.5),
        "b_router_expert": nrm(ks[17], (L, N_EXPERTS), 0.01),
        "w_gate": nrm(ks[18], (L, N_EXPERTS, D_MODEL, D_EXPERT), D_MODEL ** -0.5),
        "w_up": nrm(ks[19], (L, N_EXPERTS, D_MODEL, D_EXPERT), D_MODEL ** -0.5),
        "w_down": nrm(ks[20], (L, N_EXPERTS, D_EXPERT, D_MODEL), D_EXPERT ** -0.5),
    }


def reference(x, mem, norm1_g, w_in, q_norm_g, k_norm_g, conv_w, mem_norm_g, w_mem_kv,
              mem_q_norm_g, mem_k_norm_g, out_norm_g, w_out, norm2_g,
              w_router_group, b_router_group, w_router_expert, b_router_expert,
              w_gate, w_up, w_down):
    for l in range(DEPTH):
        x = hybrid_layer(x, mem, norm1_g[l], w_in[l], q_norm_g[l], k_norm_g[l], conv_w[l],
                         mem_norm_g[l], w_mem_kv[l], mem_q_norm_g[l], mem_k_norm_g[l],
                         out_norm_g[l], w_out[l], norm2_g[l],
                         w_router_group[l], b_router_group[l], w_router_expert[l], b_router_expert[l],
                         w_gate[l], w_up[l], w_down[l])
    return x
```

```python
import functools

import jax
import jax.numpy as jnp
from jax import lax
from jax.experimental import pallas as pl
from jax.experimental.pallas import tpu as pltpu

F32 = jnp.float32
BF16 = jnp.bfloat16
U32 = jnp.uint32
I32 = jnp.int32

D_MODEL = 2048
HEAD_DIM = 128
N_ATTN_HEADS = 8
ATTN_W = N_ATTN_HEADS * HEAD_DIM
N_MEM_HEADS = 4
MEM_W = N_MEM_HEADS * HEAD_DIM
CONV_W = D_MODEL - ATTN_W - MEM_W
CONV_GROUPS = CONV_W // HEAD_DIM
N_MEM = 256
DILATIONS = (1, 4, 16)
SPAN = 128
PROJ_W = 3 * ATTN_W + 3 * CONV_W + MEM_W
N_GROUPS = 8
EXPERTS_PER_GROUP = 8
N_EXPERTS = N_GROUPS * EXPERTS_PER_GROUP
D_EXPERT = D_MODEL // 2
EPS = 1e-6
NEG_INF = -1e30
QK_SCALE = HEAD_DIM ** -0.5

MIB = 1024 * 1024
INPROJ_TM = 512
INPROJ_TN = 1024
ATTN_CHUNK = 2048
MIXER_TM = 256
DISPATCH_TM = 512
COMBINE_TM = 256
ITEM_ROWS = 1024
FFN_SUB = 256
FFN_CHUNK = 256
HALO = 16


def _params(semantics, vmem_mib):
    return pltpu.CompilerParams(dimension_semantics=semantics, vmem_limit_bytes=vmem_mib * MIB)


def _rms(v, width):
    return lax.rsqrt(jnp.sum(v * v, axis=-1, keepdims=True) * (1.0 / width) + EPS)


def _pack_pair(lo, hi):
    lo_b = lax.bitcast_convert_type(lo.astype(BF16).astype(F32), U32)
    hi_b = lax.bitcast_convert_type(hi.astype(BF16).astype(F32), U32)
    return (lo_b >> 16) | (hi_b & jnp.uint32(0xFFFF0000))


def _unpack_pair(w):
    lo = lax.bitcast_convert_type(w << 16, F32)
    hi = lax.bitcast_convert_type(w & jnp.uint32(0xFFFF0000), F32)
    return lo, hi


def _memkv_kernel(mem_ref, g_ref, w_ref, kg_ref, o_ref):
    j = pl.program_id(0)
    m = mem_ref[...]
    h = (m * _rms(m, D_MODEL) * g_ref[...]).astype(BF16)
    kv = jnp.dot(h, w_ref[...].astype(BF16), preferred_element_type=F32)
    is_key = j == 0
    for hh in range(N_MEM_HEADS):
        t = kv[:, hh * HEAD_DIM:(hh + 1) * HEAD_DIM]
        tn = t * _rms(t, HEAD_DIM) * kg_ref[...]
        o_ref[hh] = jnp.where(is_key, tn, t).astype(BF16)


def _memkv(mem2d, mem_norm_g, w_mem_kv, mem_k_norm_g):
    rows = mem2d.shape[0]
    return pl.pallas_call(
        _memkv_kernel,
        grid=(2,),
        in_specs=[
            pl.BlockSpec((rows, D_MODEL), lambda j: (0, 0)),
            pl.BlockSpec((1, D_MODEL), lambda j: (0, 0)),
            pl.BlockSpec((D_MODEL, MEM_W), lambda j: (0, j)),
            pl.BlockSpec((1, HEAD_DIM), lambda j: (0, 0)),
        ],
        out_specs=pl.BlockSpec((N_MEM_HEADS, rows, HEAD_DIM), lambda j: (j, 0, 0)),
        out_shape=jax.ShapeDtypeStruct((2 * N_MEM_HEADS, rows, HEAD_DIM), BF16),
        compiler_params=_params(("arbitrary",), 40),
        name="memkv",
    )(mem2d, mem_norm_g, w_mem_kv, mem_k_norm_g)


def _inproj_kernel(x_ref, g1_ref, w_ref, qg_ref, kg_ref, mqg_ref, qkv_ref, rest_ref, h_ref):
    j = pl.program_id(1)

    @pl.when(j == 0)
    def _():
        x = x_ref[...]
        h_ref[...] = (x * _rms(x, D_MODEL) * g1_ref[...]).astype(BF16)

    y = jnp.dot(h_ref[...], w_ref[...], preferred_element_type=F32)

    def head(hh):
        return y[:, hh * HEAD_DIM:(hh + 1) * HEAD_DIM]

    def normed(t, g):
        return t * _rms(t, HEAD_DIM) * g

    @pl.when(j == 0)
    def _():
        g = qg_ref[...] * QK_SCALE
        for hh in range(8):
            qkv_ref[hh] = normed(head(hh), g)

    @pl.when(j == 1)
    def _():
        g = kg_ref[...]
        for hh in range(8):
            qkv_ref[hh] = normed(head(hh), g)

    @pl.when(j == 2)
    def _():
        for hh in range(8):
            qkv_ref[hh] = head(hh)

    @pl.when(j == 3)
    def _():
        for hh in range(8):
            rest_ref[hh] = head(hh).astype(BF16)

    @pl.when(j == 4)
    def _():
        g = mqg_ref[...] * QK_SCALE
        for hh in range(4):
            rest_ref[hh] = head(hh).astype(BF16)
        for hh in range(4, 8):
            rest_ref[hh] = normed(head(hh), g).astype(BF16)


def _inproj(x2d, norm1_g, w_in_bf16, q_norm_g, k_norm_g, mem_q_norm_g):
    T = x2d.shape[0]
    tm, tn = INPROJ_TM, INPROJ_TN
    small = lambda n: pl.BlockSpec((1, n), lambda i, j: (0, 0))
    return pl.pallas_call(
        _inproj_kernel,
        grid=(T // tm, PROJ_W // tn),
        in_specs=[
            pl.BlockSpec((tm, D_MODEL), lambda i, j: (i, 0)),
            small(D_MODEL),
            pl.BlockSpec((D_MODEL, tn), lambda i, j: (0, j)),
            small(HEAD_DIM), small(HEAD_DIM), small(HEAD_DIM),
        ],
        out_specs=[
            pl.BlockSpec((8, tm, HEAD_DIM), lambda i, j: (jnp.minimum(j, 2), i, 0)),
            pl.BlockSpec((8, tm, HEAD_DIM), lambda i, j: (jnp.maximum(j - 3, 0), i, 0)),
        ],
        out_shape=[
            jax.ShapeDtypeStruct((24, T, HEAD_DIM), F32),
            jax.ShapeDtypeStruct((16, T, HEAD_DIM), BF16),
        ],
        scratch_shapes=[pltpu.VMEM((tm, D_MODEL), BF16)],
        compiler_params=_params(("parallel", "arbitrary"), 40),
        name="inproj",
    )(x2d, norm1_g, w_in_bf16, q_norm_g, k_norm_g, mem_q_norm_g)


def _attn_kernel(slopes_ref, q_ref, k_ref, v_ref, o_ref, kx, vx, o_scr, l_scr):
    h = pl.program_id(1)
    c = pl.program_id(2)
    C = ATTN_CHUNK

    @pl.when(c == 0)
    def _():
        kx[0:C, :] = jnp.zeros((C, HEAD_DIM), F32)
        vx[0:C, :] = jnp.zeros((C, HEAD_DIM), F32)

    @pl.when(c > 0)
    def _():
        kx[0:C, :] = kx[C:2 * C, :]
        vx[0:C, :] = vx[C:2 * C, :]

    kx[C:2 * C, :] = k_ref[0]
    vx[C:2 * C, :] = v_ref[0]

    slope = slopes_ref[h]
    ii = lax.broadcasted_iota(I32, (SPAN, 2 * SPAN), 0)
    jj = lax.broadcasted_iota(I32, (SPAN, 2 * SPAN), 1)
    dist = ii - jj + SPAN
    band = (dist >= 0) & (dist <= SPAN)
    distf = dist.astype(F32)

    for bi, d in enumerate(DILATIONS):
        neg = (-slope * float(d)) * distf
        bias_full = jnp.where(band, neg, NEG_INF)
        bias_first = jnp.where(band & (jj >= SPAN), neg, NEG_INF)

        def unit(u, carry, d=d, bi=bi, bias_full=bias_full, bias_first=bias_first):
            r = u % d
            qb = u // d
            sq = qb * (SPAN * d) + r
            sk = C + sq - SPAN * d
            if d == 1:
                sq = pl.multiple_of(sq, SPAN)
                sk = pl.multiple_of(sk, SPAN)
                qi = pl.ds(sq, SPAN)
                ki = pl.ds(sk, 2 * SPAN)
            else:
                qi = pl.ds(sq, SPAN, stride=d)
                ki = pl.ds(sk, 2 * SPAN, stride=d)
            first = jnp.logical_and(c == 0, qb == 0)
            bias = jnp.where(first, bias_first, bias_full)
            q = q_ref[0, qi, :].astype(BF16)
            k = kx[ki, :].astype(BF16)
            v = vx[ki, :].astype(BF16)
            s = lax.dot_general(q, k, (((1,), (1,)), ((), ())), preferred_element_type=F32) + bias
            m = jnp.max(s, axis=-1, keepdims=True)
            p = jnp.exp(s - m)
            l = jnp.sum(p, axis=-1, keepdims=True)
            o = jnp.dot(p.astype(BF16), v, preferred_element_type=F32) * (1.0 / l)
            o_scr[bi, qi, :] = o
            l_scr[bi, qi, :] = jnp.broadcast_to(m + jnp.log(l), (SPAN, HEAD_DIM))
            return carry

        lax.fori_loop(0, C // SPAN, unit, 0)

    l0, l1, l2 = l_scr[0], l_scr[1], l_scr[2]
    mx = jnp.maximum(jnp.maximum(l0, l1), l2)
    e0, e1, e2 = jnp.exp(l0 - mx), jnp.exp(l1 - mx), jnp.exp(l2 - mx)
    out = (e0 * o_scr[0] + e1 * o_scr[1] + e2 * o_scr[2]) * (1.0 / (e0 + e1 + e2))
    o_ref[0] = out.astype(BF16)


def _attention(qkv, slopes, batch, seq):
    T = qkv.shape[1]
    C = ATTN_CHUNK
    nc = seq // C
    blk = lambda off: pl.BlockSpec((1, C, HEAD_DIM), lambda b, h, c, s: (h + off, b * nc + c, 0))
    grid_spec = pltpu.PrefetchScalarGridSpec(
        num_scalar_prefetch=1,
        grid=(batch, N_ATTN_HEADS, nc),
        in_specs=[blk(0), blk(8), blk(16)],
        out_specs=pl.BlockSpec((1, C, HEAD_DIM), lambda b, h, c, s: (h, b * nc + c, 0)),
        scratch_shapes=[
            pltpu.VMEM((2 * C, HEAD_DIM), F32),
            pltpu.VMEM((2 * C, HEAD_DIM), F32),
            pltpu.VMEM((3, C, HEAD_DIM), F32),
            pltpu.VMEM((3, C, HEAD_DIM), F32),
        ],
    )
    return pl.pallas_call(
        _attn_kernel,
        grid_spec=grid_spec,
        out_shape=jax.ShapeDtypeStruct((N_ATTN_HEADS, T, HEAD_DIM), BF16),
        compiler_params=_params(("parallel", "parallel", "arbitrary"), 40),
        name="attn",
    )(slopes, qkv, qkv, qkv)


def _mixer_kernel(attn_ref, rest_ref, halo_ref, kvm_ref, x_ref, wout_ref, og_ref, g2_ref, cw_ref, wr_ref, br_ref,
                  x2_ref, h2p_ref, route_ref, y_scr, z_scr, *, tiles_per_seq):
    i = pl.program_id(0)
    tm = MIXER_TM
    og = og_ref[...]

    ss = jnp.zeros((tm, 1), F32)
    for hh in range(N_ATTN_HEADS):
        a = attn_ref[hh].astype(F32)
        ss = ss + jnp.sum(a * a, axis=-1, keepdims=True)
    r = lax.rsqrt(ss * (1.0 / ATTN_W) + EPS)
    for hh in range(N_ATTN_HEADS):
        lo = hh * HEAD_DIM
        y_scr[:, lo:lo + HEAD_DIM] = (attn_ref[hh].astype(F32) * r * og[:, lo:lo + HEAD_DIM]).astype(BF16)

    seq_start = (i % tiles_per_seq) == 0
    convs = []
    ss = jnp.zeros((tm, 1), F32)
    for g in range(CONV_GROUPS):
        bg = rest_ref[g].astype(F32)
        z = rest_ref[CONV_GROUPS + g].astype(F32) * rest_ref[2 * CONV_GROUPS + g].astype(F32)
        zh = halo_ref[CONV_GROUPS + g].astype(F32) * halo_ref[2 * CONV_GROUPS + g].astype(F32)
        z_scr[g, 0:HALO, :] = jnp.where(seq_start, 0.0, zh)
        z_scr[g, HALO:HALO + tm, :] = z
        z1 = z_scr[g, HALO - 1:HALO - 1 + tm, :]
        z2 = z_scr[g, HALO - 2:HALO - 2 + tm, :]
        lo = g * HEAD_DIM
        w0 = cw_ref[0:1, lo:lo + HEAD_DIM]
        w1 = cw_ref[1:2, lo:lo + HEAD_DIM]
        w2 = cw_ref[2:3, lo:lo + HEAD_DIM]
        cv = bg * (w2 * z + w1 * z1 + w0 * z2)
        convs.append(cv)
        ss = ss + jnp.sum(cv * cv, axis=-1, keepdims=True)
    r = lax.rsqrt(ss * (1.0 / CONV_W) + EPS)
    for g in range(CONV_GROUPS):
        lo = ATTN_W + g * HEAD_DIM
        y_scr[:, lo:lo + HEAD_DIM] = (convs[g] * r * og[:, lo:lo + HEAD_DIM]).astype(BF16)

    mems = []
    ss = jnp.zeros((tm, 1), F32)
    for hh in range(N_MEM_HEADS):
        mq = rest_ref[3 * CONV_GROUPS + hh]
        s = lax.dot_general(mq, kvm_ref[hh], (((1,), (1,)), ((), ())), preferred_element_type=F32)
        m = jnp.max(s, axis=-1, keepdims=True)
        p = jnp.exp(s - m)
        l = jnp.sum(p, axis=-1, keepdims=True)
        o = jnp.dot(p.astype(BF16), kvm_ref[N_MEM_HEADS + hh], preferred_element_type=F32) * (1.0 / l)
        mems.append(o)
        ss = ss + jnp.sum(o * o, axis=-1, keepdims=True)
    r = lax.rsqrt(ss * (1.0 / MEM_W) + EPS)
    for hh in range(N_MEM_HEADS):
        lo = ATTN_W + CONV_W + hh * HEAD_DIM
        y_scr[:, lo:lo + HEAD_DIM] = (mems[hh] * r * og[:, lo:lo + HEAD_DIM]).astype(BF16)

    x2 = x_ref[...] + jnp.dot(y_scr[...], wout_ref[...], preferred_element_type=F32)
    x2_ref[...] = x2
    h2 = x2 * _rms(x2, D_MODEL) * g2_ref[...]
    h2p_ref[...] = _pack_pair(h2[:, :D_MODEL // 2], h2[:, D_MODEL // 2:])

    lg = jnp.dot(h2.astype(BF16), wr_ref[...], preferred_element_type=F32) + br_ref[...]
    lane = lax.broadcasted_iota(I32, (tm, 128), 1).astype(F32)
    gl = jnp.where(lane < N_GROUPS, lg, NEG_INF)
    gmax = jnp.max(gl, axis=-1, keepdims=True)
    g_w = 1.0 / jnp.sum(jnp.exp(gl - gmax), axis=-1, keepdims=True)
    g_idx = jnp.min(jnp.where(gl == gmax, lane, 1e9), axis=-1, keepdims=True)
    e_lo = N_GROUPS + EXPERTS_PER_GROUP * g_idx
    el = jnp.where((lane >= e_lo) & (lane < e_lo + EXPERTS_PER_GROUP), lg, NEG_INF)
    e1 = jnp.max(el, axis=-1, keepdims=True)
    i1 = jnp.min(jnp.where(el == e1, lane, 1e9), axis=-1, keepdims=True)
    el2 = jnp.where(lane == i1, NEG_INF, el)
    e2 = jnp.max(el2, axis=-1, keepdims=True)
    i2 = jnp.min(jnp.where(el2 == e2, lane, 1e9), axis=-1, keepdims=True)
    t = jnp.exp(e2 - e1)
    p1 = 1.0 / (1.0 + t)
    p2 = t * p1
    route = jnp.where(lane == 0, i1 - N_GROUPS,
                      jnp.where(lane == 1, i2 - N_GROUPS,
                                jnp.where(lane == 2, g_w * p1,
                                          jnp.where(lane == 3, g_w * p2, 0.0))))
    route_ref[...] = route


def _mixer(attn, rest, kvm, x2d, w_out_bf16, out_norm_g, norm2_g, conv_w, w_router, b_router, seq):
    T = x2d.shape[0]
    tm = MIXER_TM
    tiles_per_seq = seq // tm
    full = lambda shape: pl.BlockSpec(shape, lambda i: tuple(0 for _ in shape))
    return pl.pallas_call(
        functools.partial(_mixer_kernel, tiles_per_seq=tiles_per_seq),
        grid=(T // tm,),
        in_specs=[
            pl.BlockSpec((N_ATTN_HEADS, tm, HEAD_DIM), lambda i: (0, i, 0)),
            pl.BlockSpec((16, tm, HEAD_DIM), lambda i: (0, i, 0)),
            pl.BlockSpec((16, HALO, HEAD_DIM), lambda i: (0, jnp.maximum(i * (tm // HALO) - 1, 0), 0)),
            pl.BlockSpec((2 * N_MEM_HEADS, N_MEM, HEAD_DIM), lambda i: (0, i // tiles_per_seq, 0)),
            pl.BlockSpec((tm, D_MODEL), lambda i: (i, 0)),
            full((D_MODEL, D_MODEL)),
            full((1, D_MODEL)),
            full((1, D_MODEL)),
            full((3, CONV_W)),
            full((D_MODEL, 128)),
            full((1, 128)),
        ],
        out_specs=[
            pl.BlockSpec((tm, D_MODEL), lambda i: (i, 0)),
            pl.BlockSpec((tm, D_MODEL // 2), lambda i: (i, 0)),
            pl.BlockSpec((tm, 128), lambda i: (i, 0)),
        ],
        out_shape=[
            jax.ShapeDtypeStruct((T, D_MODEL), F32),
            jax.ShapeDtypeStruct((T, D_MODEL // 2), U32),
            jax.ShapeDtypeStruct((T, 128), F32),
        ],
        scratch_shapes=[
            pltpu.VMEM((tm, D_MODEL), BF16),
            pltpu.VMEM((CONV_GROUPS, HALO + tm, HEAD_DIM), F32),
        ],
        compiler_params=_params(("arbitrary",), 48),
        name="mixer",
    )(attn, rest, rest, kvm, x2d, w_out_bf16, out_norm_g, norm2_g, conv_w, w_router, b_router)


def _dispatch_kernel(dest_ref, h_ref, xbuf_ref, sem):
    i = pl.program_id(0)
    tm = DISPATCH_TM
    base = i * tm

    def body(t, carry):
        a = 2 * (base + t)
        src = h_ref.at[pl.ds(t, 1)]
        pltpu.make_async_copy(src, xbuf_ref.at[pl.ds(dest_ref[a], 1)], sem).start()
        pltpu.make_async_copy(src, xbuf_ref.at[pl.ds(dest_ref[a + 1], 1)], sem).start()
        return carry

    lax.fori_loop(0, tm, body, 0)
    pltpu.make_async_copy(xbuf_ref.at[pl.ds(0, 2 * tm)], xbuf_ref.at[pl.ds(0, 2 * tm)], sem).wait()


def _dispatch(dest, h2p, n_rows):
    T = h2p.shape[0]
    tm = DISPATCH_TM
    grid_spec = pltpu.PrefetchScalarGridSpec(
        num_scalar_prefetch=1,
        grid=(T // tm,),
        in_specs=[pl.BlockSpec((tm, D_MODEL // 2), lambda i, d: (i, 0))],
        out_specs=pl.BlockSpec(memory_space=pl.ANY),
        scratch_shapes=[pltpu.SemaphoreType.DMA],
    )
    return pl.pallas_call(
        _dispatch_kernel,
        grid_spec=grid_spec,
        out_shape=jax.ShapeDtypeStruct((n_rows, D_MODEL // 2), U32),
        compiler_params=_params(("arbitrary",), 32),
        name="dispatch",
    )(dest, h2p)


def _ffn_kernel(ie_ref, rows_ref, n_ref, x_ref, wg_ref, wu_ref, wd_ref, y_ref, acc_ref):
    w = pl.program_id(0)
    c = pl.program_id(1)
    rows = rows_ref[w]
    n_chunks = D_EXPERT // FFN_CHUNK
    half = D_MODEL // 2

    @pl.when(rows > 0)
    def _():
        wg = wg_ref[0].astype(BF16)
        wu = wu_ref[0].astype(BF16)
        wd = wd_ref[0].astype(BF16)
        for s in range(ITEM_ROWS // FFN_SUB):
            r0 = s * FFN_SUB

            @pl.when(r0 < rows)
            def _(r0=r0):
                rid = lax.broadcasted_iota(I32, (FFN_SUB, 1), 0) + r0
                ok = rid < rows
                lo, hi = _unpack_pair(x_ref[r0:r0 + FFN_SUB, :])
                lo = jnp.where(ok, lo, 0.0).astype(BF16)
                hi = jnp.where(ok, hi, 0.0).astype(BF16)
                a = (jnp.dot(lo, wg[:half], preferred_element_type=F32)
                     + jnp.dot(hi, wg[half:], preferred_element_type=F32))
                u = (jnp.dot(lo, wu[:half], preferred_element_type=F32)
                     + jnp.dot(hi, wu[half:], preferred_element_type=F32))
                hm = (a / (1.0 + jnp.exp(-a)) * u).astype(BF16)
                part = jnp.dot(hm, wd, preferred_element_type=F32)

                @pl.when(c == 0)
                def _():
                    acc_ref[r0:r0 + FFN_SUB, :] = part

                @pl.when(c > 0)
                def _():
                    acc_ref[r0:r0 + FFN_SUB, :] += part

        @pl.when(c == n_chunks - 1)
        def _():
            for s in range(ITEM_ROWS // FFN_SUB):
                r0 = s * FFN_SUB

                @pl.when(r0 < rows)
                def _(r0=r0):
                    y_ref[r0:r0 + FFN_SUB, :] = _pack_pair(acc_ref[r0:r0 + FFN_SUB, :half],
                                                          acc_ref[r0:r0 + FFN_SUB, half:])

                @pl.when(r0 >= rows)
                def _(r0=r0):
                    y_ref[r0:r0 + FFN_SUB, :] = jnp.zeros((FFN_SUB, half), U32)


def _ffn(item_expert, item_rows, n_items, xbuf, w_gate, w_up, w_down):
    max_items = item_expert.shape[0]
    n_chunks = D_EXPERT // FFN_CHUNK
    half = D_MODEL // 2

    def item_blk(w, c, ie, ir, n):
        return (jnp.minimum(w, n[0] - 1), 0)

    def chunk_of(w, c, n):
        return jnp.where(w < n[0], c, n_chunks - 1)

    grid_spec = pltpu.PrefetchScalarGridSpec(
        num_scalar_prefetch=3,
        grid=(max_items, n_chunks),
        in_specs=[
            pl.BlockSpec((ITEM_ROWS, half), item_blk),
            pl.BlockSpec((1, D_MODEL, FFN_CHUNK), lambda w, c, ie, ir, n: (ie[w], 0, chunk_of(w, c, n))),
            pl.BlockSpec((1, D_MODEL, FFN_CHUNK), lambda w, c, ie, ir, n: (ie[w], 0, chunk_of(w, c, n))),
            pl.BlockSpec((1, FFN_CHUNK, D_MODEL), lambda w, c, ie, ir, n: (ie[w], chunk_of(w, c, n), 0)),
        ],
        out_specs=pl.BlockSpec((ITEM_ROWS, half), item_blk),
        scratch_shapes=[pltpu.VMEM((ITEM_ROWS, D_MODEL), F32)],
    )
    return pl.pallas_call(
        _ffn_kernel,
        grid_spec=grid_spec,
        out_shape=jax.ShapeDtypeStruct(xbuf.shape, U32),
        compiler_params=_params(("arbitrary", "arbitrary"), 56),
        name="ffn",
    )(item_expert, item_rows, n_items, xbuf, w_gate, w_up, w_down)


def _combine_kernel(dest_ref, x2_ref, route_ref, ybuf_ref, o_ref, rows_scr, sem):
    i = pl.program_id(0)
    tm = COMBINE_TM
    half = D_MODEL // 2
    base = i * tm

    def body(t, carry):
        a = 2 * (base + t)
        pltpu.make_async_copy(ybuf_ref.at[pl.ds(dest_ref[a], 1)], rows_scr.at[0, pl.ds(t, 1)], sem).start()
        pltpu.make_async_copy(ybuf_ref.at[pl.ds(dest_ref[a + 1], 1)], rows_scr.at[1, pl.ds(t, 1)], sem).start()
        return carry

    lax.fori_loop(0, tm, body, 0)
    for k in range(2):
        pltpu.make_async_copy(ybuf_ref.at[pl.ds(0, tm)], rows_scr.at[k], sem).wait()

    g1 = route_ref[:, 2:3]
    g2 = route_ref[:, 3:4]
    lo1, hi1 = _unpack_pair(rows_scr[0])
    lo2, hi2 = _unpack_pair(rows_scr[1])
    o_ref[:, :half] = x2_ref[:, :half] + g1 * lo1 + g2 * lo2
    o_ref[:, half:] = x2_ref[:, half:] + g1 * hi1 + g2 * hi2


def _combine(dest, x2, route, ybuf):
    T = x2.shape[0]
    tm = COMBINE_TM
    grid_spec = pltpu.PrefetchScalarGridSpec(
        num_scalar_prefetch=1,
        grid=(T // tm,),
        in_specs=[
            pl.BlockSpec((tm, D_MODEL), lambda i, d: (i, 0)),
            pl.BlockSpec((tm, 128), lambda i, d: (i, 0)),
            pl.BlockSpec(memory_space=pl.ANY),
        ],
        out_specs=pl.BlockSpec((tm, D_MODEL), lambda i, d: (i, 0)),
        scratch_shapes=[pltpu.VMEM((2, tm, D_MODEL // 2), U32), pltpu.SemaphoreType.DMA],
    )
    return pl.pallas_call(
        _combine_kernel,
        grid_spec=grid_spec,
        out_shape=jax.ShapeDtypeStruct((T, D_MODEL), F32),
        compiler_params=_params(("arbitrary",), 32),
        name="combine",
    )(dest, x2, route, ybuf)


def _moe_plan(route, max_items):
    e_flat = route[:, :2].astype(I32).reshape(-1)
    onehot = (e_flat[:, None] == jnp.arange(N_EXPERTS, dtype=I32)[None, :]).astype(I32)
    csum = jnp.cumsum(onehot, axis=0)
    counts = csum[-1]
    rank = jnp.sum(onehot * csum, axis=1) - 1
    n_it = (counts + ITEM_ROWS - 1) // ITEM_ROWS
    it_end = jnp.cumsum(n_it)
    it_start = it_end - n_it
    dest = jnp.sum(onehot * (it_start * ITEM_ROWS)[None, :], axis=1) + rank
    n_items = it_end[-1]
    w = jnp.arange(max_items, dtype=I32)
    ie = jnp.minimum(jnp.searchsorted(it_end, w, side="right"), N_EXPERTS - 1).astype(I32)
    used = w < n_items
    rows = jnp.where(used, jnp.minimum(ITEM_ROWS, counts[ie] - (w - it_start[ie]) * ITEM_ROWS), 0)
    ie = jnp.where(used, ie, ie[jnp.maximum(n_items - 1, 0)])
    return dest.astype(I32), ie.astype(I32), rows.astype(I32), n_items.reshape(1).astype(I32)


def _layer(x, mem, norm1_g, w_in, q_norm_g, k_norm_g, conv_w, mem_norm_g, w_mem_kv, mem_q_norm_g, mem_k_norm_g,
           out_norm_g, w_out, norm2_g, w_rg, b_rg, w_re, b_re, w_gate, w_up, w_down):
    B, S, D = x.shape
    T = B * S
    x2d = x.reshape(T, D)
    row = lambda v: v.reshape(1, -1).astype(F32)

    kvm = _memkv(mem.reshape(B * N_MEM, D), row(mem_norm_g), w_mem_kv, row(mem_k_norm_g))
    qkv, rest = _inproj(x2d, row(norm1_g), w_in.astype(BF16), row(q_norm_g), row(k_norm_g), row(mem_q_norm_g))

    slopes = 2.0 ** (-8.0 * jnp.arange(1, N_ATTN_HEADS + 1, dtype=F32) / N_ATTN_HEADS)
    attn = _attention(qkv, slopes, B, S)

    w_router = jnp.zeros((D, 128), F32).at[:, :N_GROUPS].set(w_rg).at[:, N_GROUPS:N_GROUPS + N_EXPERTS].set(w_re)
    b_router = jnp.zeros((1, 128), F32).at[0, :N_GROUPS].set(b_rg).at[0, N_GROUPS:N_GROUPS + N_EXPERTS].set(b_re)
    x2, h2p, route = _mixer(attn, rest, kvm, x2d, w_out.astype(BF16), row(out_norm_g), row(norm2_g),
                            conv_w.astype(F32), w_router.astype(BF16), b_router, S)

    max_items = N_EXPERTS + (2 * T) // ITEM_ROWS
    dest, item_expert, item_rows, n_items = _moe_plan(route, max_items)
    xbuf = _dispatch(dest, h2p, max_items * ITEM_ROWS)
    ybuf = _ffn(item_expert, item_rows, n_items, xbuf, w_gate, w_up, w_down)
    out = _combine(dest, x2, route, ybuf)
    return out.reshape(B, S, D)


def kernel(x, mem, norm1_g, w_in, q_norm_g, k_norm_g, conv_w, mem_norm_g, w_mem_kv, mem_q_norm_g, mem_k_norm_g,
           out_norm_g, w_out, norm2_g, w_router_group, b_router_group, w_router_expert, b_router_expert,
           w_gate, w_up, w_down):
    for l in range(norm1_g.shape[0]):
        x = _layer(x, mem, norm1_g[l], w_in[l], q_norm_g[l], k_norm_g[l], conv_w[l], mem_norm_g[l], w_mem_kv[l],
                   mem_q_norm_g[l], mem_k_norm_g[l], out_norm_g[l], w_out[l], norm2_g[l],
                   w_router_group[l], b_router_group[l], w_router_expert[l], b_router_expert[l],
                   w_gate[l], w_up[l], w_down[l])
    return x
```

```python
import functools

import jax
import jax.numpy as jnp
from jax import lax
from jax.experimental import pallas as pl
from jax.experimental.pallas import tpu as pltpu

F32 = jnp.float32
BF16 = jnp.bfloat16
I32 = jnp.int32

D_MODEL = 2048
HEAD_DIM = 128
N_ATTN_HEADS = 8
ATTN_W = N_ATTN_HEADS * HEAD_DIM
N_MEM_HEADS = 4
MEM_W = N_MEM_HEADS * HEAD_DIM
CONV_W = D_MODEL - ATTN_W - MEM_W
CONV_GROUPS = CONV_W // HEAD_DIM
N_MEM = 256
DILATIONS = (1, 4, 16)
SPAN = 128
PROJ_W = 3 * ATTN_W + 3 * CONV_W + MEM_W
N_GROUPS = 8
EXPERTS_PER_GROUP = 8
N_EXPERTS = N_GROUPS * EXPERTS_PER_GROUP
D_EXPERT = D_MODEL // 2
EPS = 1e-6
NEG_INF = -1e30
QK_SCALE = HEAD_DIM ** -0.5
LOG2E = 1.4426950408889634

MIB = 1024 * 1024
INPROJ_TM = 256
INPROJ_TN = 1024
ATTN_CHUNK = 2048
MIXER_TM = 256
DISPATCH_TM = 256
COMBINE_TM = 256
ITEM_ROWS = 1024
FFN_SUB = 256
FFN_CHUNK = 256
HALO = 16
DMA_UNROLL = 8
ATTN_UNROLL = 16


def _params(semantics, vmem_mib):
    return pltpu.CompilerParams(dimension_semantics=semantics, vmem_limit_bytes=vmem_mib * MIB)


def _rms(v, width):
    return lax.rsqrt(jnp.sum(v * v, axis=-1, keepdims=True) * (1.0 / width) + EPS)


def _memkv_kernel(mem_ref, g_ref, w_ref, kg_ref, o_ref):
    j = pl.program_id(0)
    m = mem_ref[...]
    h = (m * _rms(m, D_MODEL) * g_ref[...]).astype(BF16)
    kv = jnp.dot(h, w_ref[...].astype(BF16), preferred_element_type=F32)
    is_key = j == 0
    for hh in range(N_MEM_HEADS):
        t = kv[:, hh * HEAD_DIM:(hh + 1) * HEAD_DIM]
        tn = t * _rms(t, HEAD_DIM) * kg_ref[...]
        o_ref[hh] = jnp.where(is_key, tn, t).astype(BF16)


def _memkv(mem2d, mem_norm_g, w_mem_kv, mem_k_norm_g):
    rows = mem2d.shape[0]
    return pl.pallas_call(
        _memkv_kernel,
        grid=(2,),
        in_specs=[
            pl.BlockSpec((rows, D_MODEL), lambda j: (0, 0)),
            pl.BlockSpec((1, D_MODEL), lambda j: (0, 0)),
            pl.BlockSpec((D_MODEL, MEM_W), lambda j: (0, j)),
            pl.BlockSpec((1, HEAD_DIM), lambda j: (0, 0)),
        ],
        out_specs=pl.BlockSpec((N_MEM_HEADS, rows, HEAD_DIM), lambda j: (j, 0, 0)),
        out_shape=jax.ShapeDtypeStruct((2 * N_MEM_HEADS, rows, HEAD_DIM), BF16),
        compiler_params=_params(("arbitrary",), 40),
        name="memkv",
    )(mem2d, mem_norm_g, w_mem_kv, mem_k_norm_g)


def _inproj_kernel(x_ref, g1_ref, w_ref, qg_ref, kg_ref, mqg_ref, qkv_ref, rest_ref):
    x = x_ref[...]
    h = (x * _rms(x, D_MODEL) * g1_ref[...]).astype(BF16)

    def normed(t, g):
        return t * _rms(t, HEAD_DIM) * g

    gq = qg_ref[...] * (QK_SCALE * LOG2E)
    gk = kg_ref[...]
    gmq = mqg_ref[...] * QK_SCALE
    groups_per_tile = INPROJ_TN // HEAD_DIM
    for j in range(PROJ_W // INPROJ_TN):
        y = jnp.dot(h, w_ref[:, j * INPROJ_TN:(j + 1) * INPROJ_TN], preferred_element_type=F32)
        for hh in range(groups_per_tile):
            t = y[:, hh * HEAD_DIM:(hh + 1) * HEAD_DIM]
            col = j * groups_per_tile + hh
            if col < 8:
                qkv_ref[col] = normed(t, gq)
            elif col < 16:
                qkv_ref[col] = normed(t, gk)
            elif col < 24:
                qkv_ref[col] = t
            elif col < 36:
                rest_ref[col - 24] = t.astype(BF16)
            else:
                rest_ref[col - 24] = normed(t, gmq).astype(BF16)


def _inproj(x2d, norm1_g, w_in_bf16, q_norm_g, k_norm_g, mem_q_norm_g):
    T = x2d.shape[0]
    tm = INPROJ_TM
    small = lambda n: pl.BlockSpec((1, n), lambda i: (0, 0))
    return pl.pallas_call(
        _inproj_kernel,
        grid=(T // tm,),
        in_specs=[
            pl.BlockSpec((tm, D_MODEL), lambda i: (i, 0)),
            small(D_MODEL),
            pl.BlockSpec((D_MODEL, PROJ_W), lambda i: (0, 0), pipeline_mode=pl.Buffered(1)),
            small(HEAD_DIM), small(HEAD_DIM), small(HEAD_DIM),
        ],
        out_specs=[
            pl.BlockSpec((24, tm, HEAD_DIM), lambda i: (0, i, 0)),
            pl.BlockSpec((16, tm, HEAD_DIM), lambda i: (0, i, 0)),
        ],
        out_shape=[
            jax.ShapeDtypeStruct((24, T, HEAD_DIM), F32),
            jax.ShapeDtypeStruct((16, T, HEAD_DIM), BF16),
        ],
        compiler_params=_params(("parallel",), 52),
        name="inproj",
    )(x2d, norm1_g, w_in_bf16, q_norm_g, k_norm_g, mem_q_norm_g)


def _attn_kernel(slopes_ref, q_ref, k_ref, v_ref, o_ref, kx, vx, o_scr, l_scr, bias_scr):
    h = pl.program_id(1)
    c = pl.program_id(2)
    C = ATTN_CHUNK

    @pl.when(c == 0)
    def _():
        kx[0:C, :] = jnp.zeros((C, HEAD_DIM), F32)
        vx[0:C, :] = jnp.zeros((C, HEAD_DIM), F32)

    @pl.when(c > 0)
    def _():
        kx[0:C, :] = kx[C:2 * C, :]
        vx[0:C, :] = vx[C:2 * C, :]

    kx[C:2 * C, :] = k_ref[0]
    vx[C:2 * C, :] = v_ref[0]

    slope = slopes_ref[h]
    ii = lax.broadcasted_iota(I32, (SPAN, 2 * SPAN), 0)
    jj = lax.broadcasted_iota(I32, (SPAN, 2 * SPAN), 1)
    dist = ii - jj + SPAN
    band = (dist >= 0) & (dist <= SPAN)
    distf = dist.astype(F32)

    for bi, d in enumerate(DILATIONS):
        neg = (-slope * float(d) * LOG2E) * distf
        bias_scr[2 * bi] = jnp.where(band, neg, NEG_INF)
        bias_scr[2 * bi + 1] = jnp.where(band & (jj >= SPAN), neg, NEG_INF)

    for bi, d in enumerate(DILATIONS):

        def unit(u, carry, d=d, bi=bi):
            r = u % d
            qb = u // d
            sq = qb * (SPAN * d) + r
            sk = C + sq - SPAN * d
            if d == 1:
                sq = pl.multiple_of(sq, SPAN)
                sk = pl.multiple_of(sk, SPAN)
                qi = pl.ds(sq, SPAN)
                ki = pl.ds(sk, 2 * SPAN)
            else:
                qi = pl.ds(sq, SPAN, stride=d)
                ki = pl.ds(sk, 2 * SPAN, stride=d)
            first = jnp.logical_and(c == 0, qb == 0).astype(I32)
            q = q_ref[0, qi, :].astype(BF16)
            k = kx[ki, :].astype(BF16)
            v = vx[ki, :].astype(BF16)
            s = lax.dot_general(q, k, (((1,), (1,)), ((), ())), preferred_element_type=F32)
            s = s + bias_scr[2 * bi + first]
            m = jnp.max(s, axis=-1, keepdims=True)
            p = jnp.exp2(s - m)
            l = jnp.sum(p, axis=-1, keepdims=True)
            o = jnp.dot(p.astype(BF16), v, preferred_element_type=F32) * (1.0 / l)
            o_scr[bi, qi, :] = o
            l_scr[bi, qi, :] = jnp.broadcast_to(m + jnp.log2(l), (SPAN, HEAD_DIM))
            return carry

        lax.fori_loop(0, C // SPAN, unit, 0, unroll=ATTN_UNROLL)

    l0, l1, l2 = l_scr[0], l_scr[1], l_scr[2]
    mx = jnp.maximum(jnp.maximum(l0, l1), l2)
    e0, e1, e2 = jnp.exp2(l0 - mx), jnp.exp2(l1 - mx), jnp.exp2(l2 - mx)
    out = (e0 * o_scr[0] + e1 * o_scr[1] + e2 * o_scr[2]) * (1.0 / (e0 + e1 + e2))
    o_ref[0] = out.astype(BF16)


def _attention(qkv, slopes, batch, seq):
    T = qkv.shape[1]
    C = ATTN_CHUNK
    nc = seq // C
    blk = lambda off: pl.BlockSpec((1, C, HEAD_DIM), lambda b, h, c, s: (h + off, b * nc + c, 0))
    grid_spec = pltpu.PrefetchScalarGridSpec(
        num_scalar_prefetch=1,
        grid=(batch, N_ATTN_HEADS, nc),
        in_specs=[blk(0), blk(8), blk(16)],
        out_specs=pl.BlockSpec((1, C, HEAD_DIM), lambda b, h, c, s: (h, b * nc + c, 0)),
        scratch_shapes=[
            pltpu.VMEM((2 * C, HEAD_DIM), F32),
            pltpu.VMEM((2 * C, HEAD_DIM), F32),
            pltpu.VMEM((3, C, HEAD_DIM), F32),
            pltpu.VMEM((3, C, HEAD_DIM), F32),
            pltpu.VMEM((2 * len(DILATIONS), SPAN, 2 * SPAN), F32),
        ],
    )
    return pl.pallas_call(
        _attn_kernel,
        grid_spec=grid_spec,
        out_shape=jax.ShapeDtypeStruct((N_ATTN_HEADS, T, HEAD_DIM), BF16),
        compiler_params=_params(("parallel", "parallel", "arbitrary"), 40),
        name="attn",
    )(slopes, qkv, qkv, qkv)


def _mixer_kernel(attn_ref, rest_ref, halo_ref, kvm_ref, x_ref, wout_ref, og_ref, g2_ref, cw_ref, wr_ref, br_ref,
                  x2_ref, h2_ref, route_ref, y_scr, z_scr, *, tiles_per_seq):
    i = pl.program_id(0)
    tm = MIXER_TM
    og = og_ref[...]

    ss = jnp.zeros((tm, 1), F32)
    for hh in range(N_ATTN_HEADS):
        a = attn_ref[hh].astype(F32)
        ss = ss + jnp.sum(a * a, axis=-1, keepdims=True)
    r = lax.rsqrt(ss * (1.0 / ATTN_W) + EPS)
    for hh in range(N_ATTN_HEADS):
        lo = hh * HEAD_DIM
        y_scr[:, lo:lo + HEAD_DIM] = (attn_ref[hh].astype(F32) * r * og[:, lo:lo + HEAD_DIM]).astype(BF16)

    seq_start = (i % tiles_per_seq) == 0
    convs = []
    ss = jnp.zeros((tm, 1), F32)
    for g in range(CONV_GROUPS):
        bg = rest_ref[g].astype(F32)
        z = rest_ref[CONV_GROUPS + g].astype(F32) * rest_ref[2 * CONV_GROUPS + g].astype(F32)
        zh = halo_ref[CONV_GROUPS + g].astype(F32) * halo_ref[2 * CONV_GROUPS + g].astype(F32)
        z_scr[g, 0:HALO, :] = jnp.where(seq_start, 0.0, zh)
        z_scr[g, HALO:HALO + tm, :] = z
        z1 = z_scr[g, HALO - 1:HALO - 1 + tm, :]
        z2 = z_scr[g, HALO - 2:HALO - 2 + tm, :]
        lo = g * HEAD_DIM
        w0 = cw_ref[0:1, lo:lo + HEAD_DIM]
        w1 = cw_ref[1:2, lo:lo + HEAD_DIM]
        w2 = cw_ref[2:3, lo:lo + HEAD_DIM]
        cv = bg * (w2 * z + w1 * z1 + w0 * z2)
        convs.append(cv)
        ss = ss + jnp.sum(cv * cv, axis=-1, keepdims=True)
    r = lax.rsqrt(ss * (1.0 / CONV_W) + EPS)
    for g in range(CONV_GROUPS):
        lo = ATTN_W + g * HEAD_DIM
        y_scr[:, lo:lo + HEAD_DIM] = (convs[g] * r * og[:, lo:lo + HEAD_DIM]).astype(BF16)

    mems = []
    ss = jnp.zeros((tm, 1), F32)
    for hh in range(N_MEM_HEADS):
        mq = rest_ref[3 * CONV_GROUPS + hh]
        s = lax.dot_general(mq, kvm_ref[hh], (((1,), (1,)), ((), ())), preferred_element_type=F32)
        m = jnp.max(s, axis=-1, keepdims=True)
        p = jnp.exp(s - m)
        l = jnp.sum(p, axis=-1, keepdims=True)
        o = jnp.dot(p.astype(BF16), kvm_ref[N_MEM_HEADS + hh], preferred_element_type=F32) * (1.0 / l)
        mems.append(o)
        ss = ss + jnp.sum(o * o, axis=-1, keepdims=True)
    r = lax.rsqrt(ss * (1.0 / MEM_W) + EPS)
    for hh in range(N_MEM_HEADS):
        lo = ATTN_W + CONV_W + hh * HEAD_DIM
        y_scr[:, lo:lo + HEAD_DIM] = (mems[hh] * r * og[:, lo:lo + HEAD_DIM]).astype(BF16)

    x2 = x_ref[...] + jnp.dot(y_scr[...], wout_ref[...], preferred_element_type=F32)
    x2_ref[...] = x2
    h2 = x2 * _rms(x2, D_MODEL) * g2_ref[...]
    h2_ref[...] = h2

    lg = jnp.dot(h2.astype(BF16), wr_ref[...], preferred_element_type=F32) + br_ref[...]
    lane = lax.broadcasted_iota(I32, (tm, 128), 1).astype(F32)
    gl = jnp.where(lane < N_GROUPS, lg, NEG_INF)
    gmax = jnp.max(gl, axis=-1, keepdims=True)
    g_w = 1.0 / jnp.sum(jnp.exp(gl - gmax), axis=-1, keepdims=True)
    g_idx = jnp.min(jnp.where(gl == gmax, lane, 1e9), axis=-1, keepdims=True)
    e_lo = N_GROUPS + EXPERTS_PER_GROUP * g_idx
    el = jnp.where((lane >= e_lo) & (lane < e_lo + EXPERTS_PER_GROUP), lg, NEG_INF)
    e1 = jnp.max(el, axis=-1, keepdims=True)
    i1 = jnp.min(jnp.where(el == e1, lane, 1e9), axis=-1, keepdims=True)
    el2 = jnp.where(lane == i1, NEG_INF, el)
    e2 = jnp.max(el2, axis=-1, keepdims=True)
    i2 = jnp.min(jnp.where(el2 == e2, lane, 1e9), axis=-1, keepdims=True)
    t = jnp.exp(e2 - e1)
    p1 = 1.0 / (1.0 + t)
    p2 = t * p1
    route = jnp.where(lane == 0, i1 - N_GROUPS,
                      jnp.where(lane == 1, i2 - N_GROUPS,
                                jnp.where(lane == 2, g_w * p1,
                                          jnp.where(lane == 3, g_w * p2, 0.0))))
    route_ref[...] = route


def _mixer(attn, rest, kvm, x2d, w_out_bf16, out_norm_g, norm2_g, conv_w, w_router, b_router, seq):
    T = x2d.shape[0]
    tm = MIXER_TM
    tiles_per_seq = seq // tm
    full = lambda shape: pl.BlockSpec(shape, lambda i: tuple(0 for _ in shape))
    return pl.pallas_call(
        functools.partial(_mixer_kernel, tiles_per_seq=tiles_per_seq),
        grid=(T // tm,),
        in_specs=[
            pl.BlockSpec((N_ATTN_HEADS, tm, HEAD_DIM), lambda i: (0, i, 0)),
            pl.BlockSpec((16, tm, HEAD_DIM), lambda i: (0, i, 0)),
            pl.BlockSpec((16, HALO, HEAD_DIM), lambda i: (0, jnp.maximum(i * (tm // HALO) - 1, 0), 0)),
            pl.BlockSpec((2 * N_MEM_HEADS, N_MEM, HEAD_DIM), lambda i: (0, i // tiles_per_seq, 0)),
            pl.BlockSpec((tm, D_MODEL), lambda i: (i, 0)),
            full((D_MODEL, D_MODEL)),
            full((1, D_MODEL)),
            full((1, D_MODEL)),
            full((3, CONV_W)),
            full((D_MODEL, 128)),
            full((1, 128)),
        ],
        out_specs=[
            pl.BlockSpec((tm, D_MODEL), lambda i: (i, 0)),
            pl.BlockSpec((tm, D_MODEL), lambda i: (i, 0)),
            pl.BlockSpec((tm, 128), lambda i: (i, 0)),
        ],
        out_shape=[
            jax.ShapeDtypeStruct((T, D_MODEL), F32),
            jax.ShapeDtypeStruct((T, D_MODEL), F32),
            jax.ShapeDtypeStruct((T, 128), F32),
        ],
        scratch_shapes=[
            pltpu.VMEM((tm, D_MODEL), BF16),
            pltpu.VMEM((CONV_GROUPS, HALO + tm, HEAD_DIM), F32),
        ],
        compiler_params=_params(("arbitrary",), 48),
        name="mixer",
    )(attn, rest, rest, kvm, x2d, w_out_bf16, out_norm_g, norm2_g, conv_w, w_router, b_router)


def _dispatch_kernel(dest_ref, h_ref, xbuf_ref, sem):
    i = pl.program_id(0)
    tm = DISPATCH_TM
    base = i * tm

    def body(t, carry):
        a = 2 * (base + t)
        src = h_ref.at[pl.ds(t, 1)]
        pltpu.make_async_copy(src, xbuf_ref.at[pl.ds(dest_ref[a], 1)], sem).start()
        pltpu.make_async_copy(src, xbuf_ref.at[pl.ds(dest_ref[a + 1], 1)], sem).start()
        return carry

    lax.fori_loop(0, tm, body, 0, unroll=DMA_UNROLL)
    pltpu.make_async_copy(xbuf_ref.at[pl.ds(0, 2 * tm)], xbuf_ref.at[pl.ds(0, 2 * tm)], sem).wait()


def _dispatch(dest, h2, n_rows):
    T = h2.shape[0]
    tm = DISPATCH_TM
    grid_spec = pltpu.PrefetchScalarGridSpec(
        num_scalar_prefetch=1,
        grid=(T // tm,),
        in_specs=[pl.BlockSpec((tm, D_MODEL), lambda i, d: (i, 0))],
        out_specs=pl.BlockSpec(memory_space=pl.ANY),
        scratch_shapes=[pltpu.SemaphoreType.DMA],
    )
    return pl.pallas_call(
        _dispatch_kernel,
        grid_spec=grid_spec,
        out_shape=jax.ShapeDtypeStruct((n_rows, D_MODEL), F32),
        compiler_params=_params(("arbitrary",), 32),
        name="dispatch",
    )(dest, h2)


def _ffn_kernel(ie_ref, rows_ref, n_ref, x_ref, wg_ref, wu_ref, wd_ref, y_ref, xb_ref):
    w = pl.program_id(0)
    c = pl.program_id(1)
    rows = rows_ref[w]

    @pl.when(rows > 0)
    def _():
        n_sub = (rows + FFN_SUB - 1) // FFN_SUB
        for ns in range(1, ITEM_ROWS // FFN_SUB + 1):
            nr = ns * FFN_SUB

            @pl.when(n_sub == ns)
            def _(nr=nr):
                @pl.when(c == 0)
                def _():
                    ok = lax.broadcasted_iota(I32, (nr, 1), 0) < rows
                    xb_ref[0:nr, :] = jnp.where(ok, x_ref[0:nr, :], 0.0).astype(BF16)
                    y_ref[...] = jnp.zeros((ITEM_ROWS, D_MODEL), F32)

                wg = wg_ref[0].astype(BF16)
                wu = wu_ref[0].astype(BF16)
                wd = wd_ref[0].astype(BF16)
                x = xb_ref[0:nr, :]
                a = jnp.dot(x, wg, preferred_element_type=F32)
                u = jnp.dot(x, wu, preferred_element_type=F32)
                hm = (a / (1.0 + jnp.exp(-a)) * u).astype(BF16)
                y_ref[0:nr, :] += jnp.dot(hm, wd, preferred_element_type=F32)


def _ffn(item_expert, item_rows, n_items, xbuf, w_gate, w_up, w_down):
    max_items = item_expert.shape[0]
    n_chunks = D_EXPERT // FFN_CHUNK

    def item_blk(w, c, ie, ir, n):
        return (jnp.minimum(w, n[0] - 1), 0)

    def chunk_of(w, c, n):
        return jnp.where(w < n[0], c, n_chunks - 1)

    grid_spec = pltpu.PrefetchScalarGridSpec(
        num_scalar_prefetch=3,
        grid=(max_items, n_chunks),
        in_specs=[
            pl.BlockSpec((ITEM_ROWS, D_MODEL), item_blk),
            pl.BlockSpec((1, D_MODEL, FFN_CHUNK), lambda w, c, ie, ir, n: (ie[w], 0, chunk_of(w, c, n))),
            pl.BlockSpec((1, D_MODEL, FFN_CHUNK), lambda w, c, ie, ir, n: (ie[w], 0, chunk_of(w, c, n))),
            pl.BlockSpec((1, FFN_CHUNK, D_MODEL), lambda w, c, ie, ir, n: (ie[w], chunk_of(w, c, n), 0)),
        ],
        out_specs=pl.BlockSpec((ITEM_ROWS, D_MODEL), item_blk),
        scratch_shapes=[pltpu.VMEM((ITEM_ROWS, D_MODEL), BF16)],
    )
    return pl.pallas_call(
        _ffn_kernel,
        grid_spec=grid_spec,
        out_shape=jax.ShapeDtypeStruct(xbuf.shape, F32),
        compiler_params=_params(("arbitrary", "arbitrary"), 60),
        name="ffn",
    )(item_expert, item_rows, n_items, xbuf, w_gate, w_up, w_down)


def _combine_kernel(dest_ref, x2_ref, route_ref, ybuf_ref, o_ref, rows_scr, sem):
    i = pl.program_id(0)
    tm = COMBINE_TM
    base = i * tm

    def body(t, carry):
        a = 2 * (base + t)
        pltpu.make_async_copy(ybuf_ref.at[pl.ds(dest_ref[a], 1)], rows_scr.at[0, pl.ds(t, 1)], sem).start()
        pltpu.make_async_copy(ybuf_ref.at[pl.ds(dest_ref[a + 1], 1)], rows_scr.at[1, pl.ds(t, 1)], sem).start()
        return carry

    lax.fori_loop(0, tm, body, 0, unroll=DMA_UNROLL)
    for k in range(2):
        pltpu.make_async_copy(ybuf_ref.at[pl.ds(0, tm)], rows_scr.at[k], sem).wait()

    o_ref[...] = x2_ref[...] + route_ref[:, 2:3] * rows_scr[0] + route_ref[:, 3:4] * rows_scr[1]


def _combine(dest, x2, route, ybuf):
    T = x2.shape[0]
    tm = COMBINE_TM
    grid_spec = pltpu.PrefetchScalarGridSpec(
        num_scalar_prefetch=1,
        grid=(T // tm,),
        in_specs=[
            pl.BlockSpec((tm, D_MODEL), lambda i, d: (i, 0)),
            pl.BlockSpec((tm, 128), lambda i, d: (i, 0)),
            pl.BlockSpec(memory_space=pl.ANY),
        ],
        out_specs=pl.BlockSpec((tm, D_MODEL), lambda i, d: (i, 0)),
        scratch_shapes=[pltpu.VMEM((2, tm, D_MODEL), F32), pltpu.SemaphoreType.DMA],
    )
    return pl.pallas_call(
        _combine_kernel,
        grid_spec=grid_spec,
        out_shape=jax.ShapeDtypeStruct((T, D_MODEL), F32),
        compiler_params=_params(("arbitrary",), 32),
        name="combine",
    )(dest, x2, route, ybuf)


def _moe_plan(route, max_items):
    e_flat = route[:, :2].astype(I32).reshape(-1)
    onehot = (e_flat[:, None] == jnp.arange(N_EXPERTS, dtype=I32)[None, :]).astype(I32)
    csum = jnp.cumsum(onehot, axis=0)
    counts = csum[-1]
    rank = jnp.sum(onehot * csum, axis=1) - 1
    n_it = (counts + ITEM_ROWS - 1) // ITEM_ROWS
    it_end = jnp.cumsum(n_it)
    it_start = it_end - n_it
    dest = jnp.sum(onehot * (it_start * ITEM_ROWS)[None, :], axis=1) + rank
    n_items = it_end[-1]
    w = jnp.arange(max_items, dtype=I32)
    ie = jnp.minimum(jnp.sum((it_end[None, :] <= w[:, None]).astype(I32), axis=1), N_EXPERTS - 1)
    used = w < n_items
    rows = jnp.where(used, jnp.minimum(ITEM_ROWS, counts[ie] - (w - it_start[ie]) * ITEM_ROWS), 0)
    ie = jnp.where(used, ie, ie[jnp.maximum(n_items - 1, 0)])
    return dest.astype(I32), ie.astype(I32), rows.astype(I32), n_items.reshape(1).astype(I32)


def _layer(x, mem, norm1_g, w_in, q_norm_g, k_norm_g, conv_w, mem_norm_g, w_mem_kv, mem_q_norm_g, mem_k_norm_g,
           out_norm_g, w_out, norm2_g, w_rg, b_rg, w_re, b_re, w_gate, w_up, w_down):
    B, S, D = x.shape
    T = B * S
    x2d = x.reshape(T, D)
    row = lambda v: v.reshape(1, -1).astype(F32)

    kvm = _memkv(mem.reshape(B * N_MEM, D), row(mem_norm_g), w_mem_kv, row(mem_k_norm_g))
    qkv, rest = _inproj(x2d, row(norm1_g), w_in.astype(BF16), row(q_norm_g), row(k_norm_g), row(mem_q_norm_g))

    slopes = 2.0 ** (-8.0 * jnp.arange(1, N_ATTN_HEADS + 1, dtype=F32) / N_ATTN_HEADS)
    attn = _attention(qkv, slopes, B, S)

    w_router = jnp.zeros((D, 128), F32).at[:, :N_GROUPS].set(w_rg).at[:, N_GROUPS:N_GROUPS + N_EXPERTS].set(w_re)
    b_router = jnp.zeros((1, 128), F32).at[0, :N_GROUPS].set(b_rg).at[0, N_GROUPS:N_GROUPS + N_EXPERTS].set(b_re)
    x2, h2, route = _mixer(attn, rest, kvm, x2d, w_out.astype(BF16), row(out_norm_g), row(norm2_g),
                           conv_w.astype(F32), w_router.astype(BF16), b_router, S)

    max_items = N_EXPERTS + (2 * T) // ITEM_ROWS
    dest, item_expert, item_rows, n_items = _moe_plan(route, max_items)
    xbuf = _dispatch(dest, h2, max_items * ITEM_ROWS)
    ybuf = _ffn(item_expert, item_rows, n_items, xbuf, w_gate, w_up, w_down)
    out = _combine(dest, x2, route, ybuf)
    return out.reshape(B, S, D)


def kernel(x, mem, norm1_g, w_in, q_norm_g, k_norm_g, conv_w, mem_norm_g, w_mem_kv, mem_q_norm_g, mem_k_norm_g,
           out_norm_g, w_out, norm2_g, w_router_group, b_router_group, w_router_expert, b_router_expert,
           w_gate, w_up, w_down):
    for l in range(norm1_g.shape[0]):
        x = _layer(x, mem, norm1_g[l], w_in[l], q_norm_g[l], k_norm_g[l], conv_w[l], mem_norm_g[l], w_mem_kv[l],
                   mem_q_norm_g[l], mem_k_norm_g[l], out_norm_g[l], w_out[l], norm2_g[l],
                   w_router_group[l], b_router_group[l], w_router_expert[l], b_router_expert[l],
                   w_gate[l], w_up[l], w_down[l])
    return x
```

```python
import functools

import jax
import jax.numpy as jnp
from jax import lax
from jax.experimental import pallas as pl
from jax.experimental.pallas import tpu as pltpu

F32 = jnp.float32
BF16 = jnp.bfloat16
I32 = jnp.int32

D_MODEL = 2048
HEAD_DIM = 128
N_ATTN_HEADS = 8
ATTN_W = N_ATTN_HEADS * HEAD_DIM
N_MEM_HEADS = 4
MEM_W = N_MEM_HEADS * HEAD_DIM
CONV_W = D_MODEL - ATTN_W - MEM_W
CONV_GROUPS = CONV_W // HEAD_DIM
N_MEM = 256
DILATIONS = (1, 4, 16)
SPAN = 128
PROJ_W = 3 * ATTN_W + 3 * CONV_W + MEM_W
N_GROUPS = 8
EXPERTS_PER_GROUP = 8
N_EXPERTS = N_GROUPS * EXPERTS_PER_GROUP
D_EXPERT = D_MODEL // 2
EPS = 1e-6
NEG_INF = -1e30
QK_SCALE = HEAD_DIM ** -0.5
LOG2E = 1.4426950408889634

MIB = 1024 * 1024
INPROJ_TM = 256
INPROJ_TN = 1024
ATTN_CHUNK = 2048
MIXER_TM = 256
DISPATCH_TM = 256
COMBINE_TM = 256
ITEM_ROWS = 1152
FFN_SUB = 256
FFN_PIECES = tuple((lo, min(FFN_SUB, ITEM_ROWS - lo)) for lo in range(0, ITEM_ROWS, FFN_SUB))
FFN_CHUNK = 256
HALO = 16
DMA_UNROLL = 8
ATTN_UNROLL = 16


def _params(semantics, vmem_mib):
    return pltpu.CompilerParams(dimension_semantics=semantics, vmem_limit_bytes=vmem_mib * MIB)


def _rms(v, width):
    return lax.rsqrt(jnp.sum(v * v, axis=-1, keepdims=True) * (1.0 / width) + EPS)


def _memkv_kernel(mem_ref, g_ref, w_ref, kg_ref, o_ref):
    j = pl.program_id(0)
    m = mem_ref[...]
    h = (m * _rms(m, D_MODEL) * g_ref[...]).astype(BF16)
    kv = jnp.dot(h, w_ref[...].astype(BF16), preferred_element_type=F32)
    is_key = j == 0
    for hh in range(N_MEM_HEADS):
        t = kv[:, hh * HEAD_DIM:(hh + 1) * HEAD_DIM]
        tn = t * _rms(t, HEAD_DIM) * kg_ref[...]
        o_ref[hh] = jnp.where(is_key, tn, t).astype(BF16)


def _memkv(mem2d, mem_norm_g, w_mem_kv, mem_k_norm_g):
    rows = mem2d.shape[0]
    return pl.pallas_call(
        _memkv_kernel,
        grid=(2,),
        in_specs=[
            pl.BlockSpec((rows, D_MODEL), lambda j: (0, 0)),
            pl.BlockSpec((1, D_MODEL), lambda j: (0, 0)),
            pl.BlockSpec((D_MODEL, MEM_W), lambda j: (0, j)),
            pl.BlockSpec((1, HEAD_DIM), lambda j: (0, 0)),
        ],
        out_specs=pl.BlockSpec((N_MEM_HEADS, rows, HEAD_DIM), lambda j: (j, 0, 0)),
        out_shape=jax.ShapeDtypeStruct((2 * N_MEM_HEADS, rows, HEAD_DIM), BF16),
        compiler_params=_params(("arbitrary",), 40),
        name="memkv",
    )(mem2d, mem_norm_g, w_mem_kv, mem_k_norm_g)


def _inproj_kernel(x_ref, g1_ref, w_ref, qg_ref, kg_ref, mqg_ref, qkv_ref, rest_ref):
    x = x_ref[...]
    h = (x * _rms(x, D_MODEL) * g1_ref[...]).astype(BF16)

    def normed(t, g):
        return t * _rms(t, HEAD_DIM) * g

    gq = qg_ref[...] * (QK_SCALE * LOG2E)
    gk = kg_ref[...]
    gmq = mqg_ref[...] * QK_SCALE
    groups_per_tile = INPROJ_TN // HEAD_DIM
    for j in range(PROJ_W // INPROJ_TN):
        y = jnp.dot(h, w_ref[:, j * INPROJ_TN:(j + 1) * INPROJ_TN], preferred_element_type=F32)
        for hh in range(groups_per_tile):
            t = y[:, hh * HEAD_DIM:(hh + 1) * HEAD_DIM]
            col = j * groups_per_tile + hh
            if col < 8:
                qkv_ref[col] = normed(t, gq)
            elif col < 16:
                qkv_ref[col] = normed(t, gk)
            elif col < 24:
                qkv_ref[col] = t
            elif col < 36:
                rest_ref[col - 24] = t.astype(BF16)
            else:
                rest_ref[col - 24] = normed(t, gmq).astype(BF16)


def _inproj(x2d, norm1_g, w_in_bf16, q_norm_g, k_norm_g, mem_q_norm_g):
    T = x2d.shape[0]
    tm = INPROJ_TM
    small = lambda n: pl.BlockSpec((1, n), lambda i: (0, 0))
    return pl.pallas_call(
        _inproj_kernel,
        grid=(T // tm,),
        in_specs=[
            pl.BlockSpec((tm, D_MODEL), lambda i: (i, 0)),
            small(D_MODEL),
            pl.BlockSpec((D_MODEL, PROJ_W), lambda i: (0, 0), pipeline_mode=pl.Buffered(1)),
            small(HEAD_DIM), small(HEAD_DIM), small(HEAD_DIM),
        ],
        out_specs=[
            pl.BlockSpec((24, tm, HEAD_DIM), lambda i: (0, i, 0)),
            pl.BlockSpec((16, tm, HEAD_DIM), lambda i: (0, i, 0)),
        ],
        out_shape=[
            jax.ShapeDtypeStruct((24, T, HEAD_DIM), F32),
            jax.ShapeDtypeStruct((16, T, HEAD_DIM), BF16),
        ],
        compiler_params=_params(("parallel",), 52),
        name="inproj",
    )(x2d, norm1_g, w_in_bf16, q_norm_g, k_norm_g, mem_q_norm_g)


def _attn_kernel(slopes_ref, q_ref, k_ref, v_ref, o_ref, kx, vx, o_scr, l_scr, bias_scr):
    h = pl.program_id(1)
    c = pl.program_id(2)
    C = ATTN_CHUNK

    @pl.when(c == 0)
    def _():
        kx[0:C, :] = jnp.zeros((C, HEAD_DIM), F32)
        vx[0:C, :] = jnp.zeros((C, HEAD_DIM), F32)

    @pl.when(c > 0)
    def _():
        kx[0:C, :] = kx[C:2 * C, :]
        vx[0:C, :] = vx[C:2 * C, :]

    kx[C:2 * C, :] = k_ref[0]
    vx[C:2 * C, :] = v_ref[0]

    slope = slopes_ref[h]
    ii = lax.broadcasted_iota(I32, (SPAN, 2 * SPAN), 0)
    jj = lax.broadcasted_iota(I32, (SPAN, 2 * SPAN), 1)
    dist = ii - jj + SPAN
    band = (dist >= 0) & (dist <= SPAN)
    distf = dist.astype(F32)

    for bi, d in enumerate(DILATIONS):
        neg = (-slope * float(d) * LOG2E) * distf
        bias_scr[2 * bi] = jnp.where(band, neg, NEG_INF)
        bias_scr[2 * bi + 1] = jnp.where(band & (jj >= SPAN), neg, NEG_INF)

    for bi, d in enumerate(DILATIONS):

        def unit(u, carry, d=d, bi=bi):
            r = u % d
            qb = u // d
            sq = qb * (SPAN * d) + r
            sk = C + sq - SPAN * d
            if d == 1:
                sq = pl.multiple_of(sq, SPAN)
                sk = pl.multiple_of(sk, SPAN)
                qi = pl.ds(sq, SPAN)
                ki = pl.ds(sk, 2 * SPAN)
            else:
                qi = pl.ds(sq, SPAN, stride=d)
                ki = pl.ds(sk, 2 * SPAN, stride=d)
            first = jnp.logical_and(c == 0, qb == 0).astype(I32)
            q = q_ref[0, qi, :].astype(BF16)
            k = kx[ki, :].astype(BF16)
            v = vx[ki, :].astype(BF16)
            s = lax.dot_general(q, k, (((1,), (1,)), ((), ())), preferred_element_type=F32)
            s = s + bias_scr[2 * bi + first]
            m = jnp.max(s, axis=-1, keepdims=True)
            p = jnp.exp2(s - m)
            l = jnp.sum(p, axis=-1, keepdims=True)
            o = jnp.dot(p.astype(BF16), v, preferred_element_type=F32) * (1.0 / l)
            o_scr[bi, qi, :] = o
            l_scr[bi, qi, :] = jnp.broadcast_to(m + jnp.log2(l), (SPAN, HEAD_DIM))
            return carry

        lax.fori_loop(0, C // SPAN, unit, 0, unroll=ATTN_UNROLL)

    l0, l1, l2 = l_scr[0], l_scr[1], l_scr[2]
    mx = jnp.maximum(jnp.maximum(l0, l1), l2)
    e0, e1, e2 = jnp.exp2(l0 - mx), jnp.exp2(l1 - mx), jnp.exp2(l2 - mx)
    out = (e0 * o_scr[0] + e1 * o_scr[1] + e2 * o_scr[2]) * (1.0 / (e0 + e1 + e2))
    o_ref[0] = out.astype(BF16)


def _attention(qkv, slopes, batch, seq):
    T = qkv.shape[1]
    C = ATTN_CHUNK
    nc = seq // C
    blk = lambda off: pl.BlockSpec((1, C, HEAD_DIM), lambda b, h, c, s: (h + off, b * nc + c, 0))
    grid_spec = pltpu.PrefetchScalarGridSpec(
        num_scalar_prefetch=1,
        grid=(batch, N_ATTN_HEADS, nc),
        in_specs=[blk(0), blk(8), blk(16)],
        out_specs=pl.BlockSpec((1, C, HEAD_DIM), lambda b, h, c, s: (h, b * nc + c, 0)),
        scratch_shapes=[
            pltpu.VMEM((2 * C, HEAD_DIM), F32),
            pltpu.VMEM((2 * C, HEAD_DIM), F32),
            pltpu.VMEM((3, C, HEAD_DIM), F32),
            pltpu.VMEM((3, C, HEAD_DIM), F32),
            pltpu.VMEM((2 * len(DILATIONS), SPAN, 2 * SPAN), F32),
        ],
    )
    return pl.pallas_call(
        _attn_kernel,
        grid_spec=grid_spec,
        out_shape=jax.ShapeDtypeStruct((N_ATTN_HEADS, T, HEAD_DIM), BF16),
        compiler_params=_params(("parallel", "parallel", "arbitrary"), 40),
        name="attn",
    )(slopes, qkv, qkv, qkv)


def _mixer_kernel(attn_ref, rest_ref, halo_ref, kvm_ref, x_ref, wout_ref, og_ref, g2_ref, cw_ref, wr_ref, br_ref,
                  x2_ref, h2_ref, route_ref, y_scr, z_scr, *, tiles_per_seq):
    i = pl.program_id(0)
    tm = MIXER_TM
    og = og_ref[...]

    ss = jnp.zeros((tm, 1), F32)
    for hh in range(N_ATTN_HEADS):
        a = attn_ref[hh].astype(F32)
        ss = ss + jnp.sum(a * a, axis=-1, keepdims=True)
    r = lax.rsqrt(ss * (1.0 / ATTN_W) + EPS)
    for hh in range(N_ATTN_HEADS):
        lo = hh * HEAD_DIM
        y_scr[:, lo:lo + HEAD_DIM] = (attn_ref[hh].astype(F32) * r * og[:, lo:lo + HEAD_DIM]).astype(BF16)

    seq_start = (i % tiles_per_seq) == 0
    convs = []
    ss = jnp.zeros((tm, 1), F32)
    for g in range(CONV_GROUPS):
        bg = rest_ref[g].astype(F32)
        z = rest_ref[CONV_GROUPS + g].astype(F32) * rest_ref[2 * CONV_GROUPS + g].astype(F32)
        zh = halo_ref[CONV_GROUPS + g].astype(F32) * halo_ref[2 * CONV_GROUPS + g].astype(F32)
        z_scr[g, 0:HALO, :] = jnp.where(seq_start, 0.0, zh)
        z_scr[g, HALO:HALO + tm, :] = z
        z1 = z_scr[g, HALO - 1:HALO - 1 + tm, :]
        z2 = z_scr[g, HALO - 2:HALO - 2 + tm, :]
        lo = g * HEAD_DIM
        w0 = cw_ref[0:1, lo:lo + HEAD_DIM]
        w1 = cw_ref[1:2, lo:lo + HEAD_DIM]
        w2 = cw_ref[2:3, lo:lo + HEAD_DIM]
        cv = bg * (w2 * z + w1 * z1 + w0 * z2)
        convs.append(cv)
        ss = ss + jnp.sum(cv * cv, axis=-1, keepdims=True)
    r = lax.rsqrt(ss * (1.0 / CONV_W) + EPS)
    for g in range(CONV_GROUPS):
        lo = ATTN_W + g * HEAD_DIM
        y_scr[:, lo:lo + HEAD_DIM] = (convs[g] * r * og[:, lo:lo + HEAD_DIM]).astype(BF16)

    mems = []
    ss = jnp.zeros((tm, 1), F32)
    for hh in range(N_MEM_HEADS):
        mq = rest_ref[3 * CONV_GROUPS + hh]
        s = lax.dot_general(mq, kvm_ref[hh], (((1,), (1,)), ((), ())), preferred_element_type=F32)
        m = jnp.max(s, axis=-1, keepdims=True)
        p = jnp.exp(s - m)
        l = jnp.sum(p, axis=-1, keepdims=True)
        o = jnp.dot(p.astype(BF16), kvm_ref[N_MEM_HEADS + hh], preferred_element_type=F32) * (1.0 / l)
        mems.append(o)
        ss = ss + jnp.sum(o * o, axis=-1, keepdims=True)
    r = lax.rsqrt(ss * (1.0 / MEM_W) + EPS)
    for hh in range(N_MEM_HEADS):
        lo = ATTN_W + CONV_W + hh * HEAD_DIM
        y_scr[:, lo:lo + HEAD_DIM] = (mems[hh] * r * og[:, lo:lo + HEAD_DIM]).astype(BF16)

    x2 = x_ref[...] + jnp.dot(y_scr[...], wout_ref[...], preferred_element_type=F32)
    x2_ref[...] = x2
    h2 = x2 * _rms(x2, D_MODEL) * g2_ref[...]
    h2_ref[...] = h2

    lg = jnp.dot(h2.astype(BF16), wr_ref[...], preferred_element_type=F32) + br_ref[...]
    lane = lax.broadcasted_iota(I32, (tm, 128), 1).astype(F32)
    gl = jnp.where(lane < N_GROUPS, lg, NEG_INF)
    gmax = jnp.max(gl, axis=-1, keepdims=True)
    g_w = 1.0 / jnp.sum(jnp.exp(gl - gmax), axis=-1, keepdims=True)
    g_idx = jnp.min(jnp.where(gl == gmax, lane, 1e9), axis=-1, keepdims=True)
    e_lo = N_GROUPS + EXPERTS_PER_GROUP * g_idx
    el = jnp.where((lane >= e_lo) & (lane < e_lo + EXPERTS_PER_GROUP), lg, NEG_INF)
    e1 = jnp.max(el, axis=-1, keepdims=True)
    i1 = jnp.min(jnp.where(el == e1, lane, 1e9), axis=-1, keepdims=True)
    el2 = jnp.where(lane == i1, NEG_INF, el)
    e2 = jnp.max(el2, axis=-1, keepdims=True)
    i2 = jnp.min(jnp.where(el2 == e2, lane, 1e9), axis=-1, keepdims=True)
    t = jnp.exp(e2 - e1)
    p1 = 1.0 / (1.0 + t)
    p2 = t * p1
    route = jnp.where(lane == 0, i1 - N_GROUPS,
                      jnp.where(lane == 1, i2 - N_GROUPS,
                                jnp.where(lane == 2, g_w * p1,
                                          jnp.where(lane == 3, g_w * p2, 0.0))))
    route_ref[...] = route


def _mixer(attn, rest, kvm, x2d, w_out_bf16, out_norm_g, norm2_g, conv_w, w_router, b_router, seq):
    T = x2d.shape[0]
    tm = MIXER_TM
    tiles_per_seq = seq // tm
    full = lambda shape: pl.BlockSpec(shape, lambda i: tuple(0 for _ in shape))
    return pl.pallas_call(
        functools.partial(_mixer_kernel, tiles_per_seq=tiles_per_seq),
        grid=(T // tm,),
        in_specs=[
            pl.BlockSpec((N_ATTN_HEADS, tm, HEAD_DIM), lambda i: (0, i, 0)),
            pl.BlockSpec((16, tm, HEAD_DIM), lambda i: (0, i, 0)),
            pl.BlockSpec((16, HALO, HEAD_DIM), lambda i: (0, jnp.maximum(i * (tm // HALO) - 1, 0), 0)),
            pl.BlockSpec((2 * N_MEM_HEADS, N_MEM, HEAD_DIM), lambda i: (0, i // tiles_per_seq, 0)),
            pl.BlockSpec((tm, D_MODEL), lambda i: (i, 0)),
            full((D_MODEL, D_MODEL)),
            full((1, D_MODEL)),
            full((1, D_MODEL)),
            full((3, CONV_W)),
            full((D_MODEL, 128)),
            full((1, 128)),
        ],
        out_specs=[
            pl.BlockSpec((tm, D_MODEL), lambda i: (i, 0)),
            pl.BlockSpec((tm, D_MODEL), lambda i: (i, 0)),
            pl.BlockSpec((tm, 128), lambda i: (i, 0)),
        ],
        out_shape=[
            jax.ShapeDtypeStruct((T, D_MODEL), F32),
            jax.ShapeDtypeStruct((T, D_MODEL), F32),
            jax.ShapeDtypeStruct((T, 128), F32),
        ],
        scratch_shapes=[
            pltpu.VMEM((tm, D_MODEL), BF16),
            pltpu.VMEM((CONV_GROUPS, HALO + tm, HEAD_DIM), F32),
        ],
        compiler_params=_params(("arbitrary",), 48),
        name="mixer",
    )(attn, rest, rest, kvm, x2d, w_out_bf16, out_norm_g, norm2_g, conv_w, w_router, b_router)


def _dispatch_kernel(dest_ref, h_ref, xbuf_ref, sem):
    i = pl.program_id(0)
    tm = DISPATCH_TM
    base = i * tm

    def body(t, carry):
        a = 2 * (base + t)
        src = h_ref.at[pl.ds(t, 1)]
        pltpu.make_async_copy(src, xbuf_ref.at[pl.ds(dest_ref[a], 1)], sem).start()
        pltpu.make_async_copy(src, xbuf_ref.at[pl.ds(dest_ref[a + 1], 1)], sem).start()
        return carry

    lax.fori_loop(0, tm, body, 0, unroll=DMA_UNROLL)
    pltpu.make_async_copy(xbuf_ref.at[pl.ds(0, 2 * tm)], xbuf_ref.at[pl.ds(0, 2 * tm)], sem).wait()


def _dispatch(dest, h2, n_rows):
    T = h2.shape[0]
    tm = DISPATCH_TM
    grid_spec = pltpu.PrefetchScalarGridSpec(
        num_scalar_prefetch=1,
        grid=(T // tm,),
        in_specs=[pl.BlockSpec((tm, D_MODEL), lambda i, d: (i, 0))],
        out_specs=pl.BlockSpec(memory_space=pl.ANY),
        scratch_shapes=[pltpu.SemaphoreType.DMA],
    )
    return pl.pallas_call(
        _dispatch_kernel,
        grid_spec=grid_spec,
        out_shape=jax.ShapeDtypeStruct((n_rows, D_MODEL), F32),
        compiler_params=_params(("arbitrary",), 32),
        name="dispatch",
    )(dest, h2)


def _ffn_kernel(ie_ref, rows_ref, x_ref, wg_ref, wu_ref, wd_ref, y_ref, xg, xb, acc, xsem, ysem):
    w = pl.program_id(0)
    c = pl.program_id(1)
    n_chunks = D_EXPERT // FFN_CHUNK

    def subs(v):
        return (v + FFN_SUB - 1) // FFN_SUB

    rows = rows_ref[w]
    ns = subs(rows)
    ns_next = subs(rows_ref[w + 1])
    ns_prev2 = jnp.where(w > 1, subs(rows_ref[jnp.maximum(w - 2, 0)]), 0)
    p = w % 2

    def x_copy(item, s):
        lo, n = FFN_PIECES[s]
        return pltpu.make_async_copy(x_ref.at[pl.ds(item * ITEM_ROWS + lo, n)], xg.at[pl.ds(lo, n)], xsem)

    def y_copy(item, slot, s):
        lo, n = FFN_PIECES[s]
        return pltpu.make_async_copy(acc.at[slot, pl.ds(lo, n)], y_ref.at[pl.ds(item * ITEM_ROWS + lo, n)],
                                     ysem.at[slot])

    def for_pieces(count, fn):
        for s in range(len(FFN_PIECES)):
            @pl.when(s < count)
            def _(s=s):
                fn(s)

    @pl.when(c == 0)
    def _():
        @pl.when(w == 0)
        def _():
            for_pieces(ns, lambda s: x_copy(0, s).start())

        for_pieces(ns, lambda s: x_copy(w, s).wait())
        for_pieces(ns_prev2, lambda s: y_copy(w - 2, p, s).wait())

    for ns_static in range(1, len(FFN_PIECES) + 1):
        nr = FFN_PIECES[ns_static - 1][0] + FFN_PIECES[ns_static - 1][1]

        @pl.when(ns == ns_static)
        def _(nr=nr):
            @pl.when(c == 0)
            def _():
                ok = lax.broadcasted_iota(I32, (nr, 1), 0) < rows
                xb[0:nr, :] = jnp.where(ok, xg[0:nr, :], 0.0).astype(BF16)
                acc[p, 0:nr, :] = jnp.zeros((nr, D_MODEL), F32)

            wg = wg_ref[0].astype(BF16)
            wu = wu_ref[0].astype(BF16)
            wd = wd_ref[0].astype(BF16)
            x = xb[0:nr, :]
            a = jnp.dot(x, wg, preferred_element_type=F32)
            u = jnp.dot(x, wu, preferred_element_type=F32)
            hm = (a / (1.0 + jnp.exp(-a)) * u).astype(BF16)
            acc[p, 0:nr, :] += jnp.dot(hm, wd, preferred_element_type=F32)

    @pl.when(c == 1)
    def _():
        for_pieces(ns_next, lambda s: x_copy(w + 1, s).start())

    @pl.when(c == n_chunks - 1)
    def _():
        for_pieces(ns, lambda s: y_copy(w, p, s).start())


def _ffn(item_expert, item_rows, xbuf, w_gate, w_up, w_down):
    n_chunks = D_EXPERT // FFN_CHUNK
    grid_items = item_rows.shape[0] - 1

    def chunk_of(w, c, ir):
        return jnp.where(ir[w] > 0, c, n_chunks - 1)

    grid_spec = pltpu.PrefetchScalarGridSpec(
        num_scalar_prefetch=2,
        grid=(grid_items, n_chunks),
        in_specs=[
            pl.BlockSpec(memory_space=pl.ANY),
            pl.BlockSpec((1, D_MODEL, FFN_CHUNK), lambda w, c, ie, ir: (ie[w], 0, chunk_of(w, c, ir))),
            pl.BlockSpec((1, D_MODEL, FFN_CHUNK), lambda w, c, ie, ir: (ie[w], 0, chunk_of(w, c, ir))),
            pl.BlockSpec((1, FFN_CHUNK, D_MODEL), lambda w, c, ie, ir: (ie[w], chunk_of(w, c, ir), 0)),
        ],
        out_specs=pl.BlockSpec(memory_space=pl.ANY),
        scratch_shapes=[
            pltpu.VMEM((ITEM_ROWS, D_MODEL), F32),
            pltpu.VMEM((ITEM_ROWS, D_MODEL), BF16),
            pltpu.VMEM((2, ITEM_ROWS, D_MODEL), F32),
            pltpu.SemaphoreType.DMA,
            pltpu.SemaphoreType.DMA((2,)),
        ],
    )
    return pl.pallas_call(
        _ffn_kernel,
        grid_spec=grid_spec,
        out_shape=jax.ShapeDtypeStruct(xbuf.shape, F32),
        compiler_params=_params(("arbitrary", "arbitrary"), 58),
        name="ffn",
    )(item_expert, item_rows, xbuf, w_gate, w_up, w_down)


def _combine_kernel(dest_ref, x2_ref, route_ref, ybuf_ref, o_ref, rows_scr, sem):
    i = pl.program_id(0)
    tm = COMBINE_TM
    base = i * tm

    def body(t, carry):
        a = 2 * (base + t)
        pltpu.make_async_copy(ybuf_ref.at[pl.ds(dest_ref[a], 1)], rows_scr.at[0, pl.ds(t, 1)], sem).start()
        pltpu.make_async_copy(ybuf_ref.at[pl.ds(dest_ref[a + 1], 1)], rows_scr.at[1, pl.ds(t, 1)], sem).start()
        return carry

    lax.fori_loop(0, tm, body, 0, unroll=DMA_UNROLL)
    for k in range(2):
        pltpu.make_async_copy(ybuf_ref.at[pl.ds(0, tm)], rows_scr.at[k], sem).wait()

    o_ref[...] = x2_ref[...] + route_ref[:, 2:3] * rows_scr[0] + route_ref[:, 3:4] * rows_scr[1]


def _combine(dest, x2, route, ybuf):
    T = x2.shape[0]
    tm = COMBINE_TM
    grid_spec = pltpu.PrefetchScalarGridSpec(
        num_scalar_prefetch=1,
        grid=(T // tm,),
        in_specs=[
            pl.BlockSpec((tm, D_MODEL), lambda i, d: (i, 0)),
            pl.BlockSpec((tm, 128), lambda i, d: (i, 0)),
            pl.BlockSpec(memory_space=pl.ANY),
        ],
        out_specs=pl.BlockSpec((tm, D_MODEL), lambda i, d: (i, 0)),
        scratch_shapes=[pltpu.VMEM((2, tm, D_MODEL), F32), pltpu.SemaphoreType.DMA],
    )
    return pl.pallas_call(
        _combine_kernel,
        grid_spec=grid_spec,
        out_shape=jax.ShapeDtypeStruct((T, D_MODEL), F32),
        compiler_params=_params(("arbitrary",), 32),
        name="combine",
    )(dest, x2, route, ybuf)


def _moe_plan(route, max_items):
    e_flat = route[:, :2].astype(I32).reshape(-1)
    onehot = (e_flat[:, None] == jnp.arange(N_EXPERTS, dtype=I32)[None, :]).astype(I32)
    csum = jnp.cumsum(onehot, axis=0)
    counts = csum[-1]
    rank = jnp.sum(onehot * csum, axis=1) - 1
    n_it = (counts + ITEM_ROWS - 1) // ITEM_ROWS
    it_end = jnp.cumsum(n_it)
    it_start = it_end - n_it
    dest = jnp.sum(onehot * (it_start * ITEM_ROWS)[None, :], axis=1) + rank
    n_items = it_end[-1]
    w = jnp.arange(max_items + 3, dtype=I32)
    ie = jnp.minimum(jnp.sum((it_end[None, :] <= w[:, None]).astype(I32), axis=1), N_EXPERTS - 1)
    used = w < n_items
    rows = jnp.where(used, jnp.minimum(ITEM_ROWS, counts[ie] - (w - it_start[ie]) * ITEM_ROWS), 0)
    ie = jnp.where(used, ie, ie[jnp.maximum(n_items - 1, 0)])
    return dest.astype(I32), ie.astype(I32), rows.astype(I32)


def _layer(x, mem, norm1_g, w_in, q_norm_g, k_norm_g, conv_w, mem_norm_g, w_mem_kv, mem_q_norm_g, mem_k_norm_g,
           out_norm_g, w_out, norm2_g, w_rg, b_rg, w_re, b_re, w_gate, w_up, w_down):
    B, S, D = x.shape
    T = B * S
    x2d = x.reshape(T, D)
    row = lambda v: v.reshape(1, -1).astype(F32)

    kvm = _memkv(mem.reshape(B * N_MEM, D), row(mem_norm_g), w_mem_kv, row(mem_k_norm_g))
    qkv, rest = _inproj(x2d, row(norm1_g), w_in.astype(BF16), row(q_norm_g), row(k_norm_g), row(mem_q_norm_g))

    slopes = 2.0 ** (-8.0 * jnp.arange(1, N_ATTN_HEADS + 1, dtype=F32) / N_ATTN_HEADS)
    attn = _attention(qkv, slopes, B, S)

    w_router = jnp.zeros((D, 128), F32).at[:, :N_GROUPS].set(w_rg).at[:, N_GROUPS:N_GROUPS + N_EXPERTS].set(w_re)
    b_router = jnp.zeros((1, 128), F32).at[0, :N_GROUPS].set(b_rg).at[0, N_GROUPS:N_GROUPS + N_EXPERTS].set(b_re)
    x2, h2, route = _mixer(attn, rest, kvm, x2d, w_out.astype(BF16), row(out_norm_g), row(norm2_g),
                           conv_w.astype(F32), w_router.astype(BF16), b_router, S)

    max_items = N_EXPERTS + (2 * T) // ITEM_ROWS
    dest, item_expert, item_rows = _moe_plan(route, max_items)
    xbuf = _dispatch(dest, h2, max_items * ITEM_ROWS)
    ybuf = _ffn(item_expert, item_rows, xbuf, w_gate, w_up, w_down)
    out = _combine(dest, x2, route, ybuf)
    return out.reshape(B, S, D)


def kernel(x, mem, norm1_g, w_in, q_norm_g, k_norm_g, conv_w, mem_norm_g, w_mem_kv, mem_q_norm_g, mem_k_norm_g,
           out_norm_g, w_out, norm2_g, w_router_group, b_router_group, w_router_expert, b_router_expert,
           w_gate, w_up, w_down):
    for l in range(norm1_g.shape[0]):
        x = _layer(x, mem, norm1_g[l], w_in[l], q_norm_g[l], k_norm_g[l], conv_w[l], mem_norm_g[l], w_mem_kv[l],
                   mem_q_norm_g[l], mem_k_norm_g[l], out_norm_g[l], w_out[l], norm2_g[l],
                   w_router_group[l], b_router_group[l], w_router_expert[l], b_router_expert[l],
                   w_gate[l], w_up[l], w_down[l])
    return x
```

```python
import functools

import jax
import jax.numpy as jnp
from jax import lax
from jax.experimental import pallas as pl
from jax.experimental.pallas import tpu as pltpu

F32 = jnp.float32
BF16 = jnp.bfloat16
I32 = jnp.int32

D_MODEL = 2048
HEAD_DIM = 128
N_ATTN_HEADS = 8
ATTN_W = N_ATTN_HEADS * HEAD_DIM
N_MEM_HEADS = 4
MEM_W = N_MEM_HEADS * HEAD_DIM
CONV_W = D_MODEL - ATTN_W - MEM_W
CONV_GROUPS = CONV_W // HEAD_DIM
N_MEM = 256
DILATIONS = (1, 4, 16)
SPAN = 128
PROJ_W = 3 * ATTN_W + 3 * CONV_W + MEM_W
N_GROUPS = 8
EXPERTS_PER_GROUP = 8
N_EXPERTS = N_GROUPS * EXPERTS_PER_GROUP
D_EXPERT = D_MODEL // 2
EPS = 1e-6
NEG_INF = -1e30
QK_SCALE = HEAD_DIM ** -0.5
LOG2E = 1.4426950408889634

MIB = 1024 * 1024
INPROJ_TM = 256
INPROJ_TN = 1024
ATTN_CHUNK = 2048
MIXER_TM = 512
DISPATCH_TM = 256
COMBINE_TM = 256
ITEM_ROWS = 1152
FFN_SUB = 256
FFN_PIECES = tuple((lo, min(FFN_SUB, ITEM_ROWS - lo)) for lo in range(0, ITEM_ROWS, FFN_SUB))
FFN_CHUNK = 256
HALO = 16
DMA_UNROLL = 8
ATTN_UNROLL = 16


def _params(semantics, vmem_mib):
    return pltpu.CompilerParams(dimension_semantics=semantics, vmem_limit_bytes=vmem_mib * MIB)


def _rms(v, width):
    return lax.rsqrt(jnp.sum(v * v, axis=-1, keepdims=True) * (1.0 / width) + EPS)


def _memkv_kernel(mem_ref, g_ref, w_ref, kg_ref, o_ref):
    j = pl.program_id(0)
    m = mem_ref[...]
    h = (m * _rms(m, D_MODEL) * g_ref[...]).astype(BF16)
    kv = jnp.dot(h, w_ref[...].astype(BF16), preferred_element_type=F32)
    is_key = j == 0
    for hh in range(N_MEM_HEADS):
        t = kv[:, hh * HEAD_DIM:(hh + 1) * HEAD_DIM]
        tn = t * _rms(t, HEAD_DIM) * kg_ref[...]
        o_ref[hh] = jnp.where(is_key, tn, t).astype(BF16)


def _memkv(mem2d, mem_norm_g, w_mem_kv, mem_k_norm_g):
    rows = mem2d.shape[0]
    return pl.pallas_call(
        _memkv_kernel,
        grid=(2,),
        in_specs=[
            pl.BlockSpec((rows, D_MODEL), lambda j: (0, 0)),
            pl.BlockSpec((1, D_MODEL), lambda j: (0, 0)),
            pl.BlockSpec((D_MODEL, MEM_W), lambda j: (0, j)),
            pl.BlockSpec((1, HEAD_DIM), lambda j: (0, 0)),
        ],
        out_specs=pl.BlockSpec((N_MEM_HEADS, rows, HEAD_DIM), lambda j: (j, 0, 0)),
        out_shape=jax.ShapeDtypeStruct((2 * N_MEM_HEADS, rows, HEAD_DIM), BF16),
        compiler_params=_params(("arbitrary",), 40),
        name="memkv",
    )(mem2d, mem_norm_g, w_mem_kv, mem_k_norm_g)


def _inproj_kernel(x_ref, g1_ref, w_ref, qg_ref, kg_ref, mqg_ref, qkv_ref, rest_ref):
    x = x_ref[...]
    h = (x * _rms(x, D_MODEL) * g1_ref[...]).astype(BF16)

    def normed(t, g):
        return t * _rms(t, HEAD_DIM) * g

    gq = qg_ref[...] * (QK_SCALE * LOG2E)
    gk = kg_ref[...]
    gmq = mqg_ref[...] * QK_SCALE
    groups_per_tile = INPROJ_TN // HEAD_DIM
    for j in range(PROJ_W // INPROJ_TN):
        y = jnp.dot(h, w_ref[:, j * INPROJ_TN:(j + 1) * INPROJ_TN], preferred_element_type=F32)
        for hh in range(groups_per_tile):
            t = y[:, hh * HEAD_DIM:(hh + 1) * HEAD_DIM]
            col = j * groups_per_tile + hh
            if col < 8:
                qkv_ref[col] = normed(t, gq)
            elif col < 16:
                qkv_ref[col] = normed(t, gk)
            elif col < 24:
                qkv_ref[col] = t
            elif col < 36:
                rest_ref[col - 24] = t.astype(BF16)
            else:
                rest_ref[col - 24] = normed(t, gmq).astype(BF16)


def _inproj(x2d, norm1_g, w_in_bf16, q_norm_g, k_norm_g, mem_q_norm_g):
    T = x2d.shape[0]
    tm = INPROJ_TM
    small = lambda n: pl.BlockSpec((1, n), lambda i: (0, 0))
    return pl.pallas_call(
        _inproj_kernel,
        grid=(T // tm,),
        in_specs=[
            pl.BlockSpec((tm, D_MODEL), lambda i: (i, 0)),
            small(D_MODEL),
            pl.BlockSpec((D_MODEL, PROJ_W), lambda i: (0, 0), pipeline_mode=pl.Buffered(1)),
            small(HEAD_DIM), small(HEAD_DIM), small(HEAD_DIM),
        ],
        out_specs=[
            pl.BlockSpec((24, tm, HEAD_DIM), lambda i: (0, i, 0)),
            pl.BlockSpec((16, tm, HEAD_DIM), lambda i: (0, i, 0)),
        ],
        out_shape=[
            jax.ShapeDtypeStruct((24, T, HEAD_DIM), F32),
            jax.ShapeDtypeStruct((16, T, HEAD_DIM), BF16),
        ],
        compiler_params=_params(("parallel",), 52),
        name="inproj",
    )(x2d, norm1_g, w_in_bf16, q_norm_g, k_norm_g, mem_q_norm_g)


def _attn_kernel(slopes_ref, q_ref, k_ref, v_ref, o_ref,
                 kx, vx, q4, k4, v4, o1, l1, o4, l4, o16, l16, out_scr, bias_scr):
    h = pl.program_id(1)
    c = pl.program_id(2)
    C = ATTN_CHUNK

    @pl.when(c == 0)
    def _():
        kx[0:C, :] = jnp.zeros((C, HEAD_DIM), F32)
        vx[0:C, :] = jnp.zeros((C, HEAD_DIM), F32)

    @pl.when(c > 0)
    def _():
        kx[0:C, :] = kx[C:2 * C, :]
        vx[0:C, :] = vx[C:2 * C, :]

    kx[C:2 * C, :] = k_ref[0]
    vx[C:2 * C, :] = v_ref[0]

    for b in range(4):
        q4[b] = q_ref[0, pl.ds(b, C // 4, stride=4), :]
        k4[b] = kx[pl.ds(b, 2 * C // 4, stride=4), :]
        v4[b] = vx[pl.ds(b, 2 * C // 4, stride=4), :]

    slope = slopes_ref[h]
    ii = lax.broadcasted_iota(I32, (SPAN, 2 * SPAN), 0)
    jj = lax.broadcasted_iota(I32, (SPAN, 2 * SPAN), 1)
    dist = ii - jj + SPAN
    band = (dist >= 0) & (dist <= SPAN)
    distf = dist.astype(F32)

    for bi, d in enumerate(DILATIONS):
        neg = (-slope * float(d) * LOG2E) * distf
        bias_scr[2 * bi] = jnp.where(band, neg, NEG_INF)
        bias_scr[2 * bi + 1] = jnp.where(band & (jj >= SPAN), neg, NEG_INF)

    ones = jnp.ones((2 * SPAN, HEAD_DIM), BF16)

    def attend(q, k, v, bias):
        s = lax.dot_general(q.astype(BF16), k.astype(BF16), (((1,), (1,)), ((), ())),
                            preferred_element_type=F32) + bias
        m = jnp.max(s, axis=-1, keepdims=True)
        p = jnp.exp2(s - m).astype(BF16)
        ol = jnp.dot(p, jnp.concatenate([v.astype(BF16), ones], axis=1), preferred_element_type=F32)
        l = ol[:, HEAD_DIM:]
        return ol[:, :HEAD_DIM] * (1.0 / l), m + jnp.log2(l)

    def unit1(u, carry):
        sq = pl.multiple_of(u * SPAN, SPAN)
        sk = pl.multiple_of(C + u * SPAN - SPAN, SPAN)
        first = jnp.logical_and(c == 0, u == 0).astype(I32)
        o, lse = attend(q_ref[0, pl.ds(sq, SPAN), :], kx[pl.ds(sk, 2 * SPAN), :], vx[pl.ds(sk, 2 * SPAN), :],
                        bias_scr[first])
        o1[pl.ds(sq, SPAN), :] = o
        l1[pl.ds(sq, SPAN), :] = lse
        return carry

    def unit4(u, carry):
        b = u % 4
        qb = u // 4
        sq = pl.multiple_of(qb * SPAN, SPAN)
        sk = pl.multiple_of(C // 4 + qb * SPAN - SPAN, SPAN)
        first = jnp.logical_and(c == 0, qb == 0).astype(I32)
        o, lse = attend(q4[b, pl.ds(sq, SPAN), :], k4[b, pl.ds(sk, 2 * SPAN), :], v4[b, pl.ds(sk, 2 * SPAN), :],
                        bias_scr[2 + first])
        o4[b, pl.ds(sq, SPAN), :] = o
        l4[b, pl.ds(sq, SPAN), :] = lse
        return carry

    def unit16(u, carry):
        b = u % 4
        a = u // 4
        first = (c == 0).astype(I32)
        qi = pl.ds(a, SPAN, stride=4)
        ki = pl.ds(a, 2 * SPAN, stride=4)
        o, lse = attend(q4[b, qi, :], k4[b, ki, :], v4[b, ki, :], bias_scr[4 + first])
        o16[b, qi, :] = o
        l16[b, qi, :] = lse
        return carry

    for unit in (unit1, unit4, unit16):
        lax.fori_loop(0, C // SPAN, unit, 0, unroll=ATTN_UNROLL)

    for b in range(4):
        rows_b = pl.ds(b, C // 4, stride=4)
        la, lb, lc = l1[rows_b, :], l4[b], l16[b]
        mx = jnp.maximum(jnp.maximum(la, lb), lc)
        ea, eb, ec = jnp.exp2(la - mx), jnp.exp2(lb - mx), jnp.exp2(lc - mx)
        out_scr[rows_b, :] = (ea * o1[rows_b, :] + eb * o4[b] + ec * o16[b]) * (1.0 / (ea + eb + ec))
    o_ref[0] = out_scr[...].astype(BF16)


def _attention(qkv, slopes, batch, seq):
    T = qkv.shape[1]
    C = ATTN_CHUNK
    nc = seq // C
    blk = lambda off: pl.BlockSpec((1, C, HEAD_DIM), lambda b, h, c, s: (h + off, b * nc + c, 0))
    grid_spec = pltpu.PrefetchScalarGridSpec(
        num_scalar_prefetch=1,
        grid=(batch, N_ATTN_HEADS, nc),
        in_specs=[blk(0), blk(8), blk(16)],
        out_specs=pl.BlockSpec((1, C, HEAD_DIM), lambda b, h, c, s: (h, b * nc + c, 0)),
        scratch_shapes=[
            pltpu.VMEM((2 * C, HEAD_DIM), F32),
            pltpu.VMEM((2 * C, HEAD_DIM), F32),
            pltpu.VMEM((4, C // 4, HEAD_DIM), F32),
            pltpu.VMEM((4, 2 * C // 4, HEAD_DIM), F32),
            pltpu.VMEM((4, 2 * C // 4, HEAD_DIM), F32),
            pltpu.VMEM((C, HEAD_DIM), F32),
            pltpu.VMEM((C, HEAD_DIM), F32),
            pltpu.VMEM((4, C // 4, HEAD_DIM), F32),
            pltpu.VMEM((4, C // 4, HEAD_DIM), F32),
            pltpu.VMEM((4, C // 4, HEAD_DIM), F32),
            pltpu.VMEM((4, C // 4, HEAD_DIM), F32),
            pltpu.VMEM((C, HEAD_DIM), F32),
            pltpu.VMEM((2 * len(DILATIONS), SPAN, 2 * SPAN), F32),
        ],
    )
    return pl.pallas_call(
        _attn_kernel,
        grid_spec=grid_spec,
        out_shape=jax.ShapeDtypeStruct((N_ATTN_HEADS, T, HEAD_DIM), BF16),
        compiler_params=_params(("parallel", "parallel", "arbitrary"), 40),
        name="attn",
    )(slopes, qkv, qkv, qkv)


def _mixer_kernel(attn_ref, rest_ref, halo_ref, kvm_ref, x_ref, wout_ref, og_ref, g2_ref, cw_ref, wr_ref, br_ref,
                  x2_ref, h2_ref, route_ref, y_scr, z_scr, *, tiles_per_seq):
    i = pl.program_id(0)
    tm = MIXER_TM
    og = og_ref[...]

    ss = jnp.zeros((tm, 1), F32)
    for hh in range(N_ATTN_HEADS):
        a = attn_ref[hh].astype(F32)
        ss = ss + jnp.sum(a * a, axis=-1, keepdims=True)
    r = lax.rsqrt(ss * (1.0 / ATTN_W) + EPS)
    for hh in range(N_ATTN_HEADS):
        lo = hh * HEAD_DIM
        y_scr[:, lo:lo + HEAD_DIM] = (attn_ref[hh].astype(F32) * r * og[:, lo:lo + HEAD_DIM]).astype(BF16)

    seq_start = (i % tiles_per_seq) == 0
    convs = []
    ss = jnp.zeros((tm, 1), F32)
    for g in range(CONV_GROUPS):
        bg = rest_ref[g].astype(F32)
        z = rest_ref[CONV_GROUPS + g].astype(F32) * rest_ref[2 * CONV_GROUPS + g].astype(F32)
        zh = halo_ref[CONV_GROUPS + g].astype(F32) * halo_ref[2 * CONV_GROUPS + g].astype(F32)
        z_scr[g, 0:HALO, :] = jnp.where(seq_start, 0.0, zh)
        z_scr[g, HALO:HALO + tm, :] = z
        z1 = z_scr[g, HALO - 1:HALO - 1 + tm, :]
        z2 = z_scr[g, HALO - 2:HALO - 2 + tm, :]
        lo = g * HEAD_DIM
        w0 = cw_ref[0:1, lo:lo + HEAD_DIM]
        w1 = cw_ref[1:2, lo:lo + HEAD_DIM]
        w2 = cw_ref[2:3, lo:lo + HEAD_DIM]
        cv = bg * (w2 * z + w1 * z1 + w0 * z2)
        convs.append(cv)
        ss = ss + jnp.sum(cv * cv, axis=-1, keepdims=True)
    r = lax.rsqrt(ss * (1.0 / CONV_W) + EPS)
    for g in range(CONV_GROUPS):
        lo = ATTN_W + g * HEAD_DIM
        y_scr[:, lo:lo + HEAD_DIM] = (convs[g] * r * og[:, lo:lo + HEAD_DIM]).astype(BF16)

    mems = []
    ss = jnp.zeros((tm, 1), F32)
    for hh in range(N_MEM_HEADS):
        mq = rest_ref[3 * CONV_GROUPS + hh]
        s = lax.dot_general(mq, kvm_ref[hh], (((1,), (1,)), ((), ())), preferred_element_type=F32)
        m = jnp.max(s, axis=-1, keepdims=True)
        p = jnp.exp(s - m)
        l = jnp.sum(p, axis=-1, keepdims=True)
        o = jnp.dot(p.astype(BF16), kvm_ref[N_MEM_HEADS + hh], preferred_element_type=F32) * (1.0 / l)
        mems.append(o)
        ss = ss + jnp.sum(o * o, axis=-1, keepdims=True)
    r = lax.rsqrt(ss * (1.0 / MEM_W) + EPS)
    for hh in range(N_MEM_HEADS):
        lo = ATTN_W + CONV_W + hh * HEAD_DIM
        y_scr[:, lo:lo + HEAD_DIM] = (mems[hh] * r * og[:, lo:lo + HEAD_DIM]).astype(BF16)

    x2 = x_ref[...] + jnp.dot(y_scr[...], wout_ref[...], preferred_element_type=F32)
    x2_ref[...] = x2
    h2 = x2 * _rms(x2, D_MODEL) * g2_ref[...]
    h2_ref[...] = h2

    lg = jnp.dot(h2.astype(BF16), wr_ref[...], preferred_element_type=F32) + br_ref[...]
    lane = lax.broadcasted_iota(I32, (tm, 128), 1).astype(F32)
    gl = jnp.where(lane < N_GROUPS, lg, NEG_INF)
    gmax = jnp.max(gl, axis=-1, keepdims=True)
    g_w = 1.0 / jnp.sum(jnp.exp(gl - gmax), axis=-1, keepdims=True)
    g_idx = jnp.min(jnp.where(gl == gmax, lane, 1e9), axis=-1, keepdims=True)
    e_lo = N_GROUPS + EXPERTS_PER_GROUP * g_idx
    el = jnp.where((lane >= e_lo) & (lane < e_lo + EXPERTS_PER_GROUP), lg, NEG_INF)
    e1 = jnp.max(el, axis=-1, keepdims=True)
    i1 = jnp.min(jnp.where(el == e1, lane, 1e9), axis=-1, keepdims=True)
    el2 = jnp.where(lane == i1, NEG_INF, el)
    e2 = jnp.max(el2, axis=-1, keepdims=True)
    i2 = jnp.min(jnp.where(el2 == e2, lane, 1e9), axis=-1, keepdims=True)
    t = jnp.exp(e2 - e1)
    p1 = 1.0 / (1.0 + t)
    p2 = t * p1
    route = jnp.where(lane == 0, i1 - N_GROUPS,
                      jnp.where(lane == 1, i2 - N_GROUPS,
                                jnp.where(lane == 2, g_w * p1,
                                          jnp.where(lane == 3, g_w * p2, 0.0))))
    route_ref[...] = route


def _mixer(attn, rest, kvm, x2d, w_out_bf16, out_norm_g, norm2_g, conv_w, w_router, b_router, seq):
    T = x2d.shape[0]
    tm = MIXER_TM
    tiles_per_seq = seq // tm
    full = lambda shape: pl.BlockSpec(shape, lambda i: tuple(0 for _ in shape))
    return pl.pallas_call(
        functools.partial(_mixer_kernel, tiles_per_seq=tiles_per_seq),
        grid=(T // tm,),
        in_specs=[
            pl.BlockSpec((N_ATTN_HEADS, tm, HEAD_DIM), lambda i: (0, i, 0)),
            pl.BlockSpec((16, tm, HEAD_DIM), lambda i: (0, i, 0)),
            pl.BlockSpec((16, HALO, HEAD_DIM), lambda i: (0, jnp.maximum(i * (tm // HALO) - 1, 0), 0)),
            pl.BlockSpec((2 * N_MEM_HEADS, N_MEM, HEAD_DIM), lambda i: (0, i // tiles_per_seq, 0)),
            pl.BlockSpec((tm, D_MODEL), lambda i: (i, 0)),
            pl.BlockSpec((D_MODEL, D_MODEL), lambda i: (0, 0), pipeline_mode=pl.Buffered(1)),
            full((1, D_MODEL)),
            full((1, D_MODEL)),
            full((3, CONV_W)),
            full((D_MODEL, 128)),
            full((1, 128)),
        ],
        out_specs=[
            pl.BlockSpec((tm, D_MODEL), lambda i: (i, 0)),
            pl.BlockSpec((tm, D_MODEL), lambda i: (i, 0)),
            pl.BlockSpec((tm, 128), lambda i: (i, 0)),
        ],
        out_shape=[
            jax.ShapeDtypeStruct((T, D_MODEL), F32),
            jax.ShapeDtypeStruct((T, D_MODEL), F32),
            jax.ShapeDtypeStruct((T, 128), F32),
        ],
        scratch_shapes=[
            pltpu.VMEM((tm, D_MODEL), BF16),
            pltpu.VMEM((CONV_GROUPS, HALO + tm, HEAD_DIM), F32),
        ],
        compiler_params=_params(("arbitrary",), 56),
        name="mixer",
    )(attn, rest, rest, kvm, x2d, w_out_bf16, out_norm_g, norm2_g, conv_w, w_router, b_router)


def _dispatch_kernel(dest_ref, h_ref, xbuf_ref, sem):
    i = pl.program_id(0)
    tm = DISPATCH_TM
    base = i * tm

    def body(t, carry):
        a = 2 * (base + t)
        src = h_ref.at[pl.ds(t, 1)]
        pltpu.make_async_copy(src, xbuf_ref.at[pl.ds(dest_ref[a], 1)], sem).start()
        pltpu.make_async_copy(src, xbuf_ref.at[pl.ds(dest_ref[a + 1], 1)], sem).start()
        return carry

    lax.fori_loop(0, tm, body, 0, unroll=DMA_UNROLL)
    pltpu.make_async_copy(xbuf_ref.at[pl.ds(0, 2 * tm)], xbuf_ref.at[pl.ds(0, 2 * tm)], sem).wait()


def _dispatch(dest, h2, n_rows):
    T = h2.shape[0]
    tm = DISPATCH_TM
    grid_spec = pltpu.PrefetchScalarGridSpec(
        num_scalar_prefetch=1,
        grid=(T // tm,),
        in_specs=[pl.BlockSpec((tm, D_MODEL), lambda i, d: (i, 0))],
        out_specs=pl.BlockSpec(memory_space=pl.ANY),
        scratch_shapes=[pltpu.SemaphoreType.DMA],
    )
    return pl.pallas_call(
        _dispatch_kernel,
        grid_spec=grid_spec,
        out_shape=jax.ShapeDtypeStruct((n_rows, D_MODEL), F32),
        compiler_params=_params(("arbitrary",), 32),
        name="dispatch",
    )(dest, h2)


def _ffn_kernel(ie_ref, rows_ref, x_ref, wg_ref, wu_ref, wd_ref, y_ref, xg, xb, acc, xsem, ysem):
    w = pl.program_id(0)
    c = pl.program_id(1)
    n_chunks = D_EXPERT // FFN_CHUNK

    def subs(v):
        return (v + FFN_SUB - 1) // FFN_SUB

    rows = rows_ref[w]
    ns = subs(rows)
    ns_next = subs(rows_ref[w + 1])
    ns_prev2 = jnp.where(w > 1, subs(rows_ref[jnp.maximum(w - 2, 0)]), 0)
    p = w % 2

    def x_copy(item, s):
        lo, n = FFN_PIECES[s]
        return pltpu.make_async_copy(x_ref.at[pl.ds(item * ITEM_ROWS + lo, n)], xg.at[pl.ds(lo, n)], xsem)

    def y_copy(item, slot, s):
        lo, n = FFN_PIECES[s]
        return pltpu.make_async_copy(acc.at[slot, pl.ds(lo, n)], y_ref.at[pl.ds(item * ITEM_ROWS + lo, n)],
                                     ysem.at[slot])

    def for_pieces(count, fn):
        for s in range(len(FFN_PIECES)):
            @pl.when(s < count)
            def _(s=s):
                fn(s)

    @pl.when(c == 0)
    def _():
        @pl.when(w == 0)
        def _():
            for_pieces(ns, lambda s: x_copy(0, s).start())

        for_pieces(ns, lambda s: x_copy(w, s).wait())
        for_pieces(ns_prev2, lambda s: y_copy(w - 2, p, s).wait())

    for ns_static in range(1, len(FFN_PIECES) + 1):
        nr = FFN_PIECES[ns_static - 1][0] + FFN_PIECES[ns_static - 1][1]

        @pl.when(ns == ns_static)
        def _(nr=nr):
            @pl.when(c == 0)
            def _():
                ok = lax.broadcasted_iota(I32, (nr, 1), 0) < rows
                xb[0:nr, :] = jnp.where(ok, xg[0:nr, :], 0.0).astype(BF16)
                acc[p, 0:nr, :] = jnp.zeros((nr, D_MODEL), F32)

            wg = wg_ref[0].astype(BF16)
            wu = wu_ref[0].astype(BF16)
            wd = wd_ref[0].astype(BF16)
            x = xb[0:nr, :]
            a = jnp.dot(x, wg, preferred_element_type=F32)
            u = jnp.dot(x, wu, preferred_element_type=F32)
            hm = (a / (1.0 + jnp.exp(-a)) * u).astype(BF16)
            acc[p, 0:nr, :] += jnp.dot(hm, wd, preferred_element_type=F32)

    @pl.when(c == 1)
    def _():
        for_pieces(ns_next, lambda s: x_copy(w + 1, s).start())

    @pl.when(c == n_chunks - 1)
    def _():
        for_pieces(ns, lambda s: y_copy(w, p, s).start())


def _ffn(item_expert, item_rows, xbuf, w_gate, w_up, w_down):
    n_chunks = D_EXPERT // FFN_CHUNK
    grid_items = item_rows.shape[0] - 1

    def chunk_of(w, c, ir):
        return jnp.where(ir[w] > 0, c, n_chunks - 1)

    grid_spec = pltpu.PrefetchScalarGridSpec(
        num_scalar_prefetch=2,
        grid=(grid_items, n_chunks),
        in_specs=[
            pl.BlockSpec(memory_space=pl.ANY),
            pl.BlockSpec((1, D_MODEL, FFN_CHUNK), lambda w, c, ie, ir: (ie[w], 0, chunk_of(w, c, ir))),
            pl.BlockSpec((1, D_MODEL, FFN_CHUNK), lambda w, c, ie, ir: (ie[w], 0, chunk_of(w, c, ir))),
            pl.BlockSpec((1, FFN_CHUNK, D_MODEL), lambda w, c, ie, ir: (ie[w], chunk_of(w, c, ir), 0)),
        ],
        out_specs=pl.BlockSpec(memory_space=pl.ANY),
        scratch_shapes=[
            pltpu.VMEM((ITEM_ROWS, D_MODEL), F32),
            pltpu.VMEM((ITEM_ROWS, D_MODEL), BF16),
            pltpu.VMEM((2, ITEM_ROWS, D_MODEL), F32),
            pltpu.SemaphoreType.DMA,
            pltpu.SemaphoreType.DMA((2,)),
        ],
    )
    return pl.pallas_call(
        _ffn_kernel,
        grid_spec=grid_spec,
        out_shape=jax.ShapeDtypeStruct(xbuf.shape, F32),
        compiler_params=_params(("arbitrary", "arbitrary"), 58),
        name="ffn",
    )(item_expert, item_rows, xbuf, w_gate, w_up, w_down)


def _combine_kernel(dest_ref, x2_ref, route_ref, ybuf_ref, o_ref, rows_scr, sem):
    i = pl.program_id(0)
    tm = COMBINE_TM
    base = i * tm

    def body(t, carry):
        a = 2 * (base + t)
        pltpu.make_async_copy(ybuf_ref.at[pl.ds(dest_ref[a], 1)], rows_scr.at[0, pl.ds(t, 1)], sem).start()
        pltpu.make_async_copy(ybuf_ref.at[pl.ds(dest_ref[a + 1], 1)], rows_scr.at[1, pl.ds(t, 1)], sem).start()
        return carry

    lax.fori_loop(0, tm, body, 0, unroll=DMA_UNROLL)
    for k in range(2):
        pltpu.make_async_copy(ybuf_ref.at[pl.ds(0, tm)], rows_scr.at[k], sem).wait()

    o_ref[...] = x2_ref[...] + route_ref[:, 2:3] * rows_scr[0] + route_ref[:, 3:4] * rows_scr[1]


def _combine(dest, x2, route, ybuf):
    T = x2.shape[0]
    tm = COMBINE_TM
    grid_spec = pltpu.PrefetchScalarGridSpec(
        num_scalar_prefetch=1,
        grid=(T // tm,),
        in_specs=[
            pl.BlockSpec((tm, D_MODEL), lambda i, d: (i, 0)),
            pl.BlockSpec((tm, 128), lambda i, d: (i, 0)),
            pl.BlockSpec(memory_space=pl.ANY),
        ],
        out_specs=pl.BlockSpec((tm, D_MODEL), lambda i, d: (i, 0)),
        scratch_shapes=[pltpu.VMEM((2, tm, D_MODEL), F32), pltpu.SemaphoreType.DMA],
    )
    return pl.pallas_call(
        _combine_kernel,
        grid_spec=grid_spec,
        out_shape=jax.ShapeDtypeStruct((T, D_MODEL), F32),
        compiler_params=_params(("arbitrary",), 32),
        name="combine",
    )(dest, x2, route, ybuf)


def _moe_plan(route, max_items):
    e_flat = route[:, :2].astype(I32).reshape(-1)
    onehot = (e_flat[:, None] == jnp.arange(N_EXPERTS, dtype=I32)[None, :]).astype(I32)
    csum = jnp.cumsum(onehot, axis=0)
    counts = csum[-1]
    rank = jnp.sum(onehot * csum, axis=1) - 1
    n_it = (counts + ITEM_ROWS - 1) // ITEM_ROWS
    it_end = jnp.cumsum(n_it)
    it_start = it_end - n_it
    dest = jnp.sum(onehot * (it_start * ITEM_ROWS)[None, :], axis=1) + rank
    n_items = it_end[-1]
    w = jnp.arange(max_items + 3, dtype=I32)
    ie = jnp.minimum(jnp.sum((it_end[None, :] <= w[:, None]).astype(I32), axis=1), N_EXPERTS - 1)
    used = w < n_items
    rows = jnp.where(used, jnp.minimum(ITEM_ROWS, counts[ie] - (w - it_start[ie]) * ITEM_ROWS), 0)
    ie = jnp.where(used, ie, ie[jnp.maximum(n_items - 1, 0)])
    return dest.astype(I32), ie.astype(I32), rows.astype(I32)


def _layer(x, mem, norm1_g, w_in, q_norm_g, k_norm_g, conv_w, mem_norm_g, w_mem_kv, mem_q_norm_g, mem_k_norm_g,
           out_norm_g, w_out, norm2_g, w_rg, b_rg, w_re, b_re, w_gate, w_up, w_down):
    B, S, D = x.shape
    T = B * S
    x2d = x.reshape(T, D)
    row = lambda v: v.reshape(1, -1).astype(F32)

    kvm = _memkv(mem.reshape(B * N_MEM, D), row(mem_norm_g), w_mem_kv, row(mem_k_norm_g))
    qkv, rest = _inproj(x2d, row(norm1_g), w_in.astype(BF16), row(q_norm_g), row(k_norm_g), row(mem_q_norm_g))

    slopes = 2.0 ** (-8.0 * jnp.arange(1, N_ATTN_HEADS + 1, dtype=F32) / N_ATTN_HEADS)
    attn = _attention(qkv, slopes, B, S)

    w_router = jnp.zeros((D, 128), F32).at[:, :N_GROUPS].set(w_rg).at[:, N_GROUPS:N_GROUPS + N_EXPERTS].set(w_re)
    b_router = jnp.zeros((1, 128), F32).at[0, :N_GROUPS].set(b_rg).at[0, N_GROUPS:N_GROUPS + N_EXPERTS].set(b_re)
    x2, h2, route = _mixer(attn, rest, kvm, x2d, w_out.astype(BF16), row(out_norm_g), row(norm2_g),
                           conv_w.astype(F32), w_router.astype(BF16), b_router, S)

    max_items = N_EXPERTS + (2 * T) // ITEM_ROWS
    dest, item_expert, item_rows = _moe_plan(route, max_items)
    xbuf = _dispatch(dest, h2, max_items * ITEM_ROWS)
    ybuf = _ffn(item_expert, item_rows, xbuf, w_gate, w_up, w_down)
    out = _combine(dest, x2, route, ybuf)
    return out.reshape(B, S, D)


def kernel(x, mem, norm1_g, w_in, q_norm_g, k_norm_g, conv_w, mem_norm_g, w_mem_kv, mem_q_norm_g, mem_k_norm_g,
           out_norm_g, w_out, norm2_g, w_router_group, b_router_group, w_router_expert, b_router_expert,
           w_gate, w_up, w_down):
    for l in range(norm1_g.shape[0]):
        x = _layer(x, mem, norm1_g[l], w_in[l], q_norm_g[l], k_norm_g[l], conv_w[l], mem_norm_g[l], w_mem_kv[l],
                   mem_q_norm_g[l], mem_k_norm_g[l], out_norm_g[l], w_out[l], norm2_g[l],
                   w_router_group[l], b_router_group[l], w_router_expert[l], b_router_expert[l],
                   w_gate[l], w_up[l], w_down[l])
    return x
```

```python
import functools

import jax
import jax.numpy as jnp
from jax import lax
from jax.experimental import pallas as pl
from jax.experimental.pallas import tpu as pltpu

F32 = jnp.float32
BF16 = jnp.bfloat16
I32 = jnp.int32

D_MODEL = 2048
HEAD_DIM = 128
N_ATTN_HEADS = 8
ATTN_W = N_ATTN_HEADS * HEAD_DIM
N_MEM_HEADS = 4
MEM_W = N_MEM_HEADS * HEAD_DIM
CONV_W = D_MODEL - ATTN_W - MEM_W
CONV_GROUPS = CONV_W // HEAD_DIM
N_MEM = 256
DILATIONS = (1, 4, 16)
SPAN = 128
PROJ_W = 3 * ATTN_W + 3 * CONV_W + MEM_W
N_GROUPS = 8
EXPERTS_PER_GROUP = 8
N_EXPERTS = N_GROUPS * EXPERTS_PER_GROUP
D_EXPERT = D_MODEL // 2
EPS = 1e-6
NEG_INF = -1e30
QK_SCALE = HEAD_DIM ** -0.5
LOG2E = 1.4426950408889634

MIB = 1024 * 1024
INPROJ_TM = 256
INPROJ_TN = 1024
ATTN_CHUNK = 2048
MIXER_TM = 512
COMBINE_TM = 512
ITEM_ROWS = 1152
FFN_SUB = 256
FFN_PIECES = tuple((lo, min(FFN_SUB, ITEM_ROWS - lo)) for lo in range(0, ITEM_ROWS, FFN_SUB))
FFN_CHUNK = 256
HALO = 16
DMA_UNROLL = 8
ATTN_UNROLL = 16


def _params(semantics, vmem_mib):
    return pltpu.CompilerParams(dimension_semantics=semantics, vmem_limit_bytes=vmem_mib * MIB)


def _rms(v, width):
    return lax.rsqrt(jnp.sum(v * v, axis=-1, keepdims=True) * (1.0 / width) + EPS)


def _memkv_kernel(mem_ref, g_ref, w_ref, kg_ref, o_ref):
    j = pl.program_id(0)
    m = mem_ref[...]
    h = (m * _rms(m, D_MODEL) * g_ref[...]).astype(BF16)
    kv = jnp.dot(h, w_ref[...].astype(BF16), preferred_element_type=F32)
    is_key = j == 0
    for hh in range(N_MEM_HEADS):
        t = kv[:, hh * HEAD_DIM:(hh + 1) * HEAD_DIM]
        tn = t * _rms(t, HEAD_DIM) * kg_ref[...]
        o_ref[hh] = jnp.where(is_key, tn, t).astype(BF16)


def _memkv(mem2d, mem_norm_g, w_mem_kv, mem_k_norm_g):
    rows = mem2d.shape[0]
    return pl.pallas_call(
        _memkv_kernel,
        grid=(2,),
        in_specs=[
            pl.BlockSpec((rows, D_MODEL), lambda j: (0, 0)),
            pl.BlockSpec((1, D_MODEL), lambda j: (0, 0)),
            pl.BlockSpec((D_MODEL, MEM_W), lambda j: (0, j)),
            pl.BlockSpec((1, HEAD_DIM), lambda j: (0, 0)),
        ],
        out_specs=pl.BlockSpec((N_MEM_HEADS, rows, HEAD_DIM), lambda j: (j, 0, 0)),
        out_shape=jax.ShapeDtypeStruct((2 * N_MEM_HEADS, rows, HEAD_DIM), BF16),
        compiler_params=_params(("arbitrary",), 40),
        name="memkv",
    )(mem2d, mem_norm_g, w_mem_kv, mem_k_norm_g)


def _inproj_kernel(x_ref, g1_ref, w_ref, qg_ref, kg_ref, mqg_ref, qkv_ref, rest_ref):
    x = x_ref[...]
    h = (x * _rms(x, D_MODEL) * g1_ref[...]).astype(BF16)

    def normed(t, g):
        return t * _rms(t, HEAD_DIM) * g

    gq = qg_ref[...] * (QK_SCALE * LOG2E)
    gk = kg_ref[...]
    gmq = mqg_ref[...] * QK_SCALE
    groups_per_tile = INPROJ_TN // HEAD_DIM
    for j in range(PROJ_W // INPROJ_TN):
        y = jnp.dot(h, w_ref[:, j * INPROJ_TN:(j + 1) * INPROJ_TN], preferred_element_type=F32)
        for hh in range(groups_per_tile):
            t = y[:, hh * HEAD_DIM:(hh + 1) * HEAD_DIM]
            col = j * groups_per_tile + hh
            if col < 8:
                qkv_ref[col] = normed(t, gq)
            elif col < 16:
                qkv_ref[col] = normed(t, gk)
            elif col < 24:
                qkv_ref[col] = t
            elif col < 36:
                rest_ref[col - 24] = t.astype(BF16)
            else:
                rest_ref[col - 24] = normed(t, gmq).astype(BF16)


def _inproj(x2d, norm1_g, w_in_bf16, q_norm_g, k_norm_g, mem_q_norm_g):
    T = x2d.shape[0]
    tm = INPROJ_TM
    small = lambda n: pl.BlockSpec((1, n), lambda i: (0, 0))
    return pl.pallas_call(
        _inproj_kernel,
        grid=(T // tm,),
        in_specs=[
            pl.BlockSpec((tm, D_MODEL), lambda i: (i, 0)),
            small(D_MODEL),
            pl.BlockSpec((D_MODEL, PROJ_W), lambda i: (0, 0), pipeline_mode=pl.Buffered(1)),
            small(HEAD_DIM), small(HEAD_DIM), small(HEAD_DIM),
        ],
        out_specs=[
            pl.BlockSpec((24, tm, HEAD_DIM), lambda i: (0, i, 0)),
            pl.BlockSpec((16, tm, HEAD_DIM), lambda i: (0, i, 0)),
        ],
        out_shape=[
            jax.ShapeDtypeStruct((24, T, HEAD_DIM), F32),
            jax.ShapeDtypeStruct((16, T, HEAD_DIM), BF16),
        ],
        compiler_params=_params(("parallel",), 52),
        name="inproj",
    )(x2d, norm1_g, w_in_bf16, q_norm_g, k_norm_g, mem_q_norm_g)


def _attn_kernel(slopes_ref, q_ref, k_ref, v_ref, o_ref,
                 kx, vx, q4, k4, v4, o1, l1, o4, l4, o16, l16, out_scr, bias_scr):
    h = pl.program_id(1)
    c = pl.program_id(2)
    C = ATTN_CHUNK

    @pl.when(c == 0)
    def _():
        kx[0:C, :] = jnp.zeros((C, HEAD_DIM), F32)
        vx[0:C, :] = jnp.zeros((C, HEAD_DIM), F32)

    @pl.when(c > 0)
    def _():
        kx[0:C, :] = kx[C:2 * C, :]
        vx[0:C, :] = vx[C:2 * C, :]

    kx[C:2 * C, :] = k_ref[0]
    vx[C:2 * C, :] = v_ref[0]

    for b in range(4):
        q4[b] = q_ref[0, pl.ds(b, C // 4, stride=4), :]
        k4[b] = kx[pl.ds(b, 2 * C // 4, stride=4), :]
        v4[b] = vx[pl.ds(b, 2 * C // 4, stride=4), :]

    slope = slopes_ref[h]
    ii = lax.broadcasted_iota(I32, (SPAN, 2 * SPAN), 0)
    jj = lax.broadcasted_iota(I32, (SPAN, 2 * SPAN), 1)
    dist = ii - jj + SPAN
    band = (dist >= 0) & (dist <= SPAN)
    distf = dist.astype(F32)

    for bi, d in enumerate(DILATIONS):
        neg = (-slope * float(d) * LOG2E) * distf
        bias_scr[2 * bi] = jnp.where(band, neg, NEG_INF)
        bias_scr[2 * bi + 1] = jnp.where(band & (jj >= SPAN), neg, NEG_INF)

    ones = jnp.ones((2 * SPAN, HEAD_DIM), BF16)

    def attend(q, k, v, bias):
        s = lax.dot_general(q.astype(BF16), k.astype(BF16), (((1,), (1,)), ((), ())),
                            preferred_element_type=F32) + bias
        m = jnp.max(s, axis=-1, keepdims=True)
        p = jnp.exp2(s - m).astype(BF16)
        ol = jnp.dot(p, jnp.concatenate([v.astype(BF16), ones], axis=1), preferred_element_type=F32)
        l = ol[:, HEAD_DIM:]
        return ol[:, :HEAD_DIM] * (1.0 / l), m + jnp.log2(l)

    def unit1(u, carry):
        sq = pl.multiple_of(u * SPAN, SPAN)
        sk = pl.multiple_of(C + u * SPAN - SPAN, SPAN)
        first = jnp.logical_and(c == 0, u == 0).astype(I32)
        o, lse = attend(q_ref[0, pl.ds(sq, SPAN), :], kx[pl.ds(sk, 2 * SPAN), :], vx[pl.ds(sk, 2 * SPAN), :],
                        bias_scr[first])
        o1[pl.ds(sq, SPAN), :] = o
        l1[pl.ds(sq, SPAN), :] = lse
        return carry

    def unit4(u, carry):
        b = u % 4
        qb = u // 4
        sq = pl.multiple_of(qb * SPAN, SPAN)
        sk = pl.multiple_of(C // 4 + qb * SPAN - SPAN, SPAN)
        first = jnp.logical_and(c == 0, qb == 0).astype(I32)
        o, lse = attend(q4[b, pl.ds(sq, SPAN), :], k4[b, pl.ds(sk, 2 * SPAN), :], v4[b, pl.ds(sk, 2 * SPAN), :],
                        bias_scr[2 + first])
        o4[b, pl.ds(sq, SPAN), :] = o
        l4[b, pl.ds(sq, SPAN), :] = lse
        return carry

    def unit16(u, carry):
        b = u % 4
        a = u // 4
        first = (c == 0).astype(I32)
        qi = pl.ds(a, SPAN, stride=4)
        ki = pl.ds(a, 2 * SPAN, stride=4)
        o, lse = attend(q4[b, qi, :], k4[b, ki, :], v4[b, ki, :], bias_scr[4 + first])
        o16[b, qi, :] = o
        l16[b, qi, :] = lse
        return carry

    for unit in (unit1, unit4, unit16):
        lax.fori_loop(0, C // SPAN, unit, 0, unroll=ATTN_UNROLL)

    for b in range(4):
        rows_b = pl.ds(b, C // 4, stride=4)
        la, lb, lc = l1[rows_b, :], l4[b], l16[b]
        mx = jnp.maximum(jnp.maximum(la, lb), lc)
        ea, eb, ec = jnp.exp2(la - mx), jnp.exp2(lb - mx), jnp.exp2(lc - mx)
        out_scr[rows_b, :] = (ea * o1[rows_b, :] + eb * o4[b] + ec * o16[b]) * (1.0 / (ea + eb + ec))
    o_ref[0] = out_scr[...].astype(BF16)


def _attention(qkv, slopes, batch, seq):
    T = qkv.shape[1]
    C = ATTN_CHUNK
    nc = seq // C
    blk = lambda off: pl.BlockSpec((1, C, HEAD_DIM), lambda b, h, c, s: (h + off, b * nc + c, 0))
    grid_spec = pltpu.PrefetchScalarGridSpec(
        num_scalar_prefetch=1,
        grid=(batch, N_ATTN_HEADS, nc),
        in_specs=[blk(0), blk(8), blk(16)],
        out_specs=pl.BlockSpec((1, C, HEAD_DIM), lambda b, h, c, s: (h, b * nc + c, 0)),
        scratch_shapes=[
            pltpu.VMEM((2 * C, HEAD_DIM), F32),
            pltpu.VMEM((2 * C, HEAD_DIM), F32),
            pltpu.VMEM((4, C // 4, HEAD_DIM), F32),
            pltpu.VMEM((4, 2 * C // 4, HEAD_DIM), F32),
            pltpu.VMEM((4, 2 * C // 4, HEAD_DIM), F32),
            pltpu.VMEM((C, HEAD_DIM), F32),
            pltpu.VMEM((C, HEAD_DIM), F32),
            pltpu.VMEM((4, C // 4, HEAD_DIM), F32),
            pltpu.VMEM((4, C // 4, HEAD_DIM), F32),
            pltpu.VMEM((4, C // 4, HEAD_DIM), F32),
            pltpu.VMEM((4, C // 4, HEAD_DIM), F32),
            pltpu.VMEM((C, HEAD_DIM), F32),
            pltpu.VMEM((2 * len(DILATIONS), SPAN, 2 * SPAN), F32),
        ],
    )
    return pl.pallas_call(
        _attn_kernel,
        grid_spec=grid_spec,
        out_shape=jax.ShapeDtypeStruct((N_ATTN_HEADS, T, HEAD_DIM), BF16),
        compiler_params=_params(("parallel", "parallel", "arbitrary"), 40),
        name="attn",
    )(slopes, qkv, qkv, qkv)


def _mixer_kernel(attn_ref, rest_ref, halo_ref, kvm_ref, x_ref, wout_ref, og_ref, g2_ref, cw_ref, wr_ref, br_ref,
                  x2_ref, h2_ref, route_ref, y_scr, z_scr, *, tiles_per_seq):
    i = pl.program_id(0)
    tm = MIXER_TM
    og = og_ref[...]

    ss = jnp.zeros((tm, 1), F32)
    for hh in range(N_ATTN_HEADS):
        a = attn_ref[hh].astype(F32)
        ss = ss + jnp.sum(a * a, axis=-1, keepdims=True)
    r = lax.rsqrt(ss * (1.0 / ATTN_W) + EPS)
    for hh in range(N_ATTN_HEADS):
        lo = hh * HEAD_DIM
        y_scr[:, lo:lo + HEAD_DIM] = (attn_ref[hh].astype(F32) * r * og[:, lo:lo + HEAD_DIM]).astype(BF16)

    seq_start = (i % tiles_per_seq) == 0
    convs = []
    ss = jnp.zeros((tm, 1), F32)
    for g in range(CONV_GROUPS):
        bg = rest_ref[g].astype(F32)
        z = rest_ref[CONV_GROUPS + g].astype(F32) * rest_ref[2 * CONV_GROUPS + g].astype(F32)
        zh = halo_ref[CONV_GROUPS + g].astype(F32) * halo_ref[2 * CONV_GROUPS + g].astype(F32)
        z_scr[g, 0:HALO, :] = jnp.where(seq_start, 0.0, zh)
        z_scr[g, HALO:HALO + tm, :] = z
        z1 = z_scr[g, HALO - 1:HALO - 1 + tm, :]
        z2 = z_scr[g, HALO - 2:HALO - 2 + tm, :]
        lo = g * HEAD_DIM
        w0 = cw_ref[0:1, lo:lo + HEAD_DIM]
        w1 = cw_ref[1:2, lo:lo + HEAD_DIM]
        w2 = cw_ref[2:3, lo:lo + HEAD_DIM]
        cv = bg * (w2 * z + w1 * z1 + w0 * z2)
        convs.append(cv)
        ss = ss + jnp.sum(cv * cv, axis=-1, keepdims=True)
    r = lax.rsqrt(ss * (1.0 / CONV_W) + EPS)
    for g in range(CONV_GROUPS):
        lo = ATTN_W + g * HEAD_DIM
        y_scr[:, lo:lo + HEAD_DIM] = (convs[g] * r * og[:, lo:lo + HEAD_DIM]).astype(BF16)

    mems = []
    ss = jnp.zeros((tm, 1), F32)
    for hh in range(N_MEM_HEADS):
        mq = rest_ref[3 * CONV_GROUPS + hh]
        s = lax.dot_general(mq, kvm_ref[hh], (((1,), (1,)), ((), ())), preferred_element_type=F32)
        m = jnp.max(s, axis=-1, keepdims=True)
        p = jnp.exp(s - m)
        l = jnp.sum(p, axis=-1, keepdims=True)
        o = jnp.dot(p.astype(BF16), kvm_ref[N_MEM_HEADS + hh], preferred_element_type=F32) * (1.0 / l)
        mems.append(o)
        ss = ss + jnp.sum(o * o, axis=-1, keepdims=True)
    r = lax.rsqrt(ss * (1.0 / MEM_W) + EPS)
    for hh in range(N_MEM_HEADS):
        lo = ATTN_W + CONV_W + hh * HEAD_DIM
        y_scr[:, lo:lo + HEAD_DIM] = (mems[hh] * r * og[:, lo:lo + HEAD_DIM]).astype(BF16)

    x2 = x_ref[...] + jnp.dot(y_scr[...], wout_ref[...], preferred_element_type=F32)
    x2_ref[...] = x2
    h2 = x2 * _rms(x2, D_MODEL) * g2_ref[...]
    h2_ref[...] = h2

    lg = jnp.dot(h2.astype(BF16), wr_ref[...], preferred_element_type=F32) + br_ref[...]
    lane = lax.broadcasted_iota(I32, (tm, 128), 1).astype(F32)
    gl = jnp.where(lane < N_GROUPS, lg, NEG_INF)
    gmax = jnp.max(gl, axis=-1, keepdims=True)
    g_w = 1.0 / jnp.sum(jnp.exp(gl - gmax), axis=-1, keepdims=True)
    g_idx = jnp.min(jnp.where(gl == gmax, lane, 1e9), axis=-1, keepdims=True)
    e_lo = N_GROUPS + EXPERTS_PER_GROUP * g_idx
    el = jnp.where((lane >= e_lo) & (lane < e_lo + EXPERTS_PER_GROUP), lg, NEG_INF)
    e1 = jnp.max(el, axis=-1, keepdims=True)
    i1 = jnp.min(jnp.where(el == e1, lane, 1e9), axis=-1, keepdims=True)
    el2 = jnp.where(lane == i1, NEG_INF, el)
    e2 = jnp.max(el2, axis=-1, keepdims=True)
    i2 = jnp.min(jnp.where(el2 == e2, lane, 1e9), axis=-1, keepdims=True)
    t = jnp.exp(e2 - e1)
    p1 = 1.0 / (1.0 + t)
    p2 = t * p1
    route = jnp.where(lane == 0, i1 - N_GROUPS,
                      jnp.where(lane == 1, i2 - N_GROUPS,
                                jnp.where(lane == 2, g_w * p1,
                                          jnp.where(lane == 3, g_w * p2, 0.0))))
    route_ref[...] = route


def _mixer(attn, rest, kvm, x2d, w_out_bf16, out_norm_g, norm2_g, conv_w, w_router, b_router, seq):
    T = x2d.shape[0]
    tm = MIXER_TM
    tiles_per_seq = seq // tm
    full = lambda shape: pl.BlockSpec(shape, lambda i: tuple(0 for _ in shape))
    return pl.pallas_call(
        functools.partial(_mixer_kernel, tiles_per_seq=tiles_per_seq),
        grid=(T // tm,),
        in_specs=[
            pl.BlockSpec((N_ATTN_HEADS, tm, HEAD_DIM), lambda i: (0, i, 0)),
            pl.BlockSpec((16, tm, HEAD_DIM), lambda i: (0, i, 0)),
            pl.BlockSpec((16, HALO, HEAD_DIM), lambda i: (0, jnp.maximum(i * (tm // HALO) - 1, 0), 0)),
            pl.BlockSpec((2 * N_MEM_HEADS, N_MEM, HEAD_DIM), lambda i: (0, i // tiles_per_seq, 0)),
            pl.BlockSpec((tm, D_MODEL), lambda i: (i, 0)),
            pl.BlockSpec((D_MODEL, D_MODEL), lambda i: (0, 0), pipeline_mode=pl.Buffered(1)),
            full((1, D_MODEL)),
            full((1, D_MODEL)),
            full((3, CONV_W)),
            full((D_MODEL, 128)),
            full((1, 128)),
        ],
        out_specs=[
            pl.BlockSpec((tm, D_MODEL), lambda i: (i, 0)),
            pl.BlockSpec((tm, D_MODEL), lambda i: (i, 0)),
            pl.BlockSpec((tm, 128), lambda i: (i, 0)),
        ],
        out_shape=[
            jax.ShapeDtypeStruct((T, D_MODEL), F32),
            jax.ShapeDtypeStruct((T, D_MODEL), F32),
            jax.ShapeDtypeStruct((T, 128), F32),
        ],
        scratch_shapes=[
            pltpu.VMEM((tm, D_MODEL), BF16),
            pltpu.VMEM((CONV_GROUPS, HALO + tm, HEAD_DIM), F32),
        ],
        compiler_params=_params(("arbitrary",), 56),
        name="mixer",
    )(attn, rest, rest, kvm, x2d, w_out_bf16, out_norm_g, norm2_g, conv_w, w_router, b_router)


def _ffn_kernel(ie_ref, rows_ref, pos_ref, assign_ref, h2_ref, wg_ref, wu_ref, wd_ref, y_ref,
                xg, xb, acc, gsem, ssem, *, n_tokens):
    w = pl.program_id(0)
    c = pl.program_id(1)
    quarter = ITEM_ROWS // (D_EXPERT // FFN_CHUNK)

    def subs(v):
        return (v + FFN_SUB - 1) // FFN_SUB

    rows = rows_ref[w]
    ns = subs(rows)
    rows_next = rows_ref[w + 1]
    rows_prev = jnp.where(w > 0, rows_ref[jnp.maximum(w - 1, 0)], 0)
    rows_prev2 = jnp.where(w > 1, rows_ref[jnp.maximum(w - 2, 0)], 0)
    p = w % 2
    q = 1 - p
    def for_rows(lo, hi, start_row):
        n = jnp.maximum(hi - lo, 0)
        groups = n // DMA_UNROLL

        def group(g, carry):
            r0 = pl.multiple_of(lo + g * DMA_UNROLL, DMA_UNROLL)
            for k in range(DMA_UNROLL):
                start_row(r0 + k)
            return carry

        def single(r, carry):
            start_row(r)
            return carry

        lax.fori_loop(0, groups, group, 0)
        lax.fori_loop(lo + groups * DMA_UNROLL, lo + n, single, 0)

    def gather_rows(item, lo, hi):
        base = pos_ref[item]

        def start_row(r):
            a = assign_ref[base + r]
            pltpu.make_async_copy(h2_ref.at[pl.ds(a >> 1, 1)], xg.at[pl.ds(r, 1)], gsem).start()

        for_rows(lo, hi, start_row)

    def scatter_rows(item, slot, lo, hi):
        base = pos_ref[item]

        def start_row(r):
            a = assign_ref[base + r]
            dst = (a & 1) * n_tokens + (a >> 1)
            pltpu.make_async_copy(acc.at[slot, pl.ds(r, 1)], y_ref.at[pl.ds(dst, 1)], ssem).start()

        for_rows(lo, hi, start_row)

    def wait_rows(n, descriptor):
        for bit in range(ITEM_ROWS.bit_length()):
            @pl.when((n & (1 << bit)) != 0)
            def _(bit=bit):
                descriptor(1 << bit).wait()

    @pl.when(c == 0)
    def _():
        @pl.when(w == 0)
        def _():
            xg[...] = jnp.zeros((ITEM_ROWS, D_MODEL), F32)
            gather_rows(0, 0, rows)

        wait_rows(rows, lambda n: pltpu.make_async_copy(h2_ref.at[pl.ds(0, n)], xg.at[pl.ds(0, n)], gsem))
        wait_rows(rows_prev2, lambda n: pltpu.make_async_copy(acc.at[p, pl.ds(0, n)], y_ref.at[pl.ds(0, n)], ssem))

    for ns_static in range(1, len(FFN_PIECES) + 1):
        nr = FFN_PIECES[ns_static - 1][0] + FFN_PIECES[ns_static - 1][1]

        @pl.when(ns == ns_static)
        def _(nr=nr):
            @pl.when(c == 0)
            def _():
                ok = lax.broadcasted_iota(I32, (nr, 1), 0) < rows
                xb[0:nr, :] = jnp.where(ok, xg[0:nr, :], 0.0).astype(BF16)
                acc[p, 0:nr, :] = jnp.zeros((nr, D_MODEL), F32)

            wg = wg_ref[0].astype(BF16)
            wu = wu_ref[0].astype(BF16)
            wd = wd_ref[0].astype(BF16)
            x = xb[0:nr, :]
            a = jnp.dot(x, wg, preferred_element_type=F32)
            u = jnp.dot(x, wu, preferred_element_type=F32)
            hm = (a / (1.0 + jnp.exp(-a)) * u).astype(BF16)
            acc[p, 0:nr, :] += jnp.dot(hm, wd, preferred_element_type=F32)

    lo = c * quarter
    gather_rows(w + 1, lo, jnp.minimum(lo + quarter, rows_next))
    scatter_rows(jnp.maximum(w - 1, 0), q, lo, jnp.minimum(lo + quarter, rows_prev))


def _ffn(item_expert, item_rows, item_pos, assign, h2, w_gate, w_up, w_down):
    T = h2.shape[0]
    n_chunks = D_EXPERT // FFN_CHUNK
    grid_items = item_rows.shape[0] - 1

    def chunk_of(w, c, ir):
        return jnp.where(ir[w] > 0, c, n_chunks - 1)

    grid_spec = pltpu.PrefetchScalarGridSpec(
        num_scalar_prefetch=4,
        grid=(grid_items, n_chunks),
        in_specs=[
            pl.BlockSpec(memory_space=pl.ANY),
            pl.BlockSpec((1, D_MODEL, FFN_CHUNK), lambda w, c, ie, ir, ip, asg: (ie[w], 0, chunk_of(w, c, ir))),
            pl.BlockSpec((1, D_MODEL, FFN_CHUNK), lambda w, c, ie, ir, ip, asg: (ie[w], 0, chunk_of(w, c, ir))),
            pl.BlockSpec((1, FFN_CHUNK, D_MODEL), lambda w, c, ie, ir, ip, asg: (ie[w], chunk_of(w, c, ir), 0)),
        ],
        out_specs=pl.BlockSpec(memory_space=pl.ANY),
        scratch_shapes=[
            pltpu.VMEM((ITEM_ROWS, D_MODEL), F32),
            pltpu.VMEM((ITEM_ROWS, D_MODEL), BF16),
            pltpu.VMEM((2, ITEM_ROWS, D_MODEL), F32),
            pltpu.SemaphoreType.DMA,
            pltpu.SemaphoreType.DMA,
        ],
    )
    return pl.pallas_call(
        functools.partial(_ffn_kernel, n_tokens=T),
        grid_spec=grid_spec,
        out_shape=jax.ShapeDtypeStruct((2 * T, D_MODEL), F32),
        compiler_params=_params(("arbitrary", "arbitrary"), 58),
        name="ffn",
    )(item_expert, item_rows, item_pos, assign, h2, w_gate, w_up, w_down)


def _combine_kernel(x2_ref, route_ref, y0_ref, y1_ref, o_ref):
    o_ref[...] = x2_ref[...] + route_ref[:, 2:3] * y0_ref[...] + route_ref[:, 3:4] * y1_ref[...]


def _combine(x2, route, y):
    T = x2.shape[0]
    tm = COMBINE_TM
    nblk = T // tm
    return pl.pallas_call(
        _combine_kernel,
        grid=(nblk,),
        in_specs=[
            pl.BlockSpec((tm, D_MODEL), lambda i: (i, 0)),
            pl.BlockSpec((tm, 128), lambda i: (i, 0)),
            pl.BlockSpec((tm, D_MODEL), lambda i: (i, 0)),
            pl.BlockSpec((tm, D_MODEL), lambda i: (nblk + i, 0)),
        ],
        out_specs=pl.BlockSpec((tm, D_MODEL), lambda i: (i, 0)),
        out_shape=jax.ShapeDtypeStruct((T, D_MODEL), F32),
        compiler_params=_params(("parallel",), 48),
        name="combine",
    )(x2, route, y, y)


def _moe_plan(route, max_items):
    e_flat = route[:, :2].astype(I32).reshape(-1)
    onehot = (e_flat[:, None] == jnp.arange(N_EXPERTS, dtype=I32)[None, :]).astype(I32)
    csum = jnp.cumsum(onehot, axis=0)
    counts = csum[-1]
    rank = jnp.sum(onehot * csum, axis=1) - 1
    starts = jnp.cumsum(counts) - counts
    pos = jnp.sum(onehot * starts[None, :], axis=1) + rank
    assign = jnp.zeros_like(e_flat).at[pos].set(jnp.arange(e_flat.shape[0], dtype=I32))
    n_it = (counts + ITEM_ROWS - 1) // ITEM_ROWS
    it_end = jnp.cumsum(n_it)
    it_start = it_end - n_it
    n_items = it_end[-1]
    w = jnp.arange(max_items + 3, dtype=I32)
    ie = jnp.minimum(jnp.sum((it_end[None, :] <= w[:, None]).astype(I32), axis=1), N_EXPERTS - 1)
    used = w < n_items
    first = (w - it_start[ie]) * ITEM_ROWS
    rows = jnp.where(used, jnp.minimum(ITEM_ROWS, counts[ie] - first), 0)
    item_pos = jnp.where(used, starts[ie] + first, 0)
    ie = jnp.where(used, ie, ie[jnp.maximum(n_items - 1, 0)])
    return assign, ie.astype(I32), rows.astype(I32), item_pos.astype(I32)


def _layer(x, mem, norm1_g, w_in, q_norm_g, k_norm_g, conv_w, mem_norm_g, w_mem_kv, mem_q_norm_g, mem_k_norm_g,
           out_norm_g, w_out, norm2_g, w_rg, b_rg, w_re, b_re, w_gate, w_up, w_down):
    B, S, D = x.shape
    T = B * S
    x2d = x.reshape(T, D)
    row = lambda v: v.reshape(1, -1).astype(F32)

    kvm = _memkv(mem.reshape(B * N_MEM, D), row(mem_norm_g), w_mem_kv, row(mem_k_norm_g))
    qkv, rest = _inproj(x2d, row(norm1_g), w_in.astype(BF16), row(q_norm_g), row(k_norm_g), row(mem_q_norm_g))

    slopes = 2.0 ** (-8.0 * jnp.arange(1, N_ATTN_HEADS + 1, dtype=F32) / N_ATTN_HEADS)
    attn = _attention(qkv, slopes, B, S)

    w_router = jnp.zeros((D, 128), F32).at[:, :N_GROUPS].set(w_rg).at[:, N_GROUPS:N_GROUPS + N_EXPERTS].set(w_re)
    b_router = jnp.zeros((1, 128), F32).at[0, :N_GROUPS].set(b_rg).at[0, N_GROUPS:N_GROUPS + N_EXPERTS].set(b_re)
    x2, h2, route = _mixer(attn, rest, kvm, x2d, w_out.astype(BF16), row(out_norm_g), row(norm2_g),
                           conv_w.astype(F32), w_router.astype(BF16), b_router, S)

    max_items = N_EXPERTS + (2 * T) // ITEM_ROWS
    assign, item_expert, item_rows, item_pos = _moe_plan(route, max_items)
    y = _ffn(item_expert, item_rows, item_pos, assign, h2, w_gate, w_up, w_down)
    out = _combine(x2, route, y)
    return out.reshape(B, S, D)


def kernel(x, mem, norm1_g, w_in, q_norm_g, k_norm_g, conv_w, mem_norm_g, w_mem_kv, mem_q_norm_g, mem_k_norm_g,
           out_norm_g, w_out, norm2_g, w_router_group, b_router_group, w_router_expert, b_router_expert,
           w_gate, w_up, w_down):
    for l in range(norm1_g.shape[0]):
        x = _layer(x, mem, norm1_g[l], w_in[l], q_norm_g[l], k_norm_g[l], conv_w[l], mem_norm_g[l], w_mem_kv[l],
                   mem_q_norm_g[l], mem_k_norm_g[l], out_norm_g[l], w_out[l], norm2_g[l],
                   w_router_group[l], b_router_group[l], w_router_expert[l], b_router_expert[l],
                   w_gate[l], w_up[l], w_down[l])
    return x
```

```python
import functools

import jax
import jax.numpy as jnp
from jax import lax
from jax.experimental import pallas as pl
from jax.experimental.pallas import tpu as pltpu

F32 = jnp.float32
BF16 = jnp.bfloat16
I32 = jnp.int32

D_MODEL = 2048
HEAD_DIM = 128
N_ATTN_HEADS = 8
ATTN_W = N_ATTN_HEADS * HEAD_DIM
N_MEM_HEADS = 4
MEM_W = N_MEM_HEADS * HEAD_DIM
CONV_W = D_MODEL - ATTN_W - MEM_W
CONV_GROUPS = CONV_W // HEAD_DIM
N_MEM = 256
DILATIONS = (1, 4, 16)
SPAN = 128
PROJ_W = 3 * ATTN_W + 3 * CONV_W + MEM_W
N_GROUPS = 8
EXPERTS_PER_GROUP = 8
N_EXPERTS = N_GROUPS * EXPERTS_PER_GROUP
D_EXPERT = D_MODEL // 2
EPS = 1e-6
NEG_INF = -1e30
QK_SCALE = HEAD_DIM ** -0.5
LOG2E = 1.4426950408889634

MIB = 1024 * 1024
INPROJ_TM = 256
INPROJ_TN = 1024
ATTN_CHUNK = 2048
MIXER_TM = 512
DISPATCH_TM = 512
COMBINE_TM = 256
ITEM_ROWS = 1152
FFN_SUB = 256
FFN_PIECES = tuple((lo, min(FFN_SUB, ITEM_ROWS - lo)) for lo in range(0, ITEM_ROWS, FFN_SUB))
X_PIECES_AT_STEP = {1: (0, 1), 2: (2, 3), 3: (4,)}
Y_PIECES_EARLY = (0, 1)
Y_PIECES_LATE = (2, 3, 4)
FFN_CHUNK = 256
HALO = 16
DMA_UNROLL = 8
ATTN_UNROLL = 16


def _params(semantics, vmem_mib):
    return pltpu.CompilerParams(dimension_semantics=semantics, vmem_limit_bytes=vmem_mib * MIB)


def _rms(v, width):
    return lax.rsqrt(jnp.sum(v * v, axis=-1, keepdims=True) * (1.0 / width) + EPS)


def _memkv_kernel(mem_ref, g_ref, w_ref, kg_ref, o_ref):
    j = pl.program_id(0)
    m = mem_ref[...]
    h = (m * _rms(m, D_MODEL) * g_ref[...]).astype(BF16)
    kv = jnp.dot(h, w_ref[...].astype(BF16), preferred_element_type=F32)
    is_key = j == 0
    for hh in range(N_MEM_HEADS):
        t = kv[:, hh * HEAD_DIM:(hh + 1) * HEAD_DIM]
        tn = t * _rms(t, HEAD_DIM) * kg_ref[...]
        o_ref[hh] = jnp.where(is_key, tn, t).astype(BF16)


def _memkv(mem2d, mem_norm_g, w_mem_kv, mem_k_norm_g):
    rows = mem2d.shape[0]
    return pl.pallas_call(
        _memkv_kernel,
        grid=(2,),
        in_specs=[
            pl.BlockSpec((rows, D_MODEL), lambda j: (0, 0)),
            pl.BlockSpec((1, D_MODEL), lambda j: (0, 0)),
            pl.BlockSpec((D_MODEL, MEM_W), lambda j: (0, j)),
            pl.BlockSpec((1, HEAD_DIM), lambda j: (0, 0)),
        ],
        out_specs=pl.BlockSpec((N_MEM_HEADS, rows, HEAD_DIM), lambda j: (j, 0, 0)),
        out_shape=jax.ShapeDtypeStruct((2 * N_MEM_HEADS, rows, HEAD_DIM), BF16),
        compiler_params=_params(("arbitrary",), 40),
        name="memkv",
    )(mem2d, mem_norm_g, w_mem_kv, mem_k_norm_g)


def _inproj_kernel(x_ref, g1_ref, w_ref, qg_ref, kg_ref, mqg_ref, qkv_ref, rest_ref):
    x = x_ref[...]
    h = (x * _rms(x, D_MODEL) * g1_ref[...]).astype(BF16)

    def normed(t, g):
        return t * _rms(t, HEAD_DIM) * g

    gq = qg_ref[...] * (QK_SCALE * LOG2E)
    gk = kg_ref[...]
    gmq = mqg_ref[...] * QK_SCALE
    groups_per_tile = INPROJ_TN // HEAD_DIM
    for j in range(PROJ_W // INPROJ_TN):
        y = jnp.dot(h, w_ref[:, j * INPROJ_TN:(j + 1) * INPROJ_TN], preferred_element_type=F32)
        for hh in range(groups_per_tile):
            t = y[:, hh * HEAD_DIM:(hh + 1) * HEAD_DIM]
            col = j * groups_per_tile + hh
            if col < 8:
                qkv_ref[col] = normed(t, gq)
            elif col < 16:
                qkv_ref[col] = normed(t, gk)
            elif col < 24:
                qkv_ref[col] = t
            elif col < 36:
                rest_ref[col - 24] = t.astype(BF16)
            else:
                rest_ref[col - 24] = normed(t, gmq).astype(BF16)


def _inproj(x2d, norm1_g, w_in_bf16, q_norm_g, k_norm_g, mem_q_norm_g):
    T = x2d.shape[0]
    tm = INPROJ_TM
    small = lambda n: pl.BlockSpec((1, n), lambda i: (0, 0))
    return pl.pallas_call(
        _inproj_kernel,
        grid=(T // tm,),
        in_specs=[
            pl.BlockSpec((tm, D_MODEL), lambda i: (i, 0)),
            small(D_MODEL),
            pl.BlockSpec((D_MODEL, PROJ_W), lambda i: (0, 0), pipeline_mode=pl.Buffered(1)),
            small(HEAD_DIM), small(HEAD_DIM), small(HEAD_DIM),
        ],
        out_specs=[
            pl.BlockSpec((24, tm, HEAD_DIM), lambda i: (0, i, 0)),
            pl.BlockSpec((16, tm, HEAD_DIM), lambda i: (0, i, 0)),
        ],
        out_shape=[
            jax.ShapeDtypeStruct((24, T, HEAD_DIM), F32),
            jax.ShapeDtypeStruct((16, T, HEAD_DIM), BF16),
        ],
        compiler_params=_params(("parallel",), 52),
        name="inproj",
    )(x2d, norm1_g, w_in_bf16, q_norm_g, k_norm_g, mem_q_norm_g)


def _attn_kernel(slopes_ref, q_ref, k_ref, v_ref, o_ref,
                 kx, vx, q4, k4, v4, o1, l1, o4, l4, o16, l16, out_scr, bias_scr):
    h = pl.program_id(1)
    c = pl.program_id(2)
    C = ATTN_CHUNK

    @pl.when(c == 0)
    def _():
        kx[0:C, :] = jnp.zeros((C, HEAD_DIM), F32)
        vx[0:C, :] = jnp.zeros((C, HEAD_DIM), F32)

    @pl.when(c > 0)
    def _():
        kx[0:C, :] = kx[C:2 * C, :]
        vx[0:C, :] = vx[C:2 * C, :]

    kx[C:2 * C, :] = k_ref[0]
    vx[C:2 * C, :] = v_ref[0]

    for b in range(4):
        q4[b] = q_ref[0, pl.ds(b, C // 4, stride=4), :]
        k4[b] = kx[pl.ds(b, 2 * C // 4, stride=4), :]
        v4[b] = vx[pl.ds(b, 2 * C // 4, stride=4), :]

    slope = slopes_ref[h]
    ii = lax.broadcasted_iota(I32, (SPAN, 2 * SPAN), 0)
    jj = lax.broadcasted_iota(I32, (SPAN, 2 * SPAN), 1)
    dist = ii - jj + SPAN
    band = (dist >= 0) & (dist <= SPAN)
    distf = dist.astype(F32)

    for bi, d in enumerate(DILATIONS):
        neg = (-slope * float(d) * LOG2E) * distf
        bias_scr[2 * bi] = jnp.where(band, neg, NEG_INF)
        bias_scr[2 * bi + 1] = jnp.where(band & (jj >= SPAN), neg, NEG_INF)

    ones = jnp.ones((2 * SPAN, HEAD_DIM), BF16)

    def attend(q, k, v, bias):
        s = lax.dot_general(q.astype(BF16), k.astype(BF16), (((1,), (1,)), ((), ())),
                            preferred_element_type=F32) + bias
        m = jnp.max(s, axis=-1, keepdims=True)
        p = jnp.exp2(s - m).astype(BF16)
        ol = jnp.dot(p, jnp.concatenate([v.astype(BF16), ones], axis=1), preferred_element_type=F32)
        l = ol[:, HEAD_DIM:]
        return ol[:, :HEAD_DIM] * (1.0 / l), m + jnp.log2(l)

    def unit1(u, carry):
        sq = pl.multiple_of(u * SPAN, SPAN)
        sk = pl.multiple_of(C + u * SPAN - SPAN, SPAN)
        first = jnp.logical_and(c == 0, u == 0).astype(I32)
        o, lse = attend(q_ref[0, pl.ds(sq, SPAN), :], kx[pl.ds(sk, 2 * SPAN), :], vx[pl.ds(sk, 2 * SPAN), :],
                        bias_scr[first])
        o1[pl.ds(sq, SPAN), :] = o
        l1[pl.ds(sq, SPAN), :] = lse
        return carry

    def unit4(u, carry):
        b = u % 4
        qb = u // 4
        sq = pl.multiple_of(qb * SPAN, SPAN)
        sk = pl.multiple_of(C // 4 + qb * SPAN - SPAN, SPAN)
        first = jnp.logical_and(c == 0, qb == 0).astype(I32)
        o, lse = attend(q4[b, pl.ds(sq, SPAN), :], k4[b, pl.ds(sk, 2 * SPAN), :], v4[b, pl.ds(sk, 2 * SPAN), :],
                        bias_scr[2 + first])
        o4[b, pl.ds(sq, SPAN), :] = o
        l4[b, pl.ds(sq, SPAN), :] = lse
        return carry

    def unit16(u, carry):
        b = u % 4
        a = u // 4
        first = (c == 0).astype(I32)
        qi = pl.ds(a, SPAN, stride=4)
        ki = pl.ds(a, 2 * SPAN, stride=4)
        o, lse = attend(q4[b, qi, :], k4[b, ki, :], v4[b, ki, :], bias_scr[4 + first])
        o16[b, qi, :] = o
        l16[b, qi, :] = lse
        return carry

    for unit in (unit1, unit4, unit16):
        lax.fori_loop(0, C // SPAN, unit, 0, unroll=ATTN_UNROLL)

    for b in range(4):
        rows_b = pl.ds(b, C // 4, stride=4)
        la, lb, lc = l1[rows_b, :], l4[b], l16[b]
        mx = jnp.maximum(jnp.maximum(la, lb), lc)
        ea, eb, ec = jnp.exp2(la - mx), jnp.exp2(lb - mx), jnp.exp2(lc - mx)
        out_scr[rows_b, :] = (ea * o1[rows_b, :] + eb * o4[b] + ec * o16[b]) * (1.0 / (ea + eb + ec))
    o_ref[0] = out_scr[...].astype(BF16)


def _attention(qkv, slopes, batch, seq):
    T = qkv.shape[1]
    C = ATTN_CHUNK
    nc = seq // C
    blk = lambda off: pl.BlockSpec((1, C, HEAD_DIM), lambda b, h, c, s: (h + off, b * nc + c, 0))
    grid_spec = pltpu.PrefetchScalarGridSpec(
        num_scalar_prefetch=1,
        grid=(batch, N_ATTN_HEADS, nc),
        in_specs=[blk(0), blk(8), blk(16)],
        out_specs=pl.BlockSpec((1, C, HEAD_DIM), lambda b, h, c, s: (h, b * nc + c, 0)),
        scratch_shapes=[
            pltpu.VMEM((2 * C, HEAD_DIM), F32),
            pltpu.VMEM((2 * C, HEAD_DIM), F32),
            pltpu.VMEM((4, C // 4, HEAD_DIM), F32),
            pltpu.VMEM((4, 2 * C // 4, HEAD_DIM), F32),
            pltpu.VMEM((4, 2 * C // 4, HEAD_DIM), F32),
            pltpu.VMEM((C, HEAD_DIM), F32),
            pltpu.VMEM((C, HEAD_DIM), F32),
            pltpu.VMEM((4, C // 4, HEAD_DIM), F32),
            pltpu.VMEM((4, C // 4, HEAD_DIM), F32),
            pltpu.VMEM((4, C // 4, HEAD_DIM), F32),
            pltpu.VMEM((4, C // 4, HEAD_DIM), F32),
            pltpu.VMEM((C, HEAD_DIM), F32),
            pltpu.VMEM((2 * len(DILATIONS), SPAN, 2 * SPAN), F32),
        ],
    )
    return pl.pallas_call(
        _attn_kernel,
        grid_spec=grid_spec,
        out_shape=jax.ShapeDtypeStruct((N_ATTN_HEADS, T, HEAD_DIM), BF16),
        compiler_params=_params(("parallel", "parallel", "arbitrary"), 40),
        name="attn",
    )(slopes, qkv, qkv, qkv)


def _mixer_kernel(attn_ref, rest_ref, halo_ref, kvm_ref, x_ref, wout_ref, og_ref, g2_ref, cw_ref, wr_ref, br_ref,
                  x2_ref, h2_ref, route_ref, y_scr, z_scr, *, tiles_per_seq):
    i = pl.program_id(0)
    tm = MIXER_TM
    og = og_ref[...]

    ss = jnp.zeros((tm, 1), F32)
    for hh in range(N_ATTN_HEADS):
        a = attn_ref[hh].astype(F32)
        ss = ss + jnp.sum(a * a, axis=-1, keepdims=True)
    r = lax.rsqrt(ss * (1.0 / ATTN_W) + EPS)
    for hh in range(N_ATTN_HEADS):
        lo = hh * HEAD_DIM
        y_scr[:, lo:lo + HEAD_DIM] = (attn_ref[hh].astype(F32) * r * og[:, lo:lo + HEAD_DIM]).astype(BF16)

    seq_start = (i % tiles_per_seq) == 0
    convs = []
    ss = jnp.zeros((tm, 1), F32)
    for g in range(CONV_GROUPS):
        bg = rest_ref[g].astype(F32)
        z = rest_ref[CONV_GROUPS + g].astype(F32) * rest_ref[2 * CONV_GROUPS + g].astype(F32)
        zh = halo_ref[CONV_GROUPS + g].astype(F32) * halo_ref[2 * CONV_GROUPS + g].astype(F32)
        z_scr[g, 0:HALO, :] = jnp.where(seq_start, 0.0, zh)
        z_scr[g, HALO:HALO + tm, :] = z
        z1 = z_scr[g, HALO - 1:HALO - 1 + tm, :]
        z2 = z_scr[g, HALO - 2:HALO - 2 + tm, :]
        lo = g * HEAD_DIM
        w0 = cw_ref[0:1, lo:lo + HEAD_DIM]
        w1 = cw_ref[1:2, lo:lo + HEAD_DIM]
        w2 = cw_ref[2:3, lo:lo + HEAD_DIM]
        cv = bg * (w2 * z + w1 * z1 + w0 * z2)
        convs.append(cv)
        ss = ss + jnp.sum(cv * cv, axis=-1, keepdims=True)
    r = lax.rsqrt(ss * (1.0 / CONV_W) + EPS)
    for g in range(CONV_GROUPS):
        lo = ATTN_W + g * HEAD_DIM
        y_scr[:, lo:lo + HEAD_DIM] = (convs[g] * r * og[:, lo:lo + HEAD_DIM]).astype(BF16)

    mems = []
    ss = jnp.zeros((tm, 1), F32)
    for hh in range(N_MEM_HEADS):
        mq = rest_ref[3 * CONV_GROUPS + hh]
        s = lax.dot_general(mq, kvm_ref[hh], (((1,), (1,)), ((), ())), preferred_element_type=F32)
        m = jnp.max(s, axis=-1, keepdims=True)
        p = jnp.exp(s - m)
        l = jnp.sum(p, axis=-1, keepdims=True)
        o = jnp.dot(p.astype(BF16), kvm_ref[N_MEM_HEADS + hh], preferred_element_type=F32) * (1.0 / l)
        mems.append(o)
        ss = ss + jnp.sum(o * o, axis=-1, keepdims=True)
    r = lax.rsqrt(ss * (1.0 / MEM_W) + EPS)
    for hh in range(N_MEM_HEADS):
        lo = ATTN_W + CONV_W + hh * HEAD_DIM
        y_scr[:, lo:lo + HEAD_DIM] = (mems[hh] * r * og[:, lo:lo + HEAD_DIM]).astype(BF16)

    x2 = x_ref[...] + jnp.dot(y_scr[...], wout_ref[...], preferred_element_type=F32)
    x2_ref[...] = x2
    h2 = x2 * _rms(x2, D_MODEL) * g2_ref[...]
    h2_ref[...] = h2

    lg = jnp.dot(h2.astype(BF16), wr_ref[...], preferred_element_type=F32) + br_ref[...]
    lane = lax.broadcasted_iota(I32, (tm, 128), 1).astype(F32)
    gl = jnp.where(lane < N_GROUPS, lg, NEG_INF)
    gmax = jnp.max(gl, axis=-1, keepdims=True)
    g_w = 1.0 / jnp.sum(jnp.exp(gl - gmax), axis=-1, keepdims=True)
    g_idx = jnp.min(jnp.where(gl == gmax, lane, 1e9), axis=-1, keepdims=True)
    e_lo = N_GROUPS + EXPERTS_PER_GROUP * g_idx
    el = jnp.where((lane >= e_lo) & (lane < e_lo + EXPERTS_PER_GROUP), lg, NEG_INF)
    e1 = jnp.max(el, axis=-1, keepdims=True)
    i1 = jnp.min(jnp.where(el == e1, lane, 1e9), axis=-1, keepdims=True)
    el2 = jnp.where(lane == i1, NEG_INF, el)
    e2 = jnp.max(el2, axis=-1, keepdims=True)
    i2 = jnp.min(jnp.where(el2 == e2, lane, 1e9), axis=-1, keepdims=True)
    t = jnp.exp(e2 - e1)
    p1 = 1.0 / (1.0 + t)
    p2 = t * p1
    route = jnp.where(lane == 0, i1 - N_GROUPS,
                      jnp.where(lane == 1, i2 - N_GROUPS,
                                jnp.where(lane == 2, g_w * p1,
                                          jnp.where(lane == 3, g_w * p2, 0.0))))
    route_ref[...] = route


def _mixer(attn, rest, kvm, x2d, w_out_bf16, out_norm_g, norm2_g, conv_w, w_router, b_router, seq):
    T = x2d.shape[0]
    tm = MIXER_TM
    tiles_per_seq = seq // tm
    full = lambda shape: pl.BlockSpec(shape, lambda i: tuple(0 for _ in shape))
    return pl.pallas_call(
        functools.partial(_mixer_kernel, tiles_per_seq=tiles_per_seq),
        grid=(T // tm,),
        in_specs=[
            pl.BlockSpec((N_ATTN_HEADS, tm, HEAD_DIM), lambda i: (0, i, 0)),
            pl.BlockSpec((16, tm, HEAD_DIM), lambda i: (0, i, 0)),
            pl.BlockSpec((16, HALO, HEAD_DIM), lambda i: (0, jnp.maximum(i * (tm // HALO) - 1, 0), 0)),
            pl.BlockSpec((2 * N_MEM_HEADS, N_MEM, HEAD_DIM), lambda i: (0, i // tiles_per_seq, 0)),
            pl.BlockSpec((tm, D_MODEL), lambda i: (i, 0)),
            pl.BlockSpec((D_MODEL, D_MODEL), lambda i: (0, 0), pipeline_mode=pl.Buffered(1)),
            full((1, D_MODEL)),
            full((1, D_MODEL)),
            full((3, CONV_W)),
            full((D_MODEL, 128)),
            full((1, 128)),
        ],
        out_specs=[
            pl.BlockSpec((tm, D_MODEL), lambda i: (i, 0)),
            pl.BlockSpec((tm, D_MODEL), lambda i: (i, 0)),
            pl.BlockSpec((tm, 128), lambda i: (i, 0)),
        ],
        out_shape=[
            jax.ShapeDtypeStruct((T, D_MODEL), F32),
            jax.ShapeDtypeStruct((T, D_MODEL), F32),
            jax.ShapeDtypeStruct((T, 128), F32),
        ],
        scratch_shapes=[
            pltpu.VMEM((tm, D_MODEL), BF16),
            pltpu.VMEM((CONV_GROUPS, HALO + tm, HEAD_DIM), F32),
        ],
        compiler_params=_params(("arbitrary",), 56),
        name="mixer",
    )(attn, rest, rest, kvm, x2d, w_out_bf16, out_norm_g, norm2_g, conv_w, w_router, b_router)


def _dispatch_kernel(dest_ref, h_ref, xbuf_ref, sem):
    i = pl.program_id(0)
    tm = DISPATCH_TM
    base = i * tm

    def body(t, carry):
        a = 2 * (base + t)
        src = h_ref.at[pl.ds(t, 1)]
        pltpu.make_async_copy(src, xbuf_ref.at[pl.ds(dest_ref[a], 1)], sem).start()
        pltpu.make_async_copy(src, xbuf_ref.at[pl.ds(dest_ref[a + 1], 1)], sem).start()
        return carry

    lax.fori_loop(0, tm, body, 0, unroll=DMA_UNROLL)
    pltpu.make_async_copy(xbuf_ref.at[pl.ds(0, 2 * tm)], xbuf_ref.at[pl.ds(0, 2 * tm)], sem).wait()


def _dispatch(dest, h2, n_rows):
    T = h2.shape[0]
    tm = DISPATCH_TM
    grid_spec = pltpu.PrefetchScalarGridSpec(
        num_scalar_prefetch=1,
        grid=(T // tm,),
        in_specs=[pl.BlockSpec((tm, D_MODEL), lambda i, d: (i, 0))],
        out_specs=pl.BlockSpec(memory_space=pl.ANY),
        scratch_shapes=[pltpu.SemaphoreType.DMA],
    )
    return pl.pallas_call(
        _dispatch_kernel,
        grid_spec=grid_spec,
        out_shape=jax.ShapeDtypeStruct((n_rows, D_MODEL), F32),
        compiler_params=_params(("arbitrary",), 32),
        name="dispatch",
    )(dest, h2)


def _ffn_kernel(ie_ref, rows_ref, x_ref, wg_ref, wu_ref, wd_ref, y_ref, xg, xb, acc, xsem, ysem):
    w = pl.program_id(0)
    c = pl.program_id(1)
    n_chunks = D_EXPERT // FFN_CHUNK

    def subs(v):
        return (v + FFN_SUB - 1) // FFN_SUB

    rows = rows_ref[w]
    ns = subs(rows)
    ns_next = subs(rows_ref[w + 1])
    ns_prev = jnp.where(w > 0, subs(rows_ref[jnp.maximum(w - 1, 0)]), 0)
    ns_prev2 = jnp.where(w > 1, subs(rows_ref[jnp.maximum(w - 2, 0)]), 0)
    p = w % 2

    def x_copy(item, s):
        lo, n = FFN_PIECES[s]
        return pltpu.make_async_copy(x_ref.at[pl.ds(item * ITEM_ROWS + lo, n)], xg.at[pl.ds(lo, n)], xsem)

    def y_copy(item, slot, s):
        lo, n = FFN_PIECES[s]
        return pltpu.make_async_copy(acc.at[slot, pl.ds(lo, n)], y_ref.at[pl.ds(item * ITEM_ROWS + lo, n)],
                                     ysem.at[slot])

    def for_pieces(pieces, count, fn):
        for s in pieces:
            @pl.when(s < count)
            def _(s=s):
                fn(s)

    all_pieces = range(len(FFN_PIECES))

    @pl.when(c == 0)
    def _():
        @pl.when(w == 0)
        def _():
            for_pieces(all_pieces, ns, lambda s: x_copy(0, s).start())

        for_pieces(all_pieces, ns, lambda s: x_copy(w, s).wait())
        for_pieces(all_pieces, ns_prev2, lambda s: y_copy(w - 2, p, s).wait())
        for_pieces(Y_PIECES_LATE, ns_prev, lambda s: y_copy(w - 1, 1 - p, s).start())

    for ns_static in range(1, len(FFN_PIECES) + 1):
        nr = FFN_PIECES[ns_static - 1][0] + FFN_PIECES[ns_static - 1][1]

        @pl.when(ns == ns_static)
        def _(nr=nr):
            @pl.when(c == 0)
            def _():
                ok = lax.broadcasted_iota(I32, (nr, 1), 0) < rows
                xb[0:nr, :] = jnp.where(ok, xg[0:nr, :], 0.0).astype(BF16)
                acc[p, 0:nr, :] = jnp.zeros((nr, D_MODEL), F32)

            wg = wg_ref[0].astype(BF16)
            wu = wu_ref[0].astype(BF16)
            wd = wd_ref[0].astype(BF16)
            x = xb[0:nr, :]
            a = jnp.dot(x, wg, preferred_element_type=F32)
            u = jnp.dot(x, wu, preferred_element_type=F32)
            hm = (a / (1.0 + jnp.exp(-a)) * u).astype(BF16)
            acc[p, 0:nr, :] += jnp.dot(hm, wd, preferred_element_type=F32)

    for step, pieces in X_PIECES_AT_STEP.items():
        @pl.when(c == step)
        def _(pieces=pieces):
            for_pieces(pieces, ns_next, lambda s: x_copy(w + 1, s).start())

    @pl.when(c == n_chunks - 1)
    def _():
        for_pieces(Y_PIECES_EARLY, ns, lambda s: y_copy(w, p, s).start())


def _ffn(item_expert, item_rows, xbuf, w_gate, w_up, w_down):
    n_chunks = D_EXPERT // FFN_CHUNK
    grid_items = item_rows.shape[0] - 1

    def chunk_of(w, c, ir):
        return jnp.where(ir[w] > 0, c, n_chunks - 1)

    grid_spec = pltpu.PrefetchScalarGridSpec(
        num_scalar_prefetch=2,
        grid=(grid_items, n_chunks),
        in_specs=[
            pl.BlockSpec(memory_space=pl.ANY),
            pl.BlockSpec((1, D_MODEL, FFN_CHUNK), lambda w, c, ie, ir: (ie[w], 0, chunk_of(w, c, ir))),
            pl.BlockSpec((1, D_MODEL, FFN_CHUNK), lambda w, c, ie, ir: (ie[w], 0, chunk_of(w, c, ir))),
            pl.BlockSpec((1, FFN_CHUNK, D_MODEL), lambda w, c, ie, ir: (ie[w], chunk_of(w, c, ir), 0)),
        ],
        out_specs=pl.BlockSpec(memory_space=pl.ANY),
        scratch_shapes=[
            pltpu.VMEM((ITEM_ROWS, D_MODEL), F32),
            pltpu.VMEM((ITEM_ROWS, D_MODEL), BF16),
            pltpu.VMEM((2, ITEM_ROWS, D_MODEL), F32),
            pltpu.SemaphoreType.DMA,
            pltpu.SemaphoreType.DMA((2,)),
        ],
    )
    return pl.pallas_call(
        _ffn_kernel,
        grid_spec=grid_spec,
        out_shape=jax.ShapeDtypeStruct(xbuf.shape, F32),
        compiler_params=_params(("arbitrary", "arbitrary"), 58),
        name="ffn",
    )(item_expert, item_rows, xbuf, w_gate, w_up, w_down)


def _combine_kernel(dest_ref, x2_ref, route_ref, ybuf_ref, o_ref, rows_scr, sem):
    i = pl.program_id(0)
    n = pl.num_programs(0)
    tm = COMBINE_TM

    def gather_tile(tile, slot):
        base = tile * tm

        def body(t, carry):
            a = 2 * (base + t)
            pltpu.make_async_copy(ybuf_ref.at[pl.ds(dest_ref[a], 1)], rows_scr.at[slot, 0, pl.ds(t, 1)],
                                  sem.at[slot]).start()
            pltpu.make_async_copy(ybuf_ref.at[pl.ds(dest_ref[a + 1], 1)], rows_scr.at[slot, 1, pl.ds(t, 1)],
                                  sem.at[slot]).start()
            return carry

        lax.fori_loop(0, tm, body, 0, unroll=DMA_UNROLL)

    slot = i % 2

    @pl.when(i == 0)
    def _():
        gather_tile(0, 0)

    @pl.when(i + 1 < n)
    def _():
        gather_tile(i + 1, 1 - slot)

    for k in range(2):
        pltpu.make_async_copy(ybuf_ref.at[pl.ds(0, tm)], rows_scr.at[slot, k], sem.at[slot]).wait()

    o_ref[...] = x2_ref[...] + route_ref[:, 2:3] * rows_scr[slot, 0] + route_ref[:, 3:4] * rows_scr[slot, 1]


def _combine(dest, x2, route, ybuf):
    T = x2.shape[0]
    tm = COMBINE_TM
    grid_spec = pltpu.PrefetchScalarGridSpec(
        num_scalar_prefetch=1,
        grid=(T // tm,),
        in_specs=[
            pl.BlockSpec((tm, D_MODEL), lambda i, d: (i, 0)),
            pl.BlockSpec((tm, 128), lambda i, d: (i, 0)),
            pl.BlockSpec(memory_space=pl.ANY),
        ],
        out_specs=pl.BlockSpec((tm, D_MODEL), lambda i, d: (i, 0)),
        scratch_shapes=[pltpu.VMEM((2, 2, tm, D_MODEL), F32), pltpu.SemaphoreType.DMA((2,))],
    )
    return pl.pallas_call(
        _combine_kernel,
        grid_spec=grid_spec,
        out_shape=jax.ShapeDtypeStruct((T, D_MODEL), F32),
        compiler_params=_params(("arbitrary",), 40),
        name="combine",
    )(dest, x2, route, ybuf)


def _moe_plan(route, max_items):
    e_flat = route[:, :2].astype(I32).reshape(-1)
    onehot = (e_flat[:, None] == jnp.arange(N_EXPERTS, dtype=I32)[None, :]).astype(I32)
    csum = jnp.cumsum(onehot, axis=0)
    counts = csum[-1]
    rank = jnp.sum(onehot * csum, axis=1) - 1
    n_it = (counts + ITEM_ROWS - 1) // ITEM_ROWS
    it_end = jnp.cumsum(n_it)
    it_start = it_end - n_it
    dest = jnp.sum(onehot * (it_start * ITEM_ROWS)[None, :], axis=1) + rank
    n_items = it_end[-1]
    w = jnp.arange(max_items + 3, dtype=I32)
    ie = jnp.minimum(jnp.sum((it_end[None, :] <= w[:, None]).astype(I32), axis=1), N_EXPERTS - 1)
    used = w < n_items
    rows = jnp.where(used, jnp.minimum(ITEM_ROWS, counts[ie] - (w - it_start[ie]) * ITEM_ROWS), 0)
    ie = jnp.where(used, ie, ie[jnp.maximum(n_items - 1, 0)])
    return dest.astype(I32), ie.astype(I32), rows.astype(I32)


def _layer(x, mem, norm1_g, w_in, q_norm_g, k_norm_g, conv_w, mem_norm_g, w_mem_kv, mem_q_norm_g, mem_k_norm_g,
           out_norm_g, w_out, norm2_g, w_rg, b_rg, w_re, b_re, w_gate, w_up, w_down):
    B, S, D = x.shape
    T = B * S
    x2d = x.reshape(T, D)
    row = lambda v: v.reshape(1, -1).astype(F32)

    kvm = _memkv(mem.reshape(B * N_MEM, D), row(mem_norm_g), w_mem_kv, row(mem_k_norm_g))
    qkv, rest = _inproj(x2d, row(norm1_g), w_in.astype(BF16), row(q_norm_g), row(k_norm_g), row(mem_q_norm_g))

    slopes = 2.0 ** (-8.0 * jnp.arange(1, N_ATTN_HEADS + 1, dtype=F32) / N_ATTN_HEADS)
    attn = _attention(qkv, slopes, B, S)

    w_router = jnp.zeros((D, 128), F32).at[:, :N_GROUPS].set(w_rg).at[:, N_GROUPS:N_GROUPS + N_EXPERTS].set(w_re)
    b_router = jnp.zeros((1, 128), F32).at[0, :N_GROUPS].set(b_rg).at[0, N_GROUPS:N_GROUPS + N_EXPERTS].set(b_re)
    x2, h2, route = _mixer(attn, rest, kvm, x2d, w_out.astype(BF16), row(out_norm_g), row(norm2_g),
                           conv_w.astype(F32), w_router.astype(BF16), b_router, S)

    max_items = N_EXPERTS + (2 * T) // ITEM_ROWS
    dest, item_expert, item_rows = _moe_plan(route, max_items)
    xbuf = _dispatch(dest, h2, max_items * ITEM_ROWS)
    ybuf = _ffn(item_expert, item_rows, xbuf, w_gate, w_up, w_down)
    out = _combine(dest, x2, route, ybuf)
    return out.reshape(B, S, D)


def kernel(x, mem, norm1_g, w_in, q_norm_g, k_norm_g, conv_w, mem_norm_g, w_mem_kv, mem_q_norm_g, mem_k_norm_g,
           out_norm_g, w_out, norm2_g, w_router_group, b_router_group, w_router_expert, b_router_expert,
           w_gate, w_up, w_down):
    for l in range(norm1_g.shape[0]):
        x = _layer(x, mem, norm1_g[l], w_in[l], q_norm_g[l], k_norm_g[l], conv_w[l], mem_norm_g[l], w_mem_kv[l],
                   mem_q_norm_g[l], mem_k_norm_g[l], out_norm_g[l], w_out[l], norm2_g[l],
                   w_router_group[l], b_router_group[l], w_router_expert[l], b_router_expert[l],
                   w_gate[l], w_up[l], w_down[l])
    return x
```

```python
import functools

import jax
import jax.numpy as jnp
from jax import lax
from jax.experimental import pallas as pl
from jax.experimental.pallas import tpu as pltpu

F32 = jnp.float32
BF16 = jnp.bfloat16
I32 = jnp.int32

D_MODEL = 2048
HEAD_DIM = 128
N_ATTN_HEADS = 8
ATTN_W = N_ATTN_HEADS * HEAD_DIM
N_MEM_HEADS = 4
MEM_W = N_MEM_HEADS * HEAD_DIM
CONV_W = D_MODEL - ATTN_W - MEM_W
CONV_GROUPS = CONV_W // HEAD_DIM
N_MEM = 256
DILATIONS = (1, 4, 16)
SPAN = 128
PROJ_W = 3 * ATTN_W + 3 * CONV_W + MEM_W
N_GROUPS = 8
EXPERTS_PER_GROUP = 8
N_EXPERTS = N_GROUPS * EXPERTS_PER_GROUP
D_EXPERT = D_MODEL // 2
EPS = 1e-6
NEG_INF = -1e30
QK_SCALE = HEAD_DIM ** -0.5
LOG2E = 1.4426950408889634

MIB = 1024 * 1024
INPROJ_TM = 256
INPROJ_TN = 1024
ATTN_CHUNK = 2048
MIXER_TM = 512
DISPATCH_TM = 512
COMBINE_TM = 256
ITEM_ROWS = 1152
FFN_SUB = 256
FFN_PIECES = tuple((lo, min(FFN_SUB, ITEM_ROWS - lo)) for lo in range(0, ITEM_ROWS, FFN_SUB))
X_PIECES_AT_STEP = {1: (0, 1), 2: (2, 3), 3: (4,)}
Y_PIECES_EARLY = (0, 1)
Y_PIECES_LATE = (2, 3, 4)
FFN_CHUNK = 256
HALO = 16
DMA_UNROLL = 8
ATTN_UNROLL = 16


def _params(semantics, vmem_mib):
    return pltpu.CompilerParams(dimension_semantics=semantics, vmem_limit_bytes=vmem_mib * MIB)


def _rms(v, width):
    return lax.rsqrt(jnp.sum(v * v, axis=-1, keepdims=True) * (1.0 / width) + EPS)


def _memkv_kernel(mem_ref, g_ref, w_ref, kg_ref, o_ref):
    j = pl.program_id(0)
    m = mem_ref[...]
    h = (m * _rms(m, D_MODEL) * g_ref[...]).astype(BF16)
    kv = jnp.dot(h, w_ref[...].astype(BF16), preferred_element_type=F32)
    is_key = j == 0
    for hh in range(N_MEM_HEADS):
        t = kv[:, hh * HEAD_DIM:(hh + 1) * HEAD_DIM]
        tn = t * _rms(t, HEAD_DIM) * kg_ref[...]
        o_ref[hh] = jnp.where(is_key, tn, t).astype(BF16)


def _memkv(mem2d, mem_norm_g, w_mem_kv, mem_k_norm_g):
    rows = mem2d.shape[0]
    return pl.pallas_call(
        _memkv_kernel,
        grid=(2,),
        in_specs=[
            pl.BlockSpec((rows, D_MODEL), lambda j: (0, 0)),
            pl.BlockSpec((1, D_MODEL), lambda j: (0, 0)),
            pl.BlockSpec((D_MODEL, MEM_W), lambda j: (0, j)),
            pl.BlockSpec((1, HEAD_DIM), lambda j: (0, 0)),
        ],
        out_specs=pl.BlockSpec((N_MEM_HEADS, rows, HEAD_DIM), lambda j: (j, 0, 0)),
        out_shape=jax.ShapeDtypeStruct((2 * N_MEM_HEADS, rows, HEAD_DIM), BF16),
        compiler_params=_params(("arbitrary",), 40),
        name="memkv",
    )(mem2d, mem_norm_g, w_mem_kv, mem_k_norm_g)


def _inproj_kernel(x_ref, g1_ref, w_ref, qg_ref, kg_ref, mqg_ref, qkv_ref, rest_ref):
    x = x_ref[...]
    h = (x * _rms(x, D_MODEL) * g1_ref[...]).astype(BF16)

    def normed(t, g):
        return t * _rms(t, HEAD_DIM) * g

    gq = qg_ref[...] * (QK_SCALE * LOG2E)
    gk = kg_ref[...]
    gmq = mqg_ref[...] * QK_SCALE
    groups_per_tile = INPROJ_TN // HEAD_DIM
    for j in range(PROJ_W // INPROJ_TN):
        y = jnp.dot(h, w_ref[:, j * INPROJ_TN:(j + 1) * INPROJ_TN], preferred_element_type=F32)
        for hh in range(groups_per_tile):
            t = y[:, hh * HEAD_DIM:(hh + 1) * HEAD_DIM]
            col = j * groups_per_tile + hh
            if col < 8:
                qkv_ref[col] = normed(t, gq)
            elif col < 16:
                qkv_ref[col] = normed(t, gk)
            elif col < 24:
                qkv_ref[col] = t
            elif col < 36:
                rest_ref[col - 24] = t.astype(BF16)
            else:
                rest_ref[col - 24] = normed(t, gmq).astype(BF16)


def _inproj(x2d, norm1_g, w_in_bf16, q_norm_g, k_norm_g, mem_q_norm_g):
    T = x2d.shape[0]
    tm = INPROJ_TM
    small = lambda n: pl.BlockSpec((1, n), lambda i: (0, 0))
    return pl.pallas_call(
        _inproj_kernel,
        grid=(T // tm,),
        in_specs=[
            pl.BlockSpec((tm, D_MODEL), lambda i: (i, 0)),
            small(D_MODEL),
            pl.BlockSpec((D_MODEL, PROJ_W), lambda i: (0, 0), pipeline_mode=pl.Buffered(1)),
            small(HEAD_DIM), small(HEAD_DIM), small(HEAD_DIM),
        ],
        out_specs=[
            pl.BlockSpec((24, tm, HEAD_DIM), lambda i: (0, i, 0)),
            pl.BlockSpec((16, tm, HEAD_DIM), lambda i: (0, i, 0)),
        ],
        out_shape=[
            jax.ShapeDtypeStruct((24, T, HEAD_DIM), F32),
            jax.ShapeDtypeStruct((16, T, HEAD_DIM), BF16),
        ],
        compiler_params=_params(("parallel",), 52),
        name="inproj",
    )(x2d, norm1_g, w_in_bf16, q_norm_g, k_norm_g, mem_q_norm_g)


def _attn_kernel(slopes_ref, q_ref, k_ref, v_ref, o_ref,
                 kx, vx, q4, k4, v4, o1, l1, o4, l4, o16, l16, out_scr, bias_scr):
    h = pl.program_id(1)
    c = pl.program_id(2)
    C = ATTN_CHUNK

    @pl.when(c == 0)
    def _():
        kx[0:C, :] = jnp.zeros((C, HEAD_DIM), F32)
        vx[0:C, :] = jnp.zeros((C, HEAD_DIM), F32)

    @pl.when(c > 0)
    def _():
        kx[0:C, :] = kx[C:2 * C, :]
        vx[0:C, :] = vx[C:2 * C, :]

    kx[C:2 * C, :] = k_ref[0]
    vx[C:2 * C, :] = v_ref[0]

    for b in range(4):
        q4[b] = q_ref[0, pl.ds(b, C // 4, stride=4), :]
        k4[b] = kx[pl.ds(b, 2 * C // 4, stride=4), :]
        v4[b] = vx[pl.ds(b, 2 * C // 4, stride=4), :]

    slope = slopes_ref[h]
    ii = lax.broadcasted_iota(I32, (SPAN, 2 * SPAN), 0)
    jj = lax.broadcasted_iota(I32, (SPAN, 2 * SPAN), 1)
    dist = ii - jj + SPAN
    band = (dist >= 0) & (dist <= SPAN)
    distf = dist.astype(F32)

    for bi, d in enumerate(DILATIONS):
        neg = (-slope * float(d) * LOG2E) * distf
        bias_scr[2 * bi] = jnp.where(band, neg, NEG_INF)
        bias_scr[2 * bi + 1] = jnp.where(band & (jj >= SPAN), neg, NEG_INF)

    ones = jnp.ones((2 * SPAN, HEAD_DIM), BF16)

    def attend(q, k, v, bias):
        s = lax.dot_general(q.astype(BF16), k.astype(BF16), (((1,), (1,)), ((), ())),
                            preferred_element_type=F32) + bias
        m = jnp.max(s, axis=-1, keepdims=True)
        p = jnp.exp2(s - m).astype(BF16)
        ol = jnp.dot(p, jnp.concatenate([v.astype(BF16), ones], axis=1), preferred_element_type=F32)
        l = ol[:, HEAD_DIM:]
        return ol[:, :HEAD_DIM] * (1.0 / l), m + jnp.log2(l)

    def unit1(u, carry):
        sq = pl.multiple_of(u * SPAN, SPAN)
        sk = pl.multiple_of(C + u * SPAN - SPAN, SPAN)
        first = jnp.logical_and(c == 0, u == 0).astype(I32)
        o, lse = attend(q_ref[0, pl.ds(sq, SPAN), :], kx[pl.ds(sk, 2 * SPAN), :], vx[pl.ds(sk, 2 * SPAN), :],
                        bias_scr[first])
        o1[pl.ds(sq, SPAN), :] = o
        l1[pl.ds(sq, SPAN), :] = lse
        return carry

    def unit4(u, carry):
        b = u % 4
        qb = u // 4
        sq = pl.multiple_of(qb * SPAN, SPAN)
        sk = pl.multiple_of(C // 4 + qb * SPAN - SPAN, SPAN)
        first = jnp.logical_and(c == 0, qb == 0).astype(I32)
        o, lse = attend(q4[b, pl.ds(sq, SPAN), :], k4[b, pl.ds(sk, 2 * SPAN), :], v4[b, pl.ds(sk, 2 * SPAN), :],
                        bias_scr[2 + first])
        o4[b, pl.ds(sq, SPAN), :] = o
        l4[b, pl.ds(sq, SPAN), :] = lse
        return carry

    def unit16(u, carry):
        b = u % 4
        a = u // 4
        first = (c == 0).astype(I32)
        qi = pl.ds(a, SPAN, stride=4)
        ki = pl.ds(a, 2 * SPAN, stride=4)
        o, lse = attend(q4[b, qi, :], k4[b, ki, :], v4[b, ki, :], bias_scr[4 + first])
        o16[b, qi, :] = o
        l16[b, qi, :] = lse
        return carry

    for unit in (unit1, unit4, unit16):
        lax.fori_loop(0, C // SPAN, unit, 0, unroll=ATTN_UNROLL)

    for b in range(4):
        rows_b = pl.ds(b, C // 4, stride=4)
        la, lb, lc = l1[rows_b, :], l4[b], l16[b]
        mx = jnp.maximum(jnp.maximum(la, lb), lc)
        ea, eb, ec = jnp.exp2(la - mx), jnp.exp2(lb - mx), jnp.exp2(lc - mx)
        out_scr[rows_b, :] = (ea * o1[rows_b, :] + eb * o4[b] + ec * o16[b]) * (1.0 / (ea + eb + ec))
    o_ref[0] = out_scr[...].astype(BF16)


def _attention(qkv, slopes, batch, seq):
    T = qkv.shape[1]
    C = ATTN_CHUNK
    nc = seq // C
    blk = lambda off: pl.BlockSpec((1, C, HEAD_DIM), lambda b, h, c, s: (h + off, b * nc + c, 0))
    grid_spec = pltpu.PrefetchScalarGridSpec(
        num_scalar_prefetch=1,
        grid=(batch, N_ATTN_HEADS, nc),
        in_specs=[blk(0), blk(8), blk(16)],
        out_specs=pl.BlockSpec((1, C, HEAD_DIM), lambda b, h, c, s: (h, b * nc + c, 0)),
        scratch_shapes=[
            pltpu.VMEM((2 * C, HEAD_DIM), F32),
            pltpu.VMEM((2 * C, HEAD_DIM), F32),
            pltpu.VMEM((4, C // 4, HEAD_DIM), F32),
            pltpu.VMEM((4, 2 * C // 4, HEAD_DIM), F32),
            pltpu.VMEM((4, 2 * C // 4, HEAD_DIM), F32),
            pltpu.VMEM((C, HEAD_DIM), F32),
            pltpu.VMEM((C, HEAD_DIM), F32),
            pltpu.VMEM((4, C // 4, HEAD_DIM), F32),
            pltpu.VMEM((4, C // 4, HEAD_DIM), F32),
            pltpu.VMEM((4, C // 4, HEAD_DIM), F32),
            pltpu.VMEM((4, C // 4, HEAD_DIM), F32),
            pltpu.VMEM((C, HEAD_DIM), F32),
            pltpu.VMEM((2 * len(DILATIONS), SPAN, 2 * SPAN), F32),
        ],
    )
    return pl.pallas_call(
        _attn_kernel,
        grid_spec=grid_spec,
        out_shape=jax.ShapeDtypeStruct((N_ATTN_HEADS, T, HEAD_DIM), BF16),
        compiler_params=_params(("parallel", "parallel", "arbitrary"), 40),
        name="attn",
    )(slopes, qkv, qkv, qkv)


def _mixer_kernel(attn_ref, rest_ref, halo_ref, kvm_ref, x_ref, wout_ref, og_ref, g2_ref, cw_ref, wr_ref, br_ref,
                  x2_ref, h2_ref, route_ref, counts_ref, y_scr, z_scr, cnt_scr, *, tiles_per_seq):
    i = pl.program_id(0)
    tm = MIXER_TM
    og = og_ref[...]

    ss = jnp.zeros((tm, 1), F32)
    for hh in range(N_ATTN_HEADS):
        a = attn_ref[hh].astype(F32)
        ss = ss + jnp.sum(a * a, axis=-1, keepdims=True)
    r = lax.rsqrt(ss * (1.0 / ATTN_W) + EPS)
    for hh in range(N_ATTN_HEADS):
        lo = hh * HEAD_DIM
        y_scr[:, lo:lo + HEAD_DIM] = (attn_ref[hh].astype(F32) * r * og[:, lo:lo + HEAD_DIM]).astype(BF16)

    seq_start = (i % tiles_per_seq) == 0
    convs = []
    ss = jnp.zeros((tm, 1), F32)
    for g in range(CONV_GROUPS):
        bg = rest_ref[g].astype(F32)
        z = rest_ref[CONV_GROUPS + g].astype(F32) * rest_ref[2 * CONV_GROUPS + g].astype(F32)
        zh = halo_ref[CONV_GROUPS + g].astype(F32) * halo_ref[2 * CONV_GROUPS + g].astype(F32)
        z_scr[g, 0:HALO, :] = jnp.where(seq_start, 0.0, zh)
        z_scr[g, HALO:HALO + tm, :] = z
        z1 = z_scr[g, HALO - 1:HALO - 1 + tm, :]
        z2 = z_scr[g, HALO - 2:HALO - 2 + tm, :]
        lo = g * HEAD_DIM
        w0 = cw_ref[0:1, lo:lo + HEAD_DIM]
        w1 = cw_ref[1:2, lo:lo + HEAD_DIM]
        w2 = cw_ref[2:3, lo:lo + HEAD_DIM]
        cv = bg * (w2 * z + w1 * z1 + w0 * z2)
        convs.append(cv)
        ss = ss + jnp.sum(cv * cv, axis=-1, keepdims=True)
    r = lax.rsqrt(ss * (1.0 / CONV_W) + EPS)
    for g in range(CONV_GROUPS):
        lo = ATTN_W + g * HEAD_DIM
        y_scr[:, lo:lo + HEAD_DIM] = (convs[g] * r * og[:, lo:lo + HEAD_DIM]).astype(BF16)

    mems = []
    ss = jnp.zeros((tm, 1), F32)
    for hh in range(N_MEM_HEADS):
        mq = rest_ref[3 * CONV_GROUPS + hh]
        s = lax.dot_general(mq, kvm_ref[hh], (((1,), (1,)), ((), ())), preferred_element_type=F32)
        m = jnp.max(s, axis=-1, keepdims=True)
        p = jnp.exp(s - m)
        l = jnp.sum(p, axis=-1, keepdims=True)
        o = jnp.dot(p.astype(BF16), kvm_ref[N_MEM_HEADS + hh], preferred_element_type=F32) * (1.0 / l)
        mems.append(o)
        ss = ss + jnp.sum(o * o, axis=-1, keepdims=True)
    r = lax.rsqrt(ss * (1.0 / MEM_W) + EPS)
    for hh in range(N_MEM_HEADS):
        lo = ATTN_W + CONV_W + hh * HEAD_DIM
        y_scr[:, lo:lo + HEAD_DIM] = (mems[hh] * r * og[:, lo:lo + HEAD_DIM]).astype(BF16)

    x2 = x_ref[...] + jnp.dot(y_scr[...], wout_ref[...], preferred_element_type=F32)
    x2_ref[...] = x2
    h2 = x2 * _rms(x2, D_MODEL) * g2_ref[...]
    h2_ref[...] = h2

    lg = jnp.dot(h2.astype(BF16), wr_ref[...], preferred_element_type=F32) + br_ref[...]
    lane = lax.broadcasted_iota(I32, (tm, 128), 1).astype(F32)
    gl = jnp.where(lane < N_GROUPS, lg, NEG_INF)
    gmax = jnp.max(gl, axis=-1, keepdims=True)
    g_w = 1.0 / jnp.sum(jnp.exp(gl - gmax), axis=-1, keepdims=True)
    g_idx = jnp.min(jnp.where(gl == gmax, lane, 1e9), axis=-1, keepdims=True)
    e_lo = N_GROUPS + EXPERTS_PER_GROUP * g_idx
    el = jnp.where((lane >= e_lo) & (lane < e_lo + EXPERTS_PER_GROUP), lg, NEG_INF)
    e1 = jnp.max(el, axis=-1, keepdims=True)
    i1 = jnp.min(jnp.where(el == e1, lane, 1e9), axis=-1, keepdims=True)
    el2 = jnp.where(lane == i1, NEG_INF, el)
    e2 = jnp.max(el2, axis=-1, keepdims=True)
    i2 = jnp.min(jnp.where(el2 == e2, lane, 1e9), axis=-1, keepdims=True)
    t = jnp.exp(e2 - e1)
    p1 = 1.0 / (1.0 + t)
    p2 = t * p1

    @pl.when(i == 0)
    def _():
        cnt_scr[...] = jnp.zeros((1, 128), F32)

    x1 = i1 - N_GROUPS
    x2 = i2 - N_GROUPS
    hot1 = lane == x1
    hot2 = lane == x2
    hot = jnp.where(hot1 | hot2, 1.0, 0.0)
    row_id = lax.broadcasted_iota(I32, (tm, tm), 0)
    col_id = lax.broadcasted_iota(I32, (tm, tm), 1)
    earlier = jnp.where(col_id < row_id, 1.0, 0.0).astype(BF16)
    before = jnp.dot(earlier, hot.astype(BF16), preferred_element_type=F32) + cnt_scr[...]
    rank1 = jnp.sum(jnp.where(hot1, before, 0.0), axis=-1, keepdims=True)
    rank2 = jnp.sum(jnp.where(hot2, before, 0.0), axis=-1, keepdims=True)
    cnt_scr[...] = cnt_scr[...] + jnp.sum(hot, axis=0, keepdims=True)
    counts_ref[...] = cnt_scr[...]

    route = jnp.where(lane == 0, x1,
                      jnp.where(lane == 1, x2,
                                jnp.where(lane == 2, g_w * p1,
                                          jnp.where(lane == 3, g_w * p2,
                                                    jnp.where(lane == 4, rank1,
                                                              jnp.where(lane == 5, rank2, 0.0))))))
    route_ref[...] = route


def _mixer(attn, rest, kvm, x2d, w_out_bf16, out_norm_g, norm2_g, conv_w, w_router, b_router, seq):
    T = x2d.shape[0]
    tm = MIXER_TM
    tiles_per_seq = seq // tm
    full = lambda shape: pl.BlockSpec(shape, lambda i: tuple(0 for _ in shape))
    return pl.pallas_call(
        functools.partial(_mixer_kernel, tiles_per_seq=tiles_per_seq),
        grid=(T // tm,),
        in_specs=[
            pl.BlockSpec((N_ATTN_HEADS, tm, HEAD_DIM), lambda i: (0, i, 0)),
            pl.BlockSpec((16, tm, HEAD_DIM), lambda i: (0, i, 0)),
            pl.BlockSpec((16, HALO, HEAD_DIM), lambda i: (0, jnp.maximum(i * (tm // HALO) - 1, 0), 0)),
            pl.BlockSpec((2 * N_MEM_HEADS, N_MEM, HEAD_DIM), lambda i: (0, i // tiles_per_seq, 0)),
            pl.BlockSpec((tm, D_MODEL), lambda i: (i, 0)),
            pl.BlockSpec((D_MODEL, D_MODEL), lambda i: (0, 0), pipeline_mode=pl.Buffered(1)),
            full((1, D_MODEL)),
            full((1, D_MODEL)),
            full((3, CONV_W)),
            full((D_MODEL, 128)),
            full((1, 128)),
        ],
        out_specs=[
            pl.BlockSpec((tm, D_MODEL), lambda i: (i, 0)),
            pl.BlockSpec((tm, D_MODEL), lambda i: (i, 0)),
            pl.BlockSpec((tm, 128), lambda i: (i, 0)),
            pl.BlockSpec((1, 128), lambda i: (0, 0)),
        ],
        out_shape=[
            jax.ShapeDtypeStruct((T, D_MODEL), F32),
            jax.ShapeDtypeStruct((T, D_MODEL), F32),
            jax.ShapeDtypeStruct((T, 128), F32),
            jax.ShapeDtypeStruct((1, 128), F32),
        ],
        scratch_shapes=[
            pltpu.VMEM((tm, D_MODEL), BF16),
            pltpu.VMEM((CONV_GROUPS, HALO + tm, HEAD_DIM), F32),
            pltpu.VMEM((1, 128), F32),
        ],
        compiler_params=_params(("arbitrary",), 56),
        name="mixer",
    )(attn, rest, rest, kvm, x2d, w_out_bf16, out_norm_g, norm2_g, conv_w, w_router, b_router)


def _dispatch_kernel(dest_ref, h_ref, xbuf_ref, sem):
    i = pl.program_id(0)
    tm = DISPATCH_TM
    base = i * tm

    def body(t, carry):
        a = 2 * (base + t)
        src = h_ref.at[pl.ds(t, 1)]
        pltpu.make_async_copy(src, xbuf_ref.at[pl.ds(dest_ref[a], 1)], sem).start()
        pltpu.make_async_copy(src, xbuf_ref.at[pl.ds(dest_ref[a + 1], 1)], sem).start()
        return carry

    lax.fori_loop(0, tm, body, 0, unroll=DMA_UNROLL)
    pltpu.make_async_copy(xbuf_ref.at[pl.ds(0, 2 * tm)], xbuf_ref.at[pl.ds(0, 2 * tm)], sem).wait()


def _dispatch(dest, h2, n_rows):
    T = h2.shape[0]
    tm = DISPATCH_TM
    grid_spec = pltpu.PrefetchScalarGridSpec(
        num_scalar_prefetch=1,
        grid=(T // tm,),
        in_specs=[pl.BlockSpec((tm, D_MODEL), lambda i, d: (i, 0))],
        out_specs=pl.BlockSpec(memory_space=pl.ANY),
        scratch_shapes=[pltpu.SemaphoreType.DMA],
    )
    return pl.pallas_call(
        _dispatch_kernel,
        grid_spec=grid_spec,
        out_shape=jax.ShapeDtypeStruct((n_rows, D_MODEL), F32),
        compiler_params=_params(("arbitrary",), 32),
        name="dispatch",
    )(dest, h2)


def _ffn_kernel(ie_ref, rows_ref, x_ref, wg_ref, wu_ref, wd_ref, y_ref, xg, xb, acc, xsem, ysem):
    w = pl.program_id(0)
    c = pl.program_id(1)
    n_chunks = D_EXPERT // FFN_CHUNK

    def subs(v):
        return (v + FFN_SUB - 1) // FFN_SUB

    rows = rows_ref[w]
    ns = subs(rows)
    ns_next = subs(rows_ref[w + 1])
    ns_prev = jnp.where(w > 0, subs(rows_ref[jnp.maximum(w - 1, 0)]), 0)
    ns_prev2 = jnp.where(w > 1, subs(rows_ref[jnp.maximum(w - 2, 0)]), 0)
    p = w % 2

    def x_copy(item, s):
        lo, n = FFN_PIECES[s]
        return pltpu.make_async_copy(x_ref.at[pl.ds(item * ITEM_ROWS + lo, n)], xg.at[pl.ds(lo, n)], xsem)

    def y_copy(item, slot, s):
        lo, n = FFN_PIECES[s]
        return pltpu.make_async_copy(acc.at[slot, pl.ds(lo, n)], y_ref.at[pl.ds(item * ITEM_ROWS + lo, n)],
                                     ysem.at[slot])

    def for_pieces(pieces, count, fn):
        for s in pieces:
            @pl.when(s < count)
            def _(s=s):
                fn(s)

    all_pieces = range(len(FFN_PIECES))

    @pl.when(c == 0)
    def _():
        @pl.when(w == 0)
        def _():
            for_pieces(all_pieces, ns, lambda s: x_copy(0, s).start())

        for_pieces(all_pieces, ns, lambda s: x_copy(w, s).wait())
        for_pieces(all_pieces, ns_prev2, lambda s: y_copy(w - 2, p, s).wait())
        for_pieces(Y_PIECES_LATE, ns_prev, lambda s: y_copy(w - 1, 1 - p, s).start())

    for ns_static in range(1, len(FFN_PIECES) + 1):
        nr = FFN_PIECES[ns_static - 1][0] + FFN_PIECES[ns_static - 1][1]

        @pl.when(ns == ns_static)
        def _(nr=nr):
            @pl.when(c == 0)
            def _():
                ok = lax.broadcasted_iota(I32, (nr, 1), 0) < rows
                xb[0:nr, :] = jnp.where(ok, xg[0:nr, :], 0.0).astype(BF16)
                acc[p, 0:nr, :] = jnp.zeros((nr, D_MODEL), F32)

            wg = wg_ref[0].astype(BF16)
            wu = wu_ref[0].astype(BF16)
            wd = wd_ref[0].astype(BF16)
            x = xb[0:nr, :]
            a = jnp.dot(x, wg, preferred_element_type=F32)
            u = jnp.dot(x, wu, preferred_element_type=F32)
            hm = (a / (1.0 + jnp.exp(-a)) * u).astype(BF16)
            acc[p, 0:nr, :] += jnp.dot(hm, wd, preferred_element_type=F32)

    for step, pieces in X_PIECES_AT_STEP.items():
        @pl.when(c == step)
        def _(pieces=pieces):
            for_pieces(pieces, ns_next, lambda s: x_copy(w + 1, s).start())

    @pl.when(c == n_chunks - 1)
    def _():
        for_pieces(Y_PIECES_EARLY, ns, lambda s: y_copy(w, p, s).start())


def _ffn(item_expert, item_rows, xbuf, w_gate, w_up, w_down):
    n_chunks = D_EXPERT // FFN_CHUNK
    grid_items = item_rows.shape[0] - 1

    def chunk_of(w, c, ir):
        return jnp.where(ir[w] > 0, c, n_chunks - 1)

    grid_spec = pltpu.PrefetchScalarGridSpec(
        num_scalar_prefetch=2,
        grid=(grid_items, n_chunks),
        in_specs=[
            pl.BlockSpec(memory_space=pl.ANY),
            pl.BlockSpec((1, D_MODEL, FFN_CHUNK), lambda w, c, ie, ir: (ie[w], 0, chunk_of(w, c, ir))),
            pl.BlockSpec((1, D_MODEL, FFN_CHUNK), lambda w, c, ie, ir: (ie[w], 0, chunk_of(w, c, ir))),
            pl.BlockSpec((1, FFN_CHUNK, D_MODEL), lambda w, c, ie, ir: (ie[w], chunk_of(w, c, ir), 0)),
        ],
        out_specs=pl.BlockSpec(memory_space=pl.ANY),
        scratch_shapes=[
            pltpu.VMEM((ITEM_ROWS, D_MODEL), F32),
            pltpu.VMEM((ITEM_ROWS, D_MODEL), BF16),
            pltpu.VMEM((2, ITEM_ROWS, D_MODEL), F32),
            pltpu.SemaphoreType.DMA,
            pltpu.SemaphoreType.DMA((2,)),
        ],
    )
    return pl.pallas_call(
        _ffn_kernel,
        grid_spec=grid_spec,
        out_shape=jax.ShapeDtypeStruct(xbuf.shape, F32),
        compiler_params=_params(("arbitrary", "arbitrary"), 58),
        name="ffn",
    )(item_expert, item_rows, xbuf, w_gate, w_up, w_down)


def _combine_kernel(dest_ref, x2_ref, route_ref, ybuf_ref, o_ref, rows_scr, sem):
    i = pl.program_id(0)
    n = pl.num_programs(0)
    tm = COMBINE_TM

    def gather_tile(tile, slot):
        base = tile * tm

        def body(t, carry):
            a = 2 * (base + t)
            pltpu.make_async_copy(ybuf_ref.at[pl.ds(dest_ref[a], 1)], rows_scr.at[slot, 0, pl.ds(t, 1)],
                                  sem.at[slot]).start()
            pltpu.make_async_copy(ybuf_ref.at[pl.ds(dest_ref[a + 1], 1)], rows_scr.at[slot, 1, pl.ds(t, 1)],
                                  sem.at[slot]).start()
            return carry

        lax.fori_loop(0, tm, body, 0, unroll=DMA_UNROLL)

    slot = i % 2

    @pl.when(i == 0)
    def _():
        gather_tile(0, 0)

    @pl.when(i + 1 < n)
    def _():
        gather_tile(i + 1, 1 - slot)

    for k in range(2):
        pltpu.make_async_copy(ybuf_ref.at[pl.ds(0, tm)], rows_scr.at[slot, k], sem.at[slot]).wait()

    o_ref[...] = x2_ref[...] + route_ref[:, 2:3] * rows_scr[slot, 0] + route_ref[:, 3:4] * rows_scr[slot, 1]


def _combine(dest, x2, route, ybuf):
    T = x2.shape[0]
    tm = COMBINE_TM
    grid_spec = pltpu.PrefetchScalarGridSpec(
        num_scalar_prefetch=1,
        grid=(T // tm,),
        in_specs=[
            pl.BlockSpec((tm, D_MODEL), lambda i, d: (i, 0)),
            pl.BlockSpec((tm, 128), lambda i, d: (i, 0)),
            pl.BlockSpec(memory_space=pl.ANY),
        ],
        out_specs=pl.BlockSpec((tm, D_MODEL), lambda i, d: (i, 0)),
        scratch_shapes=[pltpu.VMEM((2, 2, tm, D_MODEL), F32), pltpu.SemaphoreType.DMA((2,))],
    )
    return pl.pallas_call(
        _combine_kernel,
        grid_spec=grid_spec,
        out_shape=jax.ShapeDtypeStruct((T, D_MODEL), F32),
        compiler_params=_params(("arbitrary",), 40),
        name="combine",
    )(dest, x2, route, ybuf)


def _moe_plan(route, counts, max_items):
    experts = route[:, :2].astype(I32)
    rank = route[:, 4:6].astype(I32)
    counts = counts[0, :N_EXPERTS].astype(I32)
    n_it = (counts + ITEM_ROWS - 1) // ITEM_ROWS
    it_end = jnp.cumsum(n_it)
    it_start = it_end - n_it
    dest = ((it_start * ITEM_ROWS)[experts] + rank).reshape(-1)
    n_items = it_end[-1]
    w = jnp.arange(max_items + 3, dtype=I32)
    ie = jnp.minimum(jnp.sum((it_end[None, :] <= w[:, None]).astype(I32), axis=1), N_EXPERTS - 1)
    used = w < n_items
    rows = jnp.where(used, jnp.minimum(ITEM_ROWS, counts[ie] - (w - it_start[ie]) * ITEM_ROWS), 0)
    ie = jnp.where(used, ie, ie[jnp.maximum(n_items - 1, 0)])
    return dest.astype(I32), ie.astype(I32), rows.astype(I32)


def _layer(x, mem, norm1_g, w_in, q_norm_g, k_norm_g, conv_w, mem_norm_g, w_mem_kv, mem_q_norm_g, mem_k_norm_g,
           out_norm_g, w_out, norm2_g, w_rg, b_rg, w_re, b_re, w_gate, w_up, w_down):
    B, S, D = x.shape
    T = B * S
    x2d = x.reshape(T, D)
    row = lambda v: v.reshape(1, -1).astype(F32)

    kvm = _memkv(mem.reshape(B * N_MEM, D), row(mem_norm_g), w_mem_kv, row(mem_k_norm_g))
    qkv, rest = _inproj(x2d, row(norm1_g), w_in.astype(BF16), row(q_norm_g), row(k_norm_g), row(mem_q_norm_g))

    slopes = 2.0 ** (-8.0 * jnp.arange(1, N_ATTN_HEADS + 1, dtype=F32) / N_ATTN_HEADS)
    attn = _attention(qkv, slopes, B, S)

    w_router = jnp.zeros((D, 128), F32).at[:, :N_GROUPS].set(w_rg).at[:, N_GROUPS:N_GROUPS + N_EXPERTS].set(w_re)
    b_router = jnp.zeros((1, 128), F32).at[0, :N_GROUPS].set(b_rg).at[0, N_GROUPS:N_GROUPS + N_EXPERTS].set(b_re)
    x2, h2, route, counts = _mixer(attn, rest, kvm, x2d, w_out.astype(BF16), row(out_norm_g), row(norm2_g),
                                   conv_w.astype(F32), w_router.astype(BF16), b_router, S)

    max_items = N_EXPERTS + (2 * T) // ITEM_ROWS
    dest, item_expert, item_rows = _moe_plan(route, counts, max_items)
    xbuf = _dispatch(dest, h2, max_items * ITEM_ROWS)
    ybuf = _ffn(item_expert, item_rows, xbuf, w_gate, w_up, w_down)
    out = _combine(dest, x2, route, ybuf)
    return out.reshape(B, S, D)


def kernel(x, mem, norm1_g, w_in, q_norm_g, k_norm_g, conv_w, mem_norm_g, w_mem_kv, mem_q_norm_g, mem_k_norm_g,
           out_norm_g, w_out, norm2_g, w_router_group, b_router_group, w_router_expert, b_router_expert,
           w_gate, w_up, w_down):
    for l in range(norm1_g.shape[0]):
        x = _layer(x, mem, norm1_g[l], w_in[l], q_norm_g[l], k_norm_g[l], conv_w[l], mem_norm_g[l], w_mem_kv[l],
                   mem_q_norm_g[l], mem_k_norm_g[l], out_norm_g[l], w_out[l], norm2_g[l],
                   w_router_group[l], b_router_group[l], w_router_expert[l], b_router_expert[l],
                   w_gate[l], w_up[l], w_down[l])
    return x
```

```python
import functools

import jax
import jax.numpy as jnp
from jax import lax
from jax.experimental import pallas as pl
from jax.experimental.pallas import tpu as pltpu

F32 = jnp.float32
BF16 = jnp.bfloat16
I32 = jnp.int32

D_MODEL = 2048
HEAD_DIM = 128
N_ATTN_HEADS = 8
ATTN_W = N_ATTN_HEADS * HEAD_DIM
N_MEM_HEADS = 4
MEM_W = N_MEM_HEADS * HEAD_DIM
CONV_W = D_MODEL - ATTN_W - MEM_W
CONV_GROUPS = CONV_W // HEAD_DIM
N_MEM = 256
DILATIONS = (1, 4, 16)
SPAN = 128
PROJ_W = 3 * ATTN_W + 3 * CONV_W + MEM_W
N_GROUPS = 8
EXPERTS_PER_GROUP = 8
N_EXPERTS = N_GROUPS * EXPERTS_PER_GROUP
D_EXPERT = D_MODEL // 2
EPS = 1e-6
NEG_INF = -1e30
QK_SCALE = HEAD_DIM ** -0.5
LOG2E = 1.4426950408889634

MIB = 1024 * 1024
INPROJ_TM = 512
INPROJ_TN = 1024
ATTN_CHUNK = 2048
MIXER_TM = 512
DEST_TM = 2048
DISPATCH_TM = 512
COMBINE_TM = 256
ITEM_ROWS = 1152
FFN_SUB = 256
FFN_PIECES = tuple((lo, min(FFN_SUB, ITEM_ROWS - lo)) for lo in range(0, ITEM_ROWS, FFN_SUB))
X_PIECES_AT_STEP = {1: (0, 1), 2: (2, 3), 3: (4,)}
Y_PIECES_EARLY = (0, 1)
Y_PIECES_LATE = (2, 3, 4)
FFN_CHUNK = 256
HALO = 16
DMA_UNROLL = 8
ATTN_UNROLL = 16


def _params(semantics, vmem_mib):
    return pltpu.CompilerParams(dimension_semantics=semantics, vmem_limit_bytes=vmem_mib * MIB)


def _rms(v, width):
    return lax.rsqrt(jnp.sum(v * v, axis=-1, keepdims=True) * (1.0 / width) + EPS)


def _memkv_kernel(mem_ref, g_ref, w_ref, kg_ref, o_ref):
    j = pl.program_id(0)
    m = mem_ref[...]
    h = (m * _rms(m, D_MODEL) * g_ref[...]).astype(BF16)
    kv = jnp.dot(h, w_ref[...].astype(BF16), preferred_element_type=F32)
    is_key = j == 0
    for hh in range(N_MEM_HEADS):
        t = kv[:, hh * HEAD_DIM:(hh + 1) * HEAD_DIM]
        tn = t * _rms(t, HEAD_DIM) * kg_ref[...]
        o_ref[hh] = jnp.where(is_key, tn, t).astype(BF16)


def _memkv(mem2d, mem_norm_g, w_mem_kv, mem_k_norm_g):
    rows = mem2d.shape[0]
    return pl.pallas_call(
        _memkv_kernel,
        grid=(2,),
        in_specs=[
            pl.BlockSpec((rows, D_MODEL), lambda j: (0, 0)),
            pl.BlockSpec((1, D_MODEL), lambda j: (0, 0)),
            pl.BlockSpec((D_MODEL, MEM_W), lambda j: (0, j)),
            pl.BlockSpec((1, HEAD_DIM), lambda j: (0, 0)),
        ],
        out_specs=pl.BlockSpec((N_MEM_HEADS, rows, HEAD_DIM), lambda j: (j, 0, 0)),
        out_shape=jax.ShapeDtypeStruct((2 * N_MEM_HEADS, rows, HEAD_DIM), BF16),
        compiler_params=_params(("arbitrary",), 40),
        name="memkv",
    )(mem2d, mem_norm_g, w_mem_kv, mem_k_norm_g)


def _inproj_kernel(x_ref, g1_ref, w_ref, qg_ref, kg_ref, mqg_ref, qkv_ref, rest_ref):
    x = x_ref[...]
    h = (x * _rms(x, D_MODEL) * g1_ref[...]).astype(BF16)

    def normed(t, g):
        return t * _rms(t, HEAD_DIM) * g

    gq = qg_ref[...] * (QK_SCALE * LOG2E)
    gk = kg_ref[...]
    gmq = mqg_ref[...] * QK_SCALE
    groups_per_tile = INPROJ_TN // HEAD_DIM
    for j in range(PROJ_W // INPROJ_TN):
        y = jnp.dot(h, w_ref[:, j * INPROJ_TN:(j + 1) * INPROJ_TN], preferred_element_type=F32)
        for hh in range(groups_per_tile):
            t = y[:, hh * HEAD_DIM:(hh + 1) * HEAD_DIM]
            col = j * groups_per_tile + hh
            if col < 8:
                qkv_ref[col] = normed(t, gq)
            elif col < 16:
                qkv_ref[col] = normed(t, gk)
            elif col < 24:
                qkv_ref[col] = t
            elif col < 36:
                rest_ref[col - 24] = t.astype(BF16)
            else:
                rest_ref[col - 24] = normed(t, gmq).astype(BF16)


def _inproj(x2d, norm1_g, w_in_bf16, q_norm_g, k_norm_g, mem_q_norm_g):
    T = x2d.shape[0]
    tm = INPROJ_TM
    small = lambda n: pl.BlockSpec((1, n), lambda i: (0, 0))
    return pl.pallas_call(
        _inproj_kernel,
        grid=(T // tm,),
        in_specs=[
            pl.BlockSpec((tm, D_MODEL), lambda i: (i, 0)),
            small(D_MODEL),
            pl.BlockSpec((D_MODEL, PROJ_W), lambda i: (0, 0), pipeline_mode=pl.Buffered(1)),
            small(HEAD_DIM), small(HEAD_DIM), small(HEAD_DIM),
        ],
        out_specs=[
            pl.BlockSpec((24, tm, HEAD_DIM), lambda i: (0, i, 0)),
            pl.BlockSpec((16, tm, HEAD_DIM), lambda i: (0, i, 0)),
        ],
        out_shape=[
            jax.ShapeDtypeStruct((24, T, HEAD_DIM), F32),
            jax.ShapeDtypeStruct((16, T, HEAD_DIM), BF16),
        ],
        compiler_params=_params(("parallel",), 52),
        name="inproj",
    )(x2d, norm1_g, w_in_bf16, q_norm_g, k_norm_g, mem_q_norm_g)


def _attn_kernel(slopes_ref, q_ref, k_ref, v_ref, o_ref,
                 kx, vx, q4, k4, v4, o1, l1, o4, l4, o16, l16, out_scr, bias_scr):
    h = pl.program_id(1)
    c = pl.program_id(2)
    C = ATTN_CHUNK

    @pl.when(c == 0)
    def _():
        kx[0:C, :] = jnp.zeros((C, HEAD_DIM), F32)
        vx[0:C, :] = jnp.zeros((C, HEAD_DIM), F32)

    @pl.when(c > 0)
    def _():
        kx[0:C, :] = kx[C:2 * C, :]
        vx[0:C, :] = vx[C:2 * C, :]

    kx[C:2 * C, :] = k_ref[0]
    vx[C:2 * C, :] = v_ref[0]

    for b in range(4):
        q4[b] = q_ref[0, pl.ds(b, C // 4, stride=4), :]
        k4[b] = kx[pl.ds(b, 2 * C // 4, stride=4), :]
        v4[b] = vx[pl.ds(b, 2 * C // 4, stride=4), :]

    slope = slopes_ref[h]
    ii = lax.broadcasted_iota(I32, (SPAN, 2 * SPAN), 0)
    jj = lax.broadcasted_iota(I32, (SPAN, 2 * SPAN), 1)
    dist = ii - jj + SPAN
    band = (dist >= 0) & (dist <= SPAN)
    distf = dist.astype(F32)

    for bi, d in enumerate(DILATIONS):
        neg = (-slope * float(d) * LOG2E) * distf
        bias_scr[2 * bi] = jnp.where(band, neg, NEG_INF)
        bias_scr[2 * bi + 1] = jnp.where(band & (jj >= SPAN), neg, NEG_INF)

    ones = jnp.ones((2 * SPAN, HEAD_DIM), BF16)

    def attend(q, k, v, bias):
        s = lax.dot_general(q.astype(BF16), k.astype(BF16), (((1,), (1,)), ((), ())),
                            preferred_element_type=F32) + bias
        m = jnp.max(s, axis=-1, keepdims=True)
        p = jnp.exp2(s - m).astype(BF16)
        ol = jnp.dot(p, jnp.concatenate([v.astype(BF16), ones], axis=1), preferred_element_type=F32)
        l = ol[:, HEAD_DIM:]
        return ol[:, :HEAD_DIM] * (1.0 / l), m + jnp.log2(l)

    def unit1(u, carry):
        sq = pl.multiple_of(u * SPAN, SPAN)
        sk = pl.multiple_of(C + u * SPAN - SPAN, SPAN)
        first = jnp.logical_and(c == 0, u == 0).astype(I32)
        o, lse = attend(q_ref[0, pl.ds(sq, SPAN), :], kx[pl.ds(sk, 2 * SPAN), :], vx[pl.ds(sk, 2 * SPAN), :],
                        bias_scr[first])
        o1[pl.ds(sq, SPAN), :] = o
        l1[pl.ds(sq, SPAN), :] = lse
        return carry

    def unit4(u, carry):
        b = u % 4
        qb = u // 4
        sq = pl.multiple_of(qb * SPAN, SPAN)
        sk = pl.multiple_of(C // 4 + qb * SPAN - SPAN, SPAN)
        first = jnp.logical_and(c == 0, qb == 0).astype(I32)
        o, lse = attend(q4[b, pl.ds(sq, SPAN), :], k4[b, pl.ds(sk, 2 * SPAN), :], v4[b, pl.ds(sk, 2 * SPAN), :],
                        bias_scr[2 + first])
        o4[b, pl.ds(sq, SPAN), :] = o
        l4[b, pl.ds(sq, SPAN), :] = lse
        return carry

    def unit16(u, carry):
        b = u % 4
        a = u // 4
        first = (c == 0).astype(I32)
        qi = pl.ds(a, SPAN, stride=4)
        ki = pl.ds(a, 2 * SPAN, stride=4)
        o, lse = attend(q4[b, qi, :], k4[b, ki, :], v4[b, ki, :], bias_scr[4 + first])
        o16[b, qi, :] = o
        l16[b, qi, :] = lse
        return carry

    for unit in (unit1, unit4, unit16):
        lax.fori_loop(0, C // SPAN, unit, 0, unroll=ATTN_UNROLL)

    for b in range(4):
        rows_b = pl.ds(b, C // 4, stride=4)
        la, lb, lc = l1[rows_b, :], l4[b], l16[b]
        mx = jnp.maximum(jnp.maximum(la, lb), lc)
        ea, eb, ec = jnp.exp2(la - mx), jnp.exp2(lb - mx), jnp.exp2(lc - mx)
        out_scr[rows_b, :] = (ea * o1[rows_b, :] + eb * o4[b] + ec * o16[b]) * (1.0 / (ea + eb + ec))
    o_ref[0] = out_scr[...].astype(BF16)


def _attention(qkv, slopes, batch, seq):
    T = qkv.shape[1]
    C = ATTN_CHUNK
    nc = seq // C
    blk = lambda off: pl.BlockSpec((1, C, HEAD_DIM), lambda b, h, c, s: (h + off, b * nc + c, 0))
    grid_spec = pltpu.PrefetchScalarGridSpec(
        num_scalar_prefetch=1,
        grid=(batch, N_ATTN_HEADS, nc),
        in_specs=[blk(0), blk(8), blk(16)],
        out_specs=pl.BlockSpec((1, C, HEAD_DIM), lambda b, h, c, s: (h, b * nc + c, 0)),
        scratch_shapes=[
            pltpu.VMEM((2 * C, HEAD_DIM), F32),
            pltpu.VMEM((2 * C, HEAD_DIM), F32),
            pltpu.VMEM((4, C // 4, HEAD_DIM), F32),
            pltpu.VMEM((4, 2 * C // 4, HEAD_DIM), F32),
            pltpu.VMEM((4, 2 * C // 4, HEAD_DIM), F32),
            pltpu.VMEM((C, HEAD_DIM), F32),
            pltpu.VMEM((C, HEAD_DIM), F32),
            pltpu.VMEM((4, C // 4, HEAD_DIM), F32),
            pltpu.VMEM((4, C // 4, HEAD_DIM), F32),
            pltpu.VMEM((4, C // 4, HEAD_DIM), F32),
            pltpu.VMEM((4, C // 4, HEAD_DIM), F32),
            pltpu.VMEM((C, HEAD_DIM), F32),
            pltpu.VMEM((2 * len(DILATIONS), SPAN, 2 * SPAN), F32),
        ],
    )
    return pl.pallas_call(
        _attn_kernel,
        grid_spec=grid_spec,
        out_shape=jax.ShapeDtypeStruct((N_ATTN_HEADS, T, HEAD_DIM), BF16),
        compiler_params=_params(("parallel", "parallel", "arbitrary"), 40),
        name="attn",
    )(slopes, qkv, qkv, qkv)


def _mixer_kernel(attn_ref, rest_ref, halo_ref, kvm_ref, x_ref, wout_ref, og_ref, g2_ref, cw_ref, wr_ref, br_ref,
                  x2_ref, h2_ref, route_ref, counts_ref, y_scr, z_scr, cnt_scr, *, tiles_per_seq):
    i = pl.program_id(0)
    tm = MIXER_TM
    og = og_ref[...]

    ss = jnp.zeros((tm, 1), F32)
    for hh in range(N_ATTN_HEADS):
        a = attn_ref[hh].astype(F32)
        ss = ss + jnp.sum(a * a, axis=-1, keepdims=True)
    r = lax.rsqrt(ss * (1.0 / ATTN_W) + EPS)
    for hh in range(N_ATTN_HEADS):
        lo = hh * HEAD_DIM
        y_scr[:, lo:lo + HEAD_DIM] = (attn_ref[hh].astype(F32) * r * og[:, lo:lo + HEAD_DIM]).astype(BF16)

    seq_start = (i % tiles_per_seq) == 0
    convs = []
    ss = jnp.zeros((tm, 1), F32)
    for g in range(CONV_GROUPS):
        bg = rest_ref[g].astype(F32)
        z = rest_ref[CONV_GROUPS + g].astype(F32) * rest_ref[2 * CONV_GROUPS + g].astype(F32)
        zh = halo_ref[CONV_GROUPS + g].astype(F32) * halo_ref[2 * CONV_GROUPS + g].astype(F32)
        z_scr[g, 0:HALO, :] = jnp.where(seq_start, 0.0, zh)
        z_scr[g, HALO:HALO + tm, :] = z
        z1 = z_scr[g, HALO - 1:HALO - 1 + tm, :]
        z2 = z_scr[g, HALO - 2:HALO - 2 + tm, :]
        lo = g * HEAD_DIM
        w0 = cw_ref[0:1, lo:lo + HEAD_DIM]
        w1 = cw_ref[1:2, lo:lo + HEAD_DIM]
        w2 = cw_ref[2:3, lo:lo + HEAD_DIM]
        cv = bg * (w2 * z + w1 * z1 + w0 * z2)
        convs.append(cv)
        ss = ss + jnp.sum(cv * cv, axis=-1, keepdims=True)
    r = lax.rsqrt(ss * (1.0 / CONV_W) + EPS)
    for g in range(CONV_GROUPS):
        lo = ATTN_W + g * HEAD_DIM
        y_scr[:, lo:lo + HEAD_DIM] = (convs[g] * r * og[:, lo:lo + HEAD_DIM]).astype(BF16)

    mems = []
    ss = jnp.zeros((tm, 1), F32)
    for hh in range(N_MEM_HEADS):
        mq = rest_ref[3 * CONV_GROUPS + hh]
        s = lax.dot_general(mq, kvm_ref[hh], (((1,), (1,)), ((), ())), preferred_element_type=F32)
        m = jnp.max(s, axis=-1, keepdims=True)
        p = jnp.exp(s - m)
        l = jnp.sum(p, axis=-1, keepdims=True)
        o = jnp.dot(p.astype(BF16), kvm_ref[N_MEM_HEADS + hh], preferred_element_type=F32) * (1.0 / l)
        mems.append(o)
        ss = ss + jnp.sum(o * o, axis=-1, keepdims=True)
    r = lax.rsqrt(ss * (1.0 / MEM_W) + EPS)
    for hh in range(N_MEM_HEADS):
        lo = ATTN_W + CONV_W + hh * HEAD_DIM
        y_scr[:, lo:lo + HEAD_DIM] = (mems[hh] * r * og[:, lo:lo + HEAD_DIM]).astype(BF16)

    x2 = x_ref[...] + jnp.dot(y_scr[...], wout_ref[...], preferred_element_type=F32)
    x2_ref[...] = x2
    h2 = x2 * _rms(x2, D_MODEL) * g2_ref[...]
    h2_ref[...] = h2

    lg = jnp.dot(h2.astype(BF16), wr_ref[...], preferred_element_type=F32) + br_ref[...]
    lane = lax.broadcasted_iota(I32, (tm, 128), 1).astype(F32)
    gl = jnp.where(lane < N_GROUPS, lg, NEG_INF)
    gmax = jnp.max(gl, axis=-1, keepdims=True)
    g_w = 1.0 / jnp.sum(jnp.exp(gl - gmax), axis=-1, keepdims=True)
    g_idx = jnp.min(jnp.where(gl == gmax, lane, 1e9), axis=-1, keepdims=True)
    e_lo = N_GROUPS + EXPERTS_PER_GROUP * g_idx
    el = jnp.where((lane >= e_lo) & (lane < e_lo + EXPERTS_PER_GROUP), lg, NEG_INF)
    e1 = jnp.max(el, axis=-1, keepdims=True)
    i1 = jnp.min(jnp.where(el == e1, lane, 1e9), axis=-1, keepdims=True)
    el2 = jnp.where(lane == i1, NEG_INF, el)
    e2 = jnp.max(el2, axis=-1, keepdims=True)
    i2 = jnp.min(jnp.where(el2 == e2, lane, 1e9), axis=-1, keepdims=True)
    t = jnp.exp(e2 - e1)
    p1 = 1.0 / (1.0 + t)
    p2 = t * p1

    @pl.when(i == 0)
    def _():
        cnt_scr[...] = jnp.zeros((1, 128), F32)

    x1 = i1 - N_GROUPS
    x2 = i2 - N_GROUPS
    hot1 = lane == x1
    hot2 = lane == x2
    hot = jnp.where(hot1 | hot2, 1.0, 0.0)
    row_id = lax.broadcasted_iota(I32, (tm, tm), 0)
    col_id = lax.broadcasted_iota(I32, (tm, tm), 1)
    earlier = jnp.where(col_id < row_id, 1.0, 0.0).astype(BF16)
    before = jnp.dot(earlier, hot.astype(BF16), preferred_element_type=F32) + cnt_scr[...]
    rank1 = jnp.sum(jnp.where(hot1, before, 0.0), axis=-1, keepdims=True)
    rank2 = jnp.sum(jnp.where(hot2, before, 0.0), axis=-1, keepdims=True)
    cnt_scr[...] = cnt_scr[...] + jnp.sum(hot, axis=0, keepdims=True)
    counts_ref[...] = cnt_scr[...]

    route = jnp.where(lane == 0, x1,
                      jnp.where(lane == 1, x2,
                                jnp.where(lane == 2, g_w * p1,
                                          jnp.where(lane == 3, g_w * p2,
                                                    jnp.where(lane == 4, rank1,
                                                              jnp.where(lane == 5, rank2, 0.0))))))
    route_ref[...] = route


def _mixer(attn, rest, kvm, x2d, w_out_bf16, out_norm_g, norm2_g, conv_w, w_router, b_router, seq):
    T = x2d.shape[0]
    tm = MIXER_TM
    tiles_per_seq = seq // tm
    full = lambda shape: pl.BlockSpec(shape, lambda i: tuple(0 for _ in shape))
    return pl.pallas_call(
        functools.partial(_mixer_kernel, tiles_per_seq=tiles_per_seq),
        grid=(T // tm,),
        in_specs=[
            pl.BlockSpec((N_ATTN_HEADS, tm, HEAD_DIM), lambda i: (0, i, 0)),
            pl.BlockSpec((16, tm, HEAD_DIM), lambda i: (0, i, 0)),
            pl.BlockSpec((16, HALO, HEAD_DIM), lambda i: (0, jnp.maximum(i * (tm // HALO) - 1, 0), 0)),
            pl.BlockSpec((2 * N_MEM_HEADS, N_MEM, HEAD_DIM), lambda i: (0, i // tiles_per_seq, 0)),
            pl.BlockSpec((tm, D_MODEL), lambda i: (i, 0)),
            pl.BlockSpec((D_MODEL, D_MODEL), lambda i: (0, 0), pipeline_mode=pl.Buffered(1)),
            full((1, D_MODEL)),
            full((1, D_MODEL)),
            full((3, CONV_W)),
            full((D_MODEL, 128)),
            full((1, 128)),
        ],
        out_specs=[
            pl.BlockSpec((tm, D_MODEL), lambda i: (i, 0)),
            pl.BlockSpec((tm, D_MODEL), lambda i: (i, 0)),
            pl.BlockSpec((tm, 128), lambda i: (i, 0)),
            pl.BlockSpec((1, 128), lambda i: (0, 0)),
        ],
        out_shape=[
            jax.ShapeDtypeStruct((T, D_MODEL), F32),
            jax.ShapeDtypeStruct((T, D_MODEL), F32),
            jax.ShapeDtypeStruct((T, 128), F32),
            jax.ShapeDtypeStruct((1, 128), F32),
        ],
        scratch_shapes=[
            pltpu.VMEM((tm, D_MODEL), BF16),
            pltpu.VMEM((CONV_GROUPS, HALO + tm, HEAD_DIM), F32),
            pltpu.VMEM((1, 128), F32),
        ],
        compiler_params=_params(("arbitrary",), 56),
        name="mixer",
    )(attn, rest, rest, kvm, x2d, w_out_bf16, out_norm_g, norm2_g, conv_w, w_router, b_router)


def _dispatch_kernel(dest_ref, h_ref, xbuf_ref, sem):
    i = pl.program_id(0)
    tm = DISPATCH_TM
    base = i * tm

    def body(t, carry):
        a = 2 * (base + t)
        src = h_ref.at[pl.ds(t, 1)]
        pltpu.make_async_copy(src, xbuf_ref.at[pl.ds(dest_ref[a], 1)], sem).start()
        pltpu.make_async_copy(src, xbuf_ref.at[pl.ds(dest_ref[a + 1], 1)], sem).start()
        return carry

    lax.fori_loop(0, tm, body, 0, unroll=DMA_UNROLL)
    pltpu.make_async_copy(xbuf_ref.at[pl.ds(0, 2 * tm)], xbuf_ref.at[pl.ds(0, 2 * tm)], sem).wait()


def _dispatch(dest, h2, n_rows):
    T = h2.shape[0]
    tm = DISPATCH_TM
    grid_spec = pltpu.PrefetchScalarGridSpec(
        num_scalar_prefetch=1,
        grid=(T // tm,),
        in_specs=[pl.BlockSpec((tm, D_MODEL), lambda i, d: (i, 0))],
        out_specs=pl.BlockSpec(memory_space=pl.ANY),
        scratch_shapes=[pltpu.SemaphoreType.DMA],
    )
    return pl.pallas_call(
        _dispatch_kernel,
        grid_spec=grid_spec,
        out_shape=jax.ShapeDtypeStruct((n_rows, D_MODEL), F32),
        compiler_params=_params(("arbitrary",), 32),
        name="dispatch",
    )(dest, h2)


def _ffn_kernel(ie_ref, rows_ref, x_ref, wg_ref, wu_ref, wd_ref, y_ref, xg, xb, acc, xsem, ysem):
    w = pl.program_id(0)
    c = pl.program_id(1)
    n_chunks = D_EXPERT // FFN_CHUNK

    def subs(v):
        return (v + FFN_SUB - 1) // FFN_SUB

    rows = rows_ref[w]
    ns = subs(rows)
    ns_next = subs(rows_ref[w + 1])
    ns_prev = jnp.where(w > 0, subs(rows_ref[jnp.maximum(w - 1, 0)]), 0)
    ns_prev2 = jnp.where(w > 1, subs(rows_ref[jnp.maximum(w - 2, 0)]), 0)
    p = w % 2

    def x_copy(item, s):
        lo, n = FFN_PIECES[s]
        return pltpu.make_async_copy(x_ref.at[pl.ds(item * ITEM_ROWS + lo, n)], xg.at[pl.ds(lo, n)], xsem)

    def y_copy(item, slot, s):
        lo, n = FFN_PIECES[s]
        return pltpu.make_async_copy(acc.at[slot, pl.ds(lo, n)], y_ref.at[pl.ds(item * ITEM_ROWS + lo, n)],
                                     ysem.at[slot])

    def for_pieces(pieces, count, fn):
        for s in pieces:
            @pl.when(s < count)
            def _(s=s):
                fn(s)

    all_pieces = range(len(FFN_PIECES))

    @pl.when(c == 0)
    def _():
        @pl.when(w == 0)
        def _():
            for_pieces(all_pieces, ns, lambda s: x_copy(0, s).start())

        for_pieces(all_pieces, ns, lambda s: x_copy(w, s).wait())
        for_pieces(all_pieces, ns_prev2, lambda s: y_copy(w - 2, p, s).wait())
        for_pieces(Y_PIECES_LATE, ns_prev, lambda s: y_copy(w - 1, 1 - p, s).start())

    for ns_static in range(1, len(FFN_PIECES) + 1):
        nr = FFN_PIECES[ns_static - 1][0] + FFN_PIECES[ns_static - 1][1]

        @pl.when(ns == ns_static)
        def _(nr=nr):
            @pl.when(c == 0)
            def _():
                ok = lax.broadcasted_iota(I32, (nr, 1), 0) < rows
                xb[0:nr, :] = jnp.where(ok, xg[0:nr, :], 0.0).astype(BF16)
                acc[p, 0:nr, :] = jnp.zeros((nr, D_MODEL), F32)

            wg = wg_ref[0].astype(BF16)
            wu = wu_ref[0].astype(BF16)
            wd = wd_ref[0].astype(BF16)
            x = xb[0:nr, :]
            a = jnp.dot(x, wg, preferred_element_type=F32)
            u = jnp.dot(x, wu, preferred_element_type=F32)
            hm = (a / (1.0 + jnp.exp(-a)) * u).astype(BF16)
            acc[p, 0:nr, :] += jnp.dot(hm, wd, preferred_element_type=F32)

    for step, pieces in X_PIECES_AT_STEP.items():
        @pl.when(c == step)
        def _(pieces=pieces):
            for_pieces(pieces, ns_next, lambda s: x_copy(w + 1, s).start())

    @pl.when(c == n_chunks - 1)
    def _():
        for_pieces(Y_PIECES_EARLY, ns, lambda s: y_copy(w, p, s).start())


def _ffn(item_expert, item_rows, xbuf, w_gate, w_up, w_down):
    n_chunks = D_EXPERT // FFN_CHUNK
    grid_items = item_rows.shape[0] - 1

    def chunk_of(w, c, ir):
        return jnp.where(ir[w] > 0, c, n_chunks - 1)

    grid_spec = pltpu.PrefetchScalarGridSpec(
        num_scalar_prefetch=2,
        grid=(grid_items, n_chunks),
        in_specs=[
            pl.BlockSpec(memory_space=pl.ANY),
            pl.BlockSpec((1, D_MODEL, FFN_CHUNK), lambda w, c, ie, ir: (ie[w], 0, chunk_of(w, c, ir))),
            pl.BlockSpec((1, D_MODEL, FFN_CHUNK), lambda w, c, ie, ir: (ie[w], 0, chunk_of(w, c, ir))),
            pl.BlockSpec((1, FFN_CHUNK, D_MODEL), lambda w, c, ie, ir: (ie[w], chunk_of(w, c, ir), 0)),
        ],
        out_specs=pl.BlockSpec(memory_space=pl.ANY),
        scratch_shapes=[
            pltpu.VMEM((ITEM_ROWS, D_MODEL), F32),
            pltpu.VMEM((ITEM_ROWS, D_MODEL), BF16),
            pltpu.VMEM((2, ITEM_ROWS, D_MODEL), F32),
            pltpu.SemaphoreType.DMA,
            pltpu.SemaphoreType.DMA((2,)),
        ],
    )
    return pl.pallas_call(
        _ffn_kernel,
        grid_spec=grid_spec,
        out_shape=jax.ShapeDtypeStruct(xbuf.shape, F32),
        compiler_params=_params(("arbitrary", "arbitrary"), 58),
        name="ffn",
    )(item_expert, item_rows, xbuf, w_gate, w_up, w_down)


def _combine_kernel(dest_ref, x2_ref, route_ref, ybuf_ref, o_ref, rows_scr, sem):
    i = pl.program_id(0)
    n = pl.num_programs(0)
    tm = COMBINE_TM

    def gather_tile(tile, slot):
        base = tile * tm

        def body(t, carry):
            a = 2 * (base + t)
            pltpu.make_async_copy(ybuf_ref.at[pl.ds(dest_ref[a], 1)], rows_scr.at[slot, 0, pl.ds(t, 1)],
                                  sem.at[slot]).start()
            pltpu.make_async_copy(ybuf_ref.at[pl.ds(dest_ref[a + 1], 1)], rows_scr.at[slot, 1, pl.ds(t, 1)],
                                  sem.at[slot]).start()
            return carry

        lax.fori_loop(0, tm, body, 0, unroll=DMA_UNROLL)

    slot = i % 2

    @pl.when(i == 0)
    def _():
        gather_tile(0, 0)

    @pl.when(i + 1 < n)
    def _():
        gather_tile(i + 1, 1 - slot)

    for k in range(2):
        pltpu.make_async_copy(ybuf_ref.at[pl.ds(0, tm)], rows_scr.at[slot, k], sem.at[slot]).wait()

    o_ref[...] = x2_ref[...] + route_ref[:, 2:3] * rows_scr[slot, 0] + route_ref[:, 3:4] * rows_scr[slot, 1]


def _combine(dest, x2, route, ybuf):
    T = x2.shape[0]
    tm = COMBINE_TM
    grid_spec = pltpu.PrefetchScalarGridSpec(
        num_scalar_prefetch=1,
        grid=(T // tm,),
        in_specs=[
            pl.BlockSpec((tm, D_MODEL), lambda i, d: (i, 0)),
            pl.BlockSpec((tm, 128), lambda i, d: (i, 0)),
            pl.BlockSpec(memory_space=pl.ANY),
        ],
        out_specs=pl.BlockSpec((tm, D_MODEL), lambda i, d: (i, 0)),
        scratch_shapes=[pltpu.VMEM((2, 2, tm, D_MODEL), F32), pltpu.SemaphoreType.DMA((2,))],
    )
    return pl.pallas_call(
        _combine_kernel,
        grid_spec=grid_spec,
        out_shape=jax.ShapeDtypeStruct((T, D_MODEL), F32),
        compiler_params=_params(("arbitrary",), 40),
        name="combine",
    )(dest, x2, route, ybuf)


def _dest_kernel(route_ref, base_ref, o_ref):
    r = route_ref[...]
    lane = lax.broadcasted_iota(I32, r.shape, 1).astype(F32)
    base = base_ref[...]
    d1 = jnp.sum(jnp.where(lane == r[:, 0:1], base, 0.0), axis=-1, keepdims=True) + r[:, 4:5]
    d2 = jnp.sum(jnp.where(lane == r[:, 1:2], base, 0.0), axis=-1, keepdims=True) + r[:, 5:6]
    o_ref[...] = jnp.where(lane == 0, d1, jnp.where(lane == 1, d2, 0.0))


def _dest(route, base):
    T = route.shape[0]
    tm = DEST_TM
    return pl.pallas_call(
        _dest_kernel,
        grid=(T // tm,),
        in_specs=[pl.BlockSpec((tm, 128), lambda i: (i, 0)), pl.BlockSpec((1, 128), lambda i: (0, 0))],
        out_specs=pl.BlockSpec((tm, 128), lambda i: (i, 0)),
        out_shape=jax.ShapeDtypeStruct((T, 128), F32),
        compiler_params=_params(("parallel",), 32),
        name="dest",
    )(route, base)


def _moe_plan(route, counts, max_items):
    counts = counts[0, :N_EXPERTS].astype(I32)
    n_it = (counts + ITEM_ROWS - 1) // ITEM_ROWS
    it_end = jnp.cumsum(n_it)
    it_start = it_end - n_it
    base = jnp.zeros((1, 128), F32).at[0, :N_EXPERTS].set((it_start * ITEM_ROWS).astype(F32))
    dest = _dest(route, base)[:, :2].astype(I32).reshape(-1)
    n_items = it_end[-1]
    w = jnp.arange(max_items + 3, dtype=I32)
    ie = jnp.minimum(jnp.sum((it_end[None, :] <= w[:, None]).astype(I32), axis=1), N_EXPERTS - 1)
    used = w < n_items
    rows = jnp.where(used, jnp.minimum(ITEM_ROWS, counts[ie] - (w - it_start[ie]) * ITEM_ROWS), 0)
    ie = jnp.where(used, ie, ie[jnp.maximum(n_items - 1, 0)])
    return dest.astype(I32), ie.astype(I32), rows.astype(I32)


def _layer(x, mem, norm1_g, w_in, q_norm_g, k_norm_g, conv_w, mem_norm_g, w_mem_kv, mem_q_norm_g, mem_k_norm_g,
           out_norm_g, w_out, norm2_g, w_rg, b_rg, w_re, b_re, w_gate, w_up, w_down):
    B, S, D = x.shape
    T = B * S
    x2d = x.reshape(T, D)
    row = lambda v: v.reshape(1, -1).astype(F32)

    kvm = _memkv(mem.reshape(B * N_MEM, D), row(mem_norm_g), w_mem_kv, row(mem_k_norm_g))
    qkv, rest = _inproj(x2d, row(norm1_g), w_in.astype(BF16), row(q_norm_g), row(k_norm_g), row(mem_q_norm_g))

    slopes = 2.0 ** (-8.0 * jnp.arange(1, N_ATTN_HEADS + 1, dtype=F32) / N_ATTN_HEADS)
    attn = _attention(qkv, slopes, B, S)

    w_router = jnp.zeros((D, 128), F32).at[:, :N_GROUPS].set(w_rg).at[:, N_GROUPS:N_GROUPS + N_EXPERTS].set(w_re)
    b_router = jnp.zeros((1, 128), F32).at[0, :N_GROUPS].set(b_rg).at[0, N_GROUPS:N_GROUPS + N_EXPERTS].set(b_re)
    x2, h2, route, counts = _mixer(attn, rest, kvm, x2d, w_out.astype(BF16), row(out_norm_g), row(norm2_g),
                                   conv_w.astype(F32), w_router.astype(BF16), b_router, S)

    max_items = N_EXPERTS + (2 * T) // ITEM_ROWS
    dest, item_expert, item_rows = _moe_plan(route, counts, max_items)
    xbuf = _dispatch(dest, h2, max_items * ITEM_ROWS)
    ybuf = _ffn(item_expert, item_rows, xbuf, w_gate, w_up, w_down)
    out = _combine(dest, x2, route, ybuf)
    return out.reshape(B, S, D)


def kernel(x, mem, norm1_g, w_in, q_norm_g, k_norm_g, conv_w, mem_norm_g, w_mem_kv, mem_q_norm_g, mem_k_norm_g,
           out_norm_g, w_out, norm2_g, w_router_group, b_router_group, w_router_expert, b_router_expert,
           w_gate, w_up, w_down):
    for l in range(norm1_g.shape[0]):
        x = _layer(x, mem, norm1_g[l], w_in[l], q_norm_g[l], k_norm_g[l], conv_w[l], mem_norm_g[l], w_mem_kv[l],
                   mem_q_norm_g[l], mem_k_norm_g[l], out_norm_g[l], w_out[l], norm2_g[l],
                   w_router_group[l], b_router_group[l], w_router_expert[l], b_router_expert[l],
                   w_gate[l], w_up[l], w_down[l])
    return x
```

```python
import functools

import jax
import jax.numpy as jnp
from jax import lax
from jax.experimental import pallas as pl
from jax.experimental.pallas import tpu as pltpu

F32 = jnp.float32
BF16 = jnp.bfloat16
I32 = jnp.int32

D_MODEL = 2048
HEAD_DIM = 128
N_ATTN_HEADS = 8
ATTN_W = N_ATTN_HEADS * HEAD_DIM
N_MEM_HEADS = 4
MEM_W = N_MEM_HEADS * HEAD_DIM
CONV_W = D_MODEL - ATTN_W - MEM_W
CONV_GROUPS = CONV_W // HEAD_DIM
N_MEM = 256
DILATIONS = (1, 4, 16)
SPAN = 128
PROJ_W = 3 * ATTN_W + 3 * CONV_W + MEM_W
N_GROUPS = 8
EXPERTS_PER_GROUP = 8
N_EXPERTS = N_GROUPS * EXPERTS_PER_GROUP
D_EXPERT = D_MODEL // 2
EPS = 1e-6
NEG_INF = -1e30
QK_SCALE = HEAD_DIM ** -0.5
LOG2E = 1.4426950408889634

MIB = 1024 * 1024
INPROJ_TM = 512
INPROJ_TN = 1024
ATTN_CHUNK = 2048
MIXER_TM = 512
MIXER_SPLIT = 2
DEST_TM = 2048
DISPATCH_TM = 512
COMBINE_TM = 256
ITEM_ROWS = 1152
FFN_SUB = 256
FFN_PIECES = tuple((lo, min(FFN_SUB, ITEM_ROWS - lo)) for lo in range(0, ITEM_ROWS, FFN_SUB))
X_PIECES_AT_STEP = {1: (0,), 2: (1, 2), 3: (3, 4)}
Y_PIECES_AT_STEP = {0: (0, 1, 2), 1: (3, 4)}
FFN_CHUNK = 256
HALO = 16
DMA_UNROLL = 8
ATTN_UNROLL = 16


def _params(semantics, vmem_mib):
    return pltpu.CompilerParams(dimension_semantics=semantics, vmem_limit_bytes=vmem_mib * MIB)


def _rms(v, width):
    return lax.rsqrt(jnp.sum(v * v, axis=-1, keepdims=True) * (1.0 / width) + EPS)


def _memkv_kernel(mem_ref, g_ref, w_ref, kg_ref, o_ref):
    j = pl.program_id(0)
    m = mem_ref[...]
    h = (m * _rms(m, D_MODEL) * g_ref[...]).astype(BF16)
    kv = jnp.dot(h, w_ref[...].astype(BF16), preferred_element_type=F32)
    is_key = j == 0
    for hh in range(N_MEM_HEADS):
        t = kv[:, hh * HEAD_DIM:(hh + 1) * HEAD_DIM]
        tn = t * _rms(t, HEAD_DIM) * kg_ref[...]
        o_ref[hh] = jnp.where(is_key, tn, t).astype(BF16)


def _memkv(mem2d, mem_norm_g, w_mem_kv, mem_k_norm_g):
    rows = mem2d.shape[0]
    return pl.pallas_call(
        _memkv_kernel,
        grid=(2,),
        in_specs=[
            pl.BlockSpec((rows, D_MODEL), lambda j: (0, 0)),
            pl.BlockSpec((1, D_MODEL), lambda j: (0, 0)),
            pl.BlockSpec((D_MODEL, MEM_W), lambda j: (0, j)),
            pl.BlockSpec((1, HEAD_DIM), lambda j: (0, 0)),
        ],
        out_specs=pl.BlockSpec((N_MEM_HEADS, rows, HEAD_DIM), lambda j: (j, 0, 0)),
        out_shape=jax.ShapeDtypeStruct((2 * N_MEM_HEADS, rows, HEAD_DIM), BF16),
        compiler_params=_params(("arbitrary",), 40),
        name="memkv",
    )(mem2d, mem_norm_g, w_mem_kv, mem_k_norm_g)


def _inproj_kernel(x_ref, g1_ref, w_ref, qg_ref, kg_ref, mqg_ref, qkv_ref, rest_ref):
    x = x_ref[...]
    h = (x * _rms(x, D_MODEL) * g1_ref[...]).astype(BF16)

    def normed(t, g):
        return t * _rms(t, HEAD_DIM) * g

    gq = qg_ref[...] * (QK_SCALE * LOG2E)
    gk = kg_ref[...]
    gmq = mqg_ref[...] * QK_SCALE
    groups_per_tile = INPROJ_TN // HEAD_DIM
    for j in range(PROJ_W // INPROJ_TN):
        y = jnp.dot(h, w_ref[:, j * INPROJ_TN:(j + 1) * INPROJ_TN], preferred_element_type=F32)
        for hh in range(groups_per_tile):
            t = y[:, hh * HEAD_DIM:(hh + 1) * HEAD_DIM]
            col = j * groups_per_tile + hh
            if col < 8:
                qkv_ref[col] = normed(t, gq)
            elif col < 16:
                qkv_ref[col] = normed(t, gk)
            elif col < 24:
                qkv_ref[col] = t
            elif col < 36:
                rest_ref[col - 24] = t.astype(BF16)
            else:
                rest_ref[col - 24] = normed(t, gmq).astype(BF16)


def _inproj(x2d, norm1_g, w_in_bf16, q_norm_g, k_norm_g, mem_q_norm_g):
    T = x2d.shape[0]
    tm = INPROJ_TM
    small = lambda n: pl.BlockSpec((1, n), lambda i: (0, 0))
    return pl.pallas_call(
        _inproj_kernel,
        grid=(T // tm,),
        in_specs=[
            pl.BlockSpec((tm, D_MODEL), lambda i: (i, 0)),
            small(D_MODEL),
            pl.BlockSpec((D_MODEL, PROJ_W), lambda i: (0, 0), pipeline_mode=pl.Buffered(1)),
            small(HEAD_DIM), small(HEAD_DIM), small(HEAD_DIM),
        ],
        out_specs=[
            pl.BlockSpec((24, tm, HEAD_DIM), lambda i: (0, i, 0)),
            pl.BlockSpec((16, tm, HEAD_DIM), lambda i: (0, i, 0)),
        ],
        out_shape=[
            jax.ShapeDtypeStruct((24, T, HEAD_DIM), F32),
            jax.ShapeDtypeStruct((16, T, HEAD_DIM), BF16),
        ],
        compiler_params=_params(("parallel",), 52),
        name="inproj",
    )(x2d, norm1_g, w_in_bf16, q_norm_g, k_norm_g, mem_q_norm_g)


def _attn_kernel(slopes_ref, q_ref, k_ref, v_ref, o_ref,
                 kx, vx, q4, k4, v4, o1, l1, o4, l4, o16, l16, out_scr, bias_scr):
    h = pl.program_id(1)
    c = pl.program_id(2)
    C = ATTN_CHUNK

    @pl.when(c == 0)
    def _():
        kx[0:C, :] = jnp.zeros((C, HEAD_DIM), F32)
        vx[0:C, :] = jnp.zeros((C, HEAD_DIM), F32)

    @pl.when(c > 0)
    def _():
        kx[0:C, :] = kx[C:2 * C, :]
        vx[0:C, :] = vx[C:2 * C, :]

    kx[C:2 * C, :] = k_ref[0]
    vx[C:2 * C, :] = v_ref[0]

    for b in range(4):
        q4[b] = q_ref[0, pl.ds(b, C // 4, stride=4), :]
        k4[b] = kx[pl.ds(b, 2 * C // 4, stride=4), :]
        v4[b] = vx[pl.ds(b, 2 * C // 4, stride=4), :]

    slope = slopes_ref[h]
    ii = lax.broadcasted_iota(I32, (SPAN, 2 * SPAN), 0)
    jj = lax.broadcasted_iota(I32, (SPAN, 2 * SPAN), 1)
    dist = ii - jj + SPAN
    band = (dist >= 0) & (dist <= SPAN)
    distf = dist.astype(F32)

    for bi, d in enumerate(DILATIONS):
        neg = (-slope * float(d) * LOG2E) * distf
        bias_scr[2 * bi] = jnp.where(band, neg, NEG_INF)
        bias_scr[2 * bi + 1] = jnp.where(band & (jj >= SPAN), neg, NEG_INF)

    ones = jnp.ones((2 * SPAN, HEAD_DIM), BF16)

    def attend(q, k, v, bias):
        s = lax.dot_general(q.astype(BF16), k.astype(BF16), (((1,), (1,)), ((), ())),
                            preferred_element_type=F32) + bias
        m = jnp.max(s, axis=-1, keepdims=True)
        p = jnp.exp2(s - m).astype(BF16)
        ol = jnp.dot(p, jnp.concatenate([v.astype(BF16), ones], axis=1), preferred_element_type=F32)
        l = ol[:, HEAD_DIM:]
        return ol[:, :HEAD_DIM] * (1.0 / l), m + jnp.log2(l)

    def unit1(u, carry):
        sq = pl.multiple_of(u * SPAN, SPAN)
        sk = pl.multiple_of(C + u * SPAN - SPAN, SPAN)
        first = jnp.logical_and(c == 0, u == 0).astype(I32)
        o, lse = attend(q_ref[0, pl.ds(sq, SPAN), :], kx[pl.ds(sk, 2 * SPAN), :], vx[pl.ds(sk, 2 * SPAN), :],
                        bias_scr[first])
        o1[pl.ds(sq, SPAN), :] = o
        l1[pl.ds(sq, SPAN), :] = lse
        return carry

    def unit4(u, carry):
        b = u % 4
        qb = u // 4
        sq = pl.multiple_of(qb * SPAN, SPAN)
        sk = pl.multiple_of(C // 4 + qb * SPAN - SPAN, SPAN)
        first = jnp.logical_and(c == 0, qb == 0).astype(I32)
        o, lse = attend(q4[b, pl.ds(sq, SPAN), :], k4[b, pl.ds(sk, 2 * SPAN), :], v4[b, pl.ds(sk, 2 * SPAN), :],
                        bias_scr[2 + first])
        o4[b, pl.ds(sq, SPAN), :] = o
        l4[b, pl.ds(sq, SPAN), :] = lse
        return carry

    def unit16(u, carry):
        b = u % 4
        a = u // 4
        first = (c == 0).astype(I32)
        qi = pl.ds(a, SPAN, stride=4)
        ki = pl.ds(a, 2 * SPAN, stride=4)
        o, lse = attend(q4[b, qi, :], k4[b, ki, :], v4[b, ki, :], bias_scr[4 + first])
        o16[b, qi, :] = o
        l16[b, qi, :] = lse
        return carry

    for unit in (unit1, unit4, unit16):
        lax.fori_loop(0, C // SPAN, unit, 0, unroll=ATTN_UNROLL)

    for b in range(4):
        rows_b = pl.ds(b, C // 4, stride=4)
        la, lb, lc = l1[rows_b, :], l4[b], l16[b]
        mx = jnp.maximum(jnp.maximum(la, lb), lc)
        ea, eb, ec = jnp.exp2(la - mx), jnp.exp2(lb - mx), jnp.exp2(lc - mx)
        out_scr[rows_b, :] = (ea * o1[rows_b, :] + eb * o4[b] + ec * o16[b]) * (1.0 / (ea + eb + ec))
    o_ref[0] = out_scr[...].astype(BF16)


def _attention(qkv, slopes, batch, seq):
    T = qkv.shape[1]
    C = ATTN_CHUNK
    nc = seq // C
    blk = lambda off: pl.BlockSpec((1, C, HEAD_DIM), lambda b, h, c, s: (h + off, b * nc + c, 0))
    grid_spec = pltpu.PrefetchScalarGridSpec(
        num_scalar_prefetch=1,
        grid=(batch, N_ATTN_HEADS, nc),
        in_specs=[blk(0), blk(8), blk(16)],
        out_specs=pl.BlockSpec((1, C, HEAD_DIM), lambda b, h, c, s: (h, b * nc + c, 0)),
        scratch_shapes=[
            pltpu.VMEM((2 * C, HEAD_DIM), F32),
            pltpu.VMEM((2 * C, HEAD_DIM), F32),
            pltpu.VMEM((4, C // 4, HEAD_DIM), F32),
            pltpu.VMEM((4, 2 * C // 4, HEAD_DIM), F32),
            pltpu.VMEM((4, 2 * C // 4, HEAD_DIM), F32),
            pltpu.VMEM((C, HEAD_DIM), F32),
            pltpu.VMEM((C, HEAD_DIM), F32),
            pltpu.VMEM((4, C // 4, HEAD_DIM), F32),
            pltpu.VMEM((4, C // 4, HEAD_DIM), F32),
            pltpu.VMEM((4, C // 4, HEAD_DIM), F32),
            pltpu.VMEM((4, C // 4, HEAD_DIM), F32),
            pltpu.VMEM((C, HEAD_DIM), F32),
            pltpu.VMEM((2 * len(DILATIONS), SPAN, 2 * SPAN), F32),
        ],
    )
    return pl.pallas_call(
        _attn_kernel,
        grid_spec=grid_spec,
        out_shape=jax.ShapeDtypeStruct((N_ATTN_HEADS, T, HEAD_DIM), BF16),
        compiler_params=_params(("parallel", "parallel", "arbitrary"), 40),
        name="attn",
    )(slopes, qkv, qkv, qkv)


def _mixer_kernel(attn_ref, rest_ref, halo_ref, kvm_ref, x_ref, wout_ref, og_ref, g2_ref, cw_ref, wr_ref, br_ref,
                  x2_ref, h2_ref, route_ref, counts_ref, y_scr, z_scr, cnt_scr, *, tiles_per_seq):
    i = pl.program_id(0)
    tm = MIXER_TM
    og = og_ref[...]

    @pl.when(i == 0)
    def _():
        cnt_scr[...] = jnp.zeros((1, 128), F32)

    seq_start = (i % tiles_per_seq) == 0
    for g in range(CONV_GROUPS):
        z = rest_ref[CONV_GROUPS + g].astype(F32) * rest_ref[2 * CONV_GROUPS + g].astype(F32)
        zh = halo_ref[CONV_GROUPS + g].astype(F32) * halo_ref[2 * CONV_GROUPS + g].astype(F32)
        z_scr[g, 0:HALO, :] = jnp.where(seq_start, 0.0, zh)
        z_scr[g, HALO:HALO + tm, :] = z

    nb = tm // MIXER_SPLIT
    def mix(blk):
        r0 = blk * nb
        rows = slice(r0, r0 + nb)

        ss = jnp.zeros((nb, 1), F32)
        for hh in range(N_ATTN_HEADS):
            a = attn_ref[hh, rows, :].astype(F32)
            ss = ss + jnp.sum(a * a, axis=-1, keepdims=True)
        r = lax.rsqrt(ss * (1.0 / ATTN_W) + EPS)
        for hh in range(N_ATTN_HEADS):
            lo = hh * HEAD_DIM
            y_scr[rows, lo:lo + HEAD_DIM] = (attn_ref[hh, rows, :].astype(F32) * r * og[:, lo:lo + HEAD_DIM]).astype(BF16)

        convs = []
        ss = jnp.zeros((nb, 1), F32)
        for g in range(CONV_GROUPS):
            bg = rest_ref[g, rows, :].astype(F32)
            z0 = z_scr[g, HALO + r0:HALO + r0 + nb, :]
            z1 = z_scr[g, HALO - 1 + r0:HALO - 1 + r0 + nb, :]
            z2 = z_scr[g, HALO - 2 + r0:HALO - 2 + r0 + nb, :]
            lo = g * HEAD_DIM
            w0 = cw_ref[0:1, lo:lo + HEAD_DIM]
            w1 = cw_ref[1:2, lo:lo + HEAD_DIM]
            w2 = cw_ref[2:3, lo:lo + HEAD_DIM]
            cv = bg * (w2 * z0 + w1 * z1 + w0 * z2)
            convs.append(cv)
            ss = ss + jnp.sum(cv * cv, axis=-1, keepdims=True)
        r = lax.rsqrt(ss * (1.0 / CONV_W) + EPS)
        for g in range(CONV_GROUPS):
            lo = ATTN_W + g * HEAD_DIM
            y_scr[rows, lo:lo + HEAD_DIM] = (convs[g] * r * og[:, lo:lo + HEAD_DIM]).astype(BF16)

        mems = []
        ss = jnp.zeros((nb, 1), F32)
        for hh in range(N_MEM_HEADS):
            mq = rest_ref[3 * CONV_GROUPS + hh, rows, :]
            s = lax.dot_general(mq, kvm_ref[hh], (((1,), (1,)), ((), ())), preferred_element_type=F32)
            m = jnp.max(s, axis=-1, keepdims=True)
            p = jnp.exp(s - m)
            l = jnp.sum(p, axis=-1, keepdims=True)
            o = jnp.dot(p.astype(BF16), kvm_ref[N_MEM_HEADS + hh], preferred_element_type=F32) * (1.0 / l)
            mems.append(o)
            ss = ss + jnp.sum(o * o, axis=-1, keepdims=True)
        r = lax.rsqrt(ss * (1.0 / MEM_W) + EPS)
        for hh in range(N_MEM_HEADS):
            lo = ATTN_W + CONV_W + hh * HEAD_DIM
            y_scr[rows, lo:lo + HEAD_DIM] = (mems[hh] * r * og[:, lo:lo + HEAD_DIM]).astype(BF16)

    def project(blk):
        rows = slice(blk * nb, (blk + 1) * nb)
        x2 = x_ref[rows, :] + jnp.dot(y_scr[rows, :], wout_ref[...], preferred_element_type=F32)
        x2_ref[rows, :] = x2
        h2 = x2 * _rms(x2, D_MODEL) * g2_ref[...]
        h2_ref[rows, :] = h2
        return jnp.dot(h2.astype(BF16), wr_ref[...], preferred_element_type=F32) + br_ref[...]

    def route_rows(blk, lg):
        rows = slice(blk * nb, (blk + 1) * nb)
        lane = lax.broadcasted_iota(I32, (nb, 128), 1).astype(F32)
        gl = jnp.where(lane < N_GROUPS, lg, NEG_INF)
        gmax = jnp.max(gl, axis=-1, keepdims=True)
        g_w = 1.0 / jnp.sum(jnp.exp(gl - gmax), axis=-1, keepdims=True)
        g_idx = jnp.min(jnp.where(gl == gmax, lane, 1e9), axis=-1, keepdims=True)
        e_lo = N_GROUPS + EXPERTS_PER_GROUP * g_idx
        el = jnp.where((lane >= e_lo) & (lane < e_lo + EXPERTS_PER_GROUP), lg, NEG_INF)
        e1 = jnp.max(el, axis=-1, keepdims=True)
        i1 = jnp.min(jnp.where(el == e1, lane, 1e9), axis=-1, keepdims=True)
        el2 = jnp.where(lane == i1, NEG_INF, el)
        e2 = jnp.max(el2, axis=-1, keepdims=True)
        i2 = jnp.min(jnp.where(el2 == e2, lane, 1e9), axis=-1, keepdims=True)
        t = jnp.exp(e2 - e1)
        p1 = 1.0 / (1.0 + t)
        p2 = t * p1

        sel1 = i1 - N_GROUPS
        sel2 = i2 - N_GROUPS
        hot1 = lane == sel1
        hot2 = lane == sel2
        hot = jnp.where(hot1 | hot2, 1.0, 0.0)
        row_id = lax.broadcasted_iota(I32, (nb, nb), 0)
        col_id = lax.broadcasted_iota(I32, (nb, nb), 1)
        earlier = jnp.where(col_id < row_id, 1.0, 0.0).astype(BF16)
        before = jnp.dot(earlier, hot.astype(BF16), preferred_element_type=F32) + cnt_scr[...]
        rank1 = jnp.sum(jnp.where(hot1, before, 0.0), axis=-1, keepdims=True)
        rank2 = jnp.sum(jnp.where(hot2, before, 0.0), axis=-1, keepdims=True)
        cnt_scr[...] = cnt_scr[...] + jnp.sum(hot, axis=0, keepdims=True)

        route_ref[rows, :] = jnp.where(lane == 0, sel1,
                                       jnp.where(lane == 1, sel2,
                                                 jnp.where(lane == 2, g_w * p1,
                                                           jnp.where(lane == 3, g_w * p2,
                                                                     jnp.where(lane == 4, rank1,
                                                                               jnp.where(lane == 5, rank2, 0.0))))))

    mix(0)
    for blk in range(MIXER_SPLIT):
        lg = project(blk)
        if blk + 1 < MIXER_SPLIT:
            mix(blk + 1)
        route_rows(blk, lg)

    counts_ref[...] = cnt_scr[...]


def _mixer(attn, rest, kvm, x2d, w_out_bf16, out_norm_g, norm2_g, conv_w, w_router, b_router, seq):
    T = x2d.shape[0]
    tm = MIXER_TM
    tiles_per_seq = seq // tm
    full = lambda shape: pl.BlockSpec(shape, lambda i: tuple(0 for _ in shape))
    return pl.pallas_call(
        functools.partial(_mixer_kernel, tiles_per_seq=tiles_per_seq),
        grid=(T // tm,),
        in_specs=[
            pl.BlockSpec((N_ATTN_HEADS, tm, HEAD_DIM), lambda i: (0, i, 0)),
            pl.BlockSpec((16, tm, HEAD_DIM), lambda i: (0, i, 0)),
            pl.BlockSpec((16, HALO, HEAD_DIM), lambda i: (0, jnp.maximum(i * (tm // HALO) - 1, 0), 0)),
            pl.BlockSpec((2 * N_MEM_HEADS, N_MEM, HEAD_DIM), lambda i: (0, i // tiles_per_seq, 0)),
            pl.BlockSpec((tm, D_MODEL), lambda i: (i, 0)),
            pl.BlockSpec((D_MODEL, D_MODEL), lambda i: (0, 0), pipeline_mode=pl.Buffered(1)),
            full((1, D_MODEL)),
            full((1, D_MODEL)),
            full((3, CONV_W)),
            full((D_MODEL, 128)),
            full((1, 128)),
        ],
        out_specs=[
            pl.BlockSpec((tm, D_MODEL), lambda i: (i, 0)),
            pl.BlockSpec((tm, D_MODEL), lambda i: (i, 0)),
            pl.BlockSpec((tm, 128), lambda i: (i, 0)),
            pl.BlockSpec((1, 128), lambda i: (0, 0)),
        ],
        out_shape=[
            jax.ShapeDtypeStruct((T, D_MODEL), F32),
            jax.ShapeDtypeStruct((T, D_MODEL), F32),
            jax.ShapeDtypeStruct((T, 128), F32),
            jax.ShapeDtypeStruct((1, 128), F32),
        ],
        scratch_shapes=[
            pltpu.VMEM((tm, D_MODEL), BF16),
            pltpu.VMEM((CONV_GROUPS, HALO + tm, HEAD_DIM), F32),
            pltpu.VMEM((1, 128), F32),
        ],
        compiler_params=_params(("arbitrary",), 56),
        name="mixer",
    )(attn, rest, rest, kvm, x2d, w_out_bf16, out_norm_g, norm2_g, conv_w, w_router, b_router)


def _dispatch_kernel(dest_ref, h_ref, xbuf_ref, sem):
    i = pl.program_id(0)
    tm = DISPATCH_TM
    base = i * tm

    def body(t, carry):
        a = 2 * (base + t)
        src = h_ref.at[pl.ds(t, 1)]
        pltpu.make_async_copy(src, xbuf_ref.at[pl.ds(dest_ref[a], 1)], sem).start()
        pltpu.make_async_copy(src, xbuf_ref.at[pl.ds(dest_ref[a + 1], 1)], sem).start()
        return carry

    lax.fori_loop(0, tm, body, 0, unroll=DMA_UNROLL)
    pltpu.make_async_copy(xbuf_ref.at[pl.ds(0, 2 * tm)], xbuf_ref.at[pl.ds(0, 2 * tm)], sem).wait()


def _dispatch(dest, h2, n_rows):
    T = h2.shape[0]
    tm = DISPATCH_TM
    grid_spec = pltpu.PrefetchScalarGridSpec(
        num_scalar_prefetch=1,
        grid=(T // tm,),
        in_specs=[pl.BlockSpec((tm, D_MODEL), lambda i, d: (i, 0))],
        out_specs=pl.BlockSpec(memory_space=pl.ANY),
        scratch_shapes=[pltpu.SemaphoreType.DMA],
    )
    return pl.pallas_call(
        _dispatch_kernel,
        grid_spec=grid_spec,
        out_shape=jax.ShapeDtypeStruct((n_rows, D_MODEL), F32),
        compiler_params=_params(("arbitrary",), 32),
        name="dispatch",
    )(dest, h2)


def _ffn_kernel(ie_ref, rows_ref, x_ref, wg_ref, wu_ref, wd_ref, y_ref, xg, xb, acc, xsem, ysem):
    w = pl.program_id(0)
    c = pl.program_id(1)

    def subs(v):
        return (v + FFN_SUB - 1) // FFN_SUB

    rows = rows_ref[w]
    ns = subs(rows)
    ns_next = subs(rows_ref[w + 1])
    ns_prev = jnp.where(w > 0, subs(rows_ref[jnp.maximum(w - 1, 0)]), 0)
    ns_prev2 = jnp.where(w > 1, subs(rows_ref[jnp.maximum(w - 2, 0)]), 0)
    p = w % 2

    def x_copy(item, s):
        lo, n = FFN_PIECES[s]
        return pltpu.make_async_copy(x_ref.at[pl.ds(item * ITEM_ROWS + lo, n)], xg.at[pl.ds(lo, n)], xsem)

    def y_copy(item, slot, s):
        lo, n = FFN_PIECES[s]
        return pltpu.make_async_copy(acc.at[slot, pl.ds(lo, n)], y_ref.at[pl.ds(item * ITEM_ROWS + lo, n)],
                                     ysem.at[slot])

    def for_pieces(pieces, count, fn):
        for s in pieces:
            @pl.when(s < count)
            def _(s=s):
                fn(s)

    all_pieces = range(len(FFN_PIECES))

    @pl.when(c == 0)
    def _():
        @pl.when(w == 0)
        def _():
            for_pieces(all_pieces, ns, lambda s: x_copy(0, s).start())

        for_pieces(all_pieces, ns, lambda s: x_copy(w, s).wait())
        for_pieces(all_pieces, ns_prev2, lambda s: y_copy(w - 2, p, s).wait())

    for step, pieces in Y_PIECES_AT_STEP.items():
        @pl.when(c == step)
        def _(pieces=pieces):
            for_pieces(pieces, ns_prev, lambda s: y_copy(w - 1, 1 - p, s).start())

    for step, pieces in X_PIECES_AT_STEP.items():
        @pl.when(c == step)
        def _(pieces=pieces):
            for_pieces(pieces, ns_next, lambda s: x_copy(w + 1, s).start())

    for ns_static in range(1, len(FFN_PIECES) + 1):
        nr = FFN_PIECES[ns_static - 1][0] + FFN_PIECES[ns_static - 1][1]

        @pl.when(ns == ns_static)
        def _(nr=nr):
            @pl.when(c == 0)
            def _():
                ok = lax.broadcasted_iota(I32, (nr, 1), 0) < rows
                xb[0:nr, :] = jnp.where(ok, xg[0:nr, :], 0.0).astype(BF16)
                acc[p, 0:nr, :] = jnp.zeros((nr, D_MODEL), F32)

            wg = wg_ref[0].astype(BF16)
            wu = wu_ref[0].astype(BF16)
            wd = wd_ref[0].astype(BF16)
            x = xb[0:nr, :]
            a = jnp.dot(x, wg, preferred_element_type=F32)
            u = jnp.dot(x, wu, preferred_element_type=F32)
            hm = (a / (1.0 + jnp.exp(-a)) * u).astype(BF16)
            acc[p, 0:nr, :] += jnp.dot(hm, wd, preferred_element_type=F32)

def _ffn(item_expert, item_rows, xbuf, w_gate, w_up, w_down):
    n_chunks = D_EXPERT // FFN_CHUNK
    grid_items = item_rows.shape[0] - 1

    def chunk_of(w, c, ir):
        return jnp.where(ir[w] > 0, c, n_chunks - 1)

    grid_spec = pltpu.PrefetchScalarGridSpec(
        num_scalar_prefetch=2,
        grid=(grid_items, n_chunks),
        in_specs=[
            pl.BlockSpec(memory_space=pl.ANY),
            pl.BlockSpec((1, D_MODEL, FFN_CHUNK), lambda w, c, ie, ir: (ie[w], 0, chunk_of(w, c, ir))),
            pl.BlockSpec((1, D_MODEL, FFN_CHUNK), lambda w, c, ie, ir: (ie[w], 0, chunk_of(w, c, ir))),
            pl.BlockSpec((1, FFN_CHUNK, D_MODEL), lambda w, c, ie, ir: (ie[w], chunk_of(w, c, ir), 0)),
        ],
        out_specs=pl.BlockSpec(memory_space=pl.ANY),
        scratch_shapes=[
            pltpu.VMEM((ITEM_ROWS, D_MODEL), F32),
            pltpu.VMEM((ITEM_ROWS, D_MODEL), BF16),
            pltpu.VMEM((2, ITEM_ROWS, D_MODEL), F32),
            pltpu.SemaphoreType.DMA,
            pltpu.SemaphoreType.DMA((2,)),
        ],
    )
    return pl.pallas_call(
        _ffn_kernel,
        grid_spec=grid_spec,
        out_shape=jax.ShapeDtypeStruct(xbuf.shape, F32),
        compiler_params=_params(("arbitrary", "arbitrary"), 58),
        name="ffn",
    )(item_expert, item_rows, xbuf, w_gate, w_up, w_down)


def _combine_kernel(dest_ref, x2_ref, route_ref, ybuf_ref, o_ref, rows_scr, sem):
    i = pl.program_id(0)
    n = pl.num_programs(0)
    tm = COMBINE_TM

    def gather_tile(tile, slot):
        base = tile * tm

        def body(t, carry):
            a = 2 * (base + t)
            pltpu.make_async_copy(ybuf_ref.at[pl.ds(dest_ref[a], 1)], rows_scr.at[slot, 0, pl.ds(t, 1)],
                                  sem.at[slot]).start()
            pltpu.make_async_copy(ybuf_ref.at[pl.ds(dest_ref[a + 1], 1)], rows_scr.at[slot, 1, pl.ds(t, 1)],
                                  sem.at[slot]).start()
            return carry

        lax.fori_loop(0, tm, body, 0, unroll=DMA_UNROLL)

    slot = i % 2

    @pl.when(i == 0)
    def _():
        gather_tile(0, 0)

    @pl.when(i + 1 < n)
    def _():
        gather_tile(i + 1, 1 - slot)

    for k in range(2):
        pltpu.make_async_copy(ybuf_ref.at[pl.ds(0, tm)], rows_scr.at[slot, k], sem.at[slot]).wait()

    o_ref[...] = x2_ref[...] + route_ref[:, 2:3] * rows_scr[slot, 0] + route_ref[:, 3:4] * rows_scr[slot, 1]


def _combine(dest, x2, route, ybuf):
    T = x2.shape[0]
    tm = COMBINE_TM
    grid_spec = pltpu.PrefetchScalarGridSpec(
        num_scalar_prefetch=1,
        grid=(T // tm,),
        in_specs=[
            pl.BlockSpec((tm, D_MODEL), lambda i, d: (i, 0)),
            pl.BlockSpec((tm, 128), lambda i, d: (i, 0)),
            pl.BlockSpec(memory_space=pl.ANY),
        ],
        out_specs=pl.BlockSpec((tm, D_MODEL), lambda i, d: (i, 0)),
        scratch_shapes=[pltpu.VMEM((2, 2, tm, D_MODEL), F32), pltpu.SemaphoreType.DMA((2,))],
    )
    return pl.pallas_call(
        _combine_kernel,
        grid_spec=grid_spec,
        out_shape=jax.ShapeDtypeStruct((T, D_MODEL), F32),
        compiler_params=_params(("arbitrary",), 40),
        name="combine",
    )(dest, x2, route, ybuf)


def _dest_kernel(route_ref, base_ref, o_ref):
    r = route_ref[...]
    lane = lax.broadcasted_iota(I32, r.shape, 1).astype(F32)
    base = base_ref[...]
    d1 = jnp.sum(jnp.where(lane == r[:, 0:1], base, 0.0), axis=-1, keepdims=True) + r[:, 4:5]
    d2 = jnp.sum(jnp.where(lane == r[:, 1:2], base, 0.0), axis=-1, keepdims=True) + r[:, 5:6]
    o_ref[...] = jnp.where(lane == 0, d1, jnp.where(lane == 1, d2, 0.0))


def _dest(route, base):
    T = route.shape[0]
    tm = DEST_TM
    return pl.pallas_call(
        _dest_kernel,
        grid=(T // tm,),
        in_specs=[pl.BlockSpec((tm, 128), lambda i: (i, 0)), pl.BlockSpec((1, 128), lambda i: (0, 0))],
        out_specs=pl.BlockSpec((tm, 128), lambda i: (i, 0)),
        out_shape=jax.ShapeDtypeStruct((T, 128), F32),
        compiler_params=_params(("parallel",), 32),
        name="dest",
    )(route, base)


def _moe_plan(route, counts, max_items):
    counts = counts[0, :N_EXPERTS].astype(I32)
    n_it = (counts + ITEM_ROWS - 1) // ITEM_ROWS
    it_end = jnp.cumsum(n_it)
    it_start = it_end - n_it
    base = jnp.zeros((1, 128), F32).at[0, :N_EXPERTS].set((it_start * ITEM_ROWS).astype(F32))
    dest = _dest(route, base)[:, :2].astype(I32).reshape(-1)
    n_items = it_end[-1]
    w = jnp.arange(max_items + 3, dtype=I32)
    ie = jnp.minimum(jnp.sum((it_end[None, :] <= w[:, None]).astype(I32), axis=1), N_EXPERTS - 1)
    used = w < n_items
    rows = jnp.where(used, jnp.minimum(ITEM_ROWS, counts[ie] - (w - it_start[ie]) * ITEM_ROWS), 0)
    ie = jnp.where(used, ie, ie[jnp.maximum(n_items - 1, 0)])
    return dest.astype(I32), ie.astype(I32), rows.astype(I32)


def _layer(x, mem, norm1_g, w_in, q_norm_g, k_norm_g, conv_w, mem_norm_g, w_mem_kv, mem_q_norm_g, mem_k_norm_g,
           out_norm_g, w_out, norm2_g, w_rg, b_rg, w_re, b_re, w_gate, w_up, w_down):
    B, S, D = x.shape
    T = B * S
    x2d = x.reshape(T, D)
    row = lambda v: v.reshape(1, -1).astype(F32)

    kvm = _memkv(mem.reshape(B * N_MEM, D), row(mem_norm_g), w_mem_kv, row(mem_k_norm_g))
    qkv, rest = _inproj(x2d, row(norm1_g), w_in.astype(BF16), row(q_norm_g), row(k_norm_g), row(mem_q_norm_g))

    slopes = 2.0 ** (-8.0 * jnp.arange(1, N_ATTN_HEADS + 1, dtype=F32) / N_ATTN_HEADS)
    attn = _attention(qkv, slopes, B, S)

    w_router = jnp.zeros((D, 128), F32).at[:, :N_GROUPS].set(w_rg).at[:, N_GROUPS:N_GROUPS + N_EXPERTS].set(w_re)
    b_router = jnp.zeros((1, 128), F32).at[0, :N_GROUPS].set(b_rg).at[0, N_GROUPS:N_GROUPS + N_EXPERTS].set(b_re)
    x2, h2, route, counts = _mixer(attn, rest, kvm, x2d, w_out.astype(BF16), row(out_norm_g), row(norm2_g),
                                   conv_w.astype(F32), w_router.astype(BF16), b_router, S)

    max_items = N_EXPERTS + (2 * T) // ITEM_ROWS
    dest, item_expert, item_rows = _moe_plan(route, counts, max_items)
    xbuf = _dispatch(dest, h2, max_items * ITEM_ROWS)
    ybuf = _ffn(item_expert, item_rows, xbuf, w_gate, w_up, w_down)
    out = _combine(dest, x2, route, ybuf)
    return out.reshape(B, S, D)


def kernel(x, mem, norm1_g, w_in, q_norm_g, k_norm_g, conv_w, mem_norm_g, w_mem_kv, mem_q_norm_g, mem_k_norm_g,
           out_norm_g, w_out, norm2_g, w_router_group, b_router_group, w_router_expert, b_router_expert,
           w_gate, w_up, w_down):
    for l in range(norm1_g.shape[0]):
        x = _layer(x, mem, norm1_g[l], w_in[l], q_norm_g[l], k_norm_g[l], conv_w[l], mem_norm_g[l], w_mem_kv[l],
                   mem_q_norm_g[l], mem_k_norm_g[l], out_norm_g[l], w_out[l], norm2_g[l],
                   w_router_group[l], b_router_group[l], w_router_expert[l], b_router_expert[l],
                   w_gate[l], w_up[l], w_down[l])
    return x
```

```python
import functools

import jax
import jax.numpy as jnp
from jax import lax
from jax.experimental import pallas as pl
from jax.experimental.pallas import tpu as pltpu

F32 = jnp.float32
BF16 = jnp.bfloat16
I32 = jnp.int32

D_MODEL = 2048
HEAD_DIM = 128
N_ATTN_HEADS = 8
ATTN_W = N_ATTN_HEADS * HEAD_DIM
N_MEM_HEADS = 4
MEM_W = N_MEM_HEADS * HEAD_DIM
CONV_W = D_MODEL - ATTN_W - MEM_W
CONV_GROUPS = CONV_W // HEAD_DIM
N_MEM = 256
DILATIONS = (1, 4, 16)
SPAN = 128
PROJ_W = 3 * ATTN_W + 3 * CONV_W + MEM_W
N_GROUPS = 8
EXPERTS_PER_GROUP = 8
N_EXPERTS = N_GROUPS * EXPERTS_PER_GROUP
D_EXPERT = D_MODEL // 2
EPS = 1e-6
NEG_INF = -1e30
QK_SCALE = HEAD_DIM ** -0.5
LOG2E = 1.4426950408889634

MIB = 1024 * 1024
INPROJ_TM = 512
INPROJ_TN = 1024
ATTN_CHUNK = 2048
MIXER_TM = 512
MIXER_SPLIT = 2
DEST_TM = 2048
DISPATCH_TM = 1024
COMBINE_TM = 512
ITEM_ROWS = 1152
FFN_SUB = 256
FFN_PIECES = tuple((lo, min(FFN_SUB, ITEM_ROWS - lo)) for lo in range(0, ITEM_ROWS, FFN_SUB))
X_PIECES_AT_STEP = {1: (0,), 2: (1, 2), 3: (3, 4)}
Y_PIECES_AT_STEP = {0: (0, 1, 2), 1: (3, 4)}
FFN_CHUNK = 256
HALO = 16
DMA_UNROLL = 8
ATTN_UNROLL = 16


def _params(semantics, vmem_mib):
    return pltpu.CompilerParams(dimension_semantics=semantics, vmem_limit_bytes=vmem_mib * MIB)


def _rms(v, width):
    return lax.rsqrt(jnp.sum(v * v, axis=-1, keepdims=True) * (1.0 / width) + EPS)


def _memkv_kernel(mem_ref, g_ref, w_ref, kg_ref, o_ref):
    j = pl.program_id(0)
    m = mem_ref[...]
    h = (m * _rms(m, D_MODEL) * g_ref[...]).astype(BF16)
    kv = jnp.dot(h, w_ref[...].astype(BF16), preferred_element_type=F32)
    is_key = j == 0
    for hh in range(N_MEM_HEADS):
        t = kv[:, hh * HEAD_DIM:(hh + 1) * HEAD_DIM]
        tn = t * _rms(t, HEAD_DIM) * kg_ref[...]
        o_ref[hh] = jnp.where(is_key, tn, t).astype(BF16)


def _memkv(mem2d, mem_norm_g, w_mem_kv, mem_k_norm_g):
    rows = mem2d.shape[0]
    return pl.pallas_call(
        _memkv_kernel,
        grid=(2,),
        in_specs=[
            pl.BlockSpec((rows, D_MODEL), lambda j: (0, 0)),
            pl.BlockSpec((1, D_MODEL), lambda j: (0, 0)),
            pl.BlockSpec((D_MODEL, MEM_W), lambda j: (0, j)),
            pl.BlockSpec((1, HEAD_DIM), lambda j: (0, 0)),
        ],
        out_specs=pl.BlockSpec((N_MEM_HEADS, rows, HEAD_DIM), lambda j: (j, 0, 0)),
        out_shape=jax.ShapeDtypeStruct((2 * N_MEM_HEADS, rows, HEAD_DIM), BF16),
        compiler_params=_params(("arbitrary",), 40),
        name="memkv",
    )(mem2d, mem_norm_g, w_mem_kv, mem_k_norm_g)


def _inproj_kernel(x_ref, g1_ref, w_ref, qg_ref, kg_ref, mqg_ref, qkv_ref, rest_ref):
    x = x_ref[...]
    h = (x * _rms(x, D_MODEL) * g1_ref[...]).astype(BF16)

    def normed(t, g):
        return t * _rms(t, HEAD_DIM) * g

    gq = qg_ref[...] * (QK_SCALE * LOG2E)
    gk = kg_ref[...]
    gmq = mqg_ref[...] * QK_SCALE
    groups_per_tile = INPROJ_TN // HEAD_DIM
    for j in range(PROJ_W // INPROJ_TN):
        y = jnp.dot(h, w_ref[:, j * INPROJ_TN:(j + 1) * INPROJ_TN], preferred_element_type=F32)
        for hh in range(groups_per_tile):
            t = y[:, hh * HEAD_DIM:(hh + 1) * HEAD_DIM]
            col = j * groups_per_tile + hh
            if col < 8:
                qkv_ref[col] = normed(t, gq)
            elif col < 16:
                qkv_ref[col] = normed(t, gk)
            elif col < 24:
                qkv_ref[col] = t
            elif col < 36:
                rest_ref[col - 24] = t.astype(BF16)
            else:
                rest_ref[col - 24] = normed(t, gmq).astype(BF16)


def _inproj(x2d, norm1_g, w_in_bf16, q_norm_g, k_norm_g, mem_q_norm_g):
    T = x2d.shape[0]
    tm = INPROJ_TM
    small = lambda n: pl.BlockSpec((1, n), lambda i: (0, 0))
    return pl.pallas_call(
        _inproj_kernel,
        grid=(T // tm,),
        in_specs=[
            pl.BlockSpec((tm, D_MODEL), lambda i: (i, 0)),
            small(D_MODEL),
            pl.BlockSpec((D_MODEL, PROJ_W), lambda i: (0, 0), pipeline_mode=pl.Buffered(1)),
            small(HEAD_DIM), small(HEAD_DIM), small(HEAD_DIM),
        ],
        out_specs=[
            pl.BlockSpec((24, tm, HEAD_DIM), lambda i: (0, i, 0)),
            pl.BlockSpec((16, tm, HEAD_DIM), lambda i: (0, i, 0)),
        ],
        out_shape=[
            jax.ShapeDtypeStruct((24, T, HEAD_DIM), F32),
            jax.ShapeDtypeStruct((16, T, HEAD_DIM), BF16),
        ],
        compiler_params=_params(("parallel",), 52),
        name="inproj",
    )(x2d, norm1_g, w_in_bf16, q_norm_g, k_norm_g, mem_q_norm_g)


def _attn_kernel(slopes_ref, q_ref, k_ref, v_ref, o_ref,
                 kx, vx, q4, k4, v4, o1, l1, o4, l4, o16, l16, out_scr, bias_scr):
    h = pl.program_id(1)
    c = pl.program_id(2)
    C = ATTN_CHUNK

    @pl.when(c == 0)
    def _():
        kx[0:C, :] = jnp.zeros((C, HEAD_DIM), F32)
        vx[0:C, :] = jnp.zeros((C, HEAD_DIM), F32)

    @pl.when(c > 0)
    def _():
        kx[0:C, :] = kx[C:2 * C, :]
        vx[0:C, :] = vx[C:2 * C, :]

    kx[C:2 * C, :] = k_ref[0]
    vx[C:2 * C, :] = v_ref[0]

    for b in range(4):
        q4[b] = q_ref[0, pl.ds(b, C // 4, stride=4), :]
        k4[b] = kx[pl.ds(b, 2 * C // 4, stride=4), :]
        v4[b] = vx[pl.ds(b, 2 * C // 4, stride=4), :]

    slope = slopes_ref[h]
    ii = lax.broadcasted_iota(I32, (SPAN, 2 * SPAN), 0)
    jj = lax.broadcasted_iota(I32, (SPAN, 2 * SPAN), 1)
    dist = ii - jj + SPAN
    band = (dist >= 0) & (dist <= SPAN)
    distf = dist.astype(F32)

    for bi, d in enumerate(DILATIONS):
        neg = (-slope * float(d) * LOG2E) * distf
        bias_scr[2 * bi] = jnp.where(band, neg, NEG_INF)
        bias_scr[2 * bi + 1] = jnp.where(band & (jj >= SPAN), neg, NEG_INF)

    ones = jnp.ones((2 * SPAN, HEAD_DIM), BF16)

    def attend(q, k, v, bias):
        s = lax.dot_general(q.astype(BF16), k.astype(BF16), (((1,), (1,)), ((), ())),
                            preferred_element_type=F32) + bias
        m = jnp.max(s, axis=-1, keepdims=True)
        p = jnp.exp2(s - m).astype(BF16)
        ol = jnp.dot(p, jnp.concatenate([v.astype(BF16), ones], axis=1), preferred_element_type=F32)
        l = ol[:, HEAD_DIM:]
        return ol[:, :HEAD_DIM] * (1.0 / l), m + jnp.log2(l)

    def unit1(u, carry):
        sq = pl.multiple_of(u * SPAN, SPAN)
        sk = pl.multiple_of(C + u * SPAN - SPAN, SPAN)
        first = jnp.logical_and(c == 0, u == 0).astype(I32)
        o, lse = attend(q_ref[0, pl.ds(sq, SPAN), :], kx[pl.ds(sk, 2 * SPAN), :], vx[pl.ds(sk, 2 * SPAN), :],
                        bias_scr[first])
        o1[pl.ds(sq, SPAN), :] = o
        l1[pl.ds(sq, SPAN), :] = lse
        return carry

    def unit4(u, carry):
        b = u % 4
        qb = u // 4
        sq = pl.multiple_of(qb * SPAN, SPAN)
        sk = pl.multiple_of(C // 4 + qb * SPAN - SPAN, SPAN)
        first = jnp.logical_and(c == 0, qb == 0).astype(I32)
        o, lse = attend(q4[b, pl.ds(sq, SPAN), :], k4[b, pl.ds(sk, 2 * SPAN), :], v4[b, pl.ds(sk, 2 * SPAN), :],
                        bias_scr[2 + first])
        o4[b, pl.ds(sq, SPAN), :] = o
        l4[b, pl.ds(sq, SPAN), :] = lse
        return carry

    def unit16(u, carry):
        b = u % 4
        a = u // 4
        first = (c == 0).astype(I32)
        qi = pl.ds(a, SPAN, stride=4)
        ki = pl.ds(a, 2 * SPAN, stride=4)
        o, lse = attend(q4[b, qi, :], k4[b, ki, :], v4[b, ki, :], bias_scr[4 + first])
        o16[b, qi, :] = o
        l16[b, qi, :] = lse
        return carry

    for unit in (unit1, unit4, unit16):
        lax.fori_loop(0, C // SPAN, unit, 0, unroll=ATTN_UNROLL)

    for b in range(4):
        rows_b = pl.ds(b, C // 4, stride=4)
        la, lb, lc = l1[rows_b, :], l4[b], l16[b]
        mx = jnp.maximum(jnp.maximum(la, lb), lc)
        ea, eb, ec = jnp.exp2(la - mx), jnp.exp2(lb - mx), jnp.exp2(lc - mx)
        out_scr[rows_b, :] = (ea * o1[rows_b, :] + eb * o4[b] + ec * o16[b]) * (1.0 / (ea + eb + ec))
    o_ref[0] = out_scr[...].astype(BF16)


def _attention(qkv, slopes, batch, seq):
    T = qkv.shape[1]
    C = ATTN_CHUNK
    nc = seq // C
    blk = lambda off: pl.BlockSpec((1, C, HEAD_DIM), lambda b, h, c, s: (h + off, b * nc + c, 0))
    grid_spec = pltpu.PrefetchScalarGridSpec(
        num_scalar_prefetch=1,
        grid=(batch, N_ATTN_HEADS, nc),
        in_specs=[blk(0), blk(8), blk(16)],
        out_specs=pl.BlockSpec((1, C, HEAD_DIM), lambda b, h, c, s: (h, b * nc + c, 0)),
        scratch_shapes=[
            pltpu.VMEM((2 * C, HEAD_DIM), F32),
            pltpu.VMEM((2 * C, HEAD_DIM), F32),
            pltpu.VMEM((4, C // 4, HEAD_DIM), F32),
            pltpu.VMEM((4, 2 * C // 4, HEAD_DIM), F32),
            pltpu.VMEM((4, 2 * C // 4, HEAD_DIM), F32),
            pltpu.VMEM((C, HEAD_DIM), F32),
            pltpu.VMEM((C, HEAD_DIM), F32),
            pltpu.VMEM((4, C // 4, HEAD_DIM), F32),
            pltpu.VMEM((4, C // 4, HEAD_DIM), F32),
            pltpu.VMEM((4, C // 4, HEAD_DIM), F32),
            pltpu.VMEM((4, C // 4, HEAD_DIM), F32),
            pltpu.VMEM((C, HEAD_DIM), F32),
            pltpu.VMEM((2 * len(DILATIONS), SPAN, 2 * SPAN), F32),
        ],
    )
    return pl.pallas_call(
        _attn_kernel,
        grid_spec=grid_spec,
        out_shape=jax.ShapeDtypeStruct((N_ATTN_HEADS, T, HEAD_DIM), BF16),
        compiler_params=_params(("parallel", "parallel", "arbitrary"), 40),
        name="attn",
    )(slopes, qkv, qkv, qkv)


def _mixer_kernel(attn_ref, rest_ref, halo_ref, kvm_ref, x_ref, wout_ref, og_ref, g2_ref, cw_ref, wr_ref, br_ref,
                  x2_ref, h2_ref, route_ref, counts_ref, y_scr, z_scr, cnt_scr, *, tiles_per_seq):
    i = pl.program_id(0)
    tm = MIXER_TM
    og = og_ref[...]

    @pl.when(i == 0)
    def _():
        cnt_scr[...] = jnp.zeros((1, 128), F32)

    seq_start = (i % tiles_per_seq) == 0
    for g in range(CONV_GROUPS):
        z = rest_ref[CONV_GROUPS + g].astype(F32) * rest_ref[2 * CONV_GROUPS + g].astype(F32)
        zh = halo_ref[CONV_GROUPS + g].astype(F32) * halo_ref[2 * CONV_GROUPS + g].astype(F32)
        z_scr[g, 0:HALO, :] = jnp.where(seq_start, 0.0, zh)
        z_scr[g, HALO:HALO + tm, :] = z

    nb = tm // MIXER_SPLIT
    def mix(blk):
        r0 = blk * nb
        rows = slice(r0, r0 + nb)

        ss = jnp.zeros((nb, 1), F32)
        for hh in range(N_ATTN_HEADS):
            a = attn_ref[hh, rows, :].astype(F32)
            ss = ss + jnp.sum(a * a, axis=-1, keepdims=True)
        r = lax.rsqrt(ss * (1.0 / ATTN_W) + EPS)
        for hh in range(N_ATTN_HEADS):
            lo = hh * HEAD_DIM
            y_scr[rows, lo:lo + HEAD_DIM] = (attn_ref[hh, rows, :].astype(F32) * r * og[:, lo:lo + HEAD_DIM]).astype(BF16)

        convs = []
        ss = jnp.zeros((nb, 1), F32)
        for g in range(CONV_GROUPS):
            bg = rest_ref[g, rows, :].astype(F32)
            z0 = z_scr[g, HALO + r0:HALO + r0 + nb, :]
            z1 = z_scr[g, HALO - 1 + r0:HALO - 1 + r0 + nb, :]
            z2 = z_scr[g, HALO - 2 + r0:HALO - 2 + r0 + nb, :]
            lo = g * HEAD_DIM
            w0 = cw_ref[0:1, lo:lo + HEAD_DIM]
            w1 = cw_ref[1:2, lo:lo + HEAD_DIM]
            w2 = cw_ref[2:3, lo:lo + HEAD_DIM]
            cv = bg * (w2 * z0 + w1 * z1 + w0 * z2)
            convs.append(cv)
            ss = ss + jnp.sum(cv * cv, axis=-1, keepdims=True)
        r = lax.rsqrt(ss * (1.0 / CONV_W) + EPS)
        for g in range(CONV_GROUPS):
            lo = ATTN_W + g * HEAD_DIM
            y_scr[rows, lo:lo + HEAD_DIM] = (convs[g] * r * og[:, lo:lo + HEAD_DIM]).astype(BF16)

        mems = []
        ss = jnp.zeros((nb, 1), F32)
        for hh in range(N_MEM_HEADS):
            mq = rest_ref[3 * CONV_GROUPS + hh, rows, :]
            s = lax.dot_general(mq, kvm_ref[hh], (((1,), (1,)), ((), ())), preferred_element_type=F32)
            m = jnp.max(s, axis=-1, keepdims=True)
            p = jnp.exp(s - m)
            l = jnp.sum(p, axis=-1, keepdims=True)
            o = jnp.dot(p.astype(BF16), kvm_ref[N_MEM_HEADS + hh], preferred_element_type=F32) * (1.0 / l)
            mems.append(o)
            ss = ss + jnp.sum(o * o, axis=-1, keepdims=True)
        r = lax.rsqrt(ss * (1.0 / MEM_W) + EPS)
        for hh in range(N_MEM_HEADS):
            lo = ATTN_W + CONV_W + hh * HEAD_DIM
            y_scr[rows, lo:lo + HEAD_DIM] = (mems[hh] * r * og[:, lo:lo + HEAD_DIM]).astype(BF16)

    def project(blk):
        rows = slice(blk * nb, (blk + 1) * nb)
        x2 = x_ref[rows, :] + jnp.dot(y_scr[rows, :], wout_ref[...], preferred_element_type=F32)
        x2_ref[rows, :] = x2
        h2 = x2 * _rms(x2, D_MODEL) * g2_ref[...]
        h2_ref[rows, :] = h2
        return jnp.dot(h2.astype(BF16), wr_ref[...], preferred_element_type=F32) + br_ref[...]

    def route_rows(blk, lg):
        rows = slice(blk * nb, (blk + 1) * nb)
        lane = lax.broadcasted_iota(I32, (nb, 128), 1).astype(F32)
        gl = jnp.where(lane < N_GROUPS, lg, NEG_INF)
        gmax = jnp.max(gl, axis=-1, keepdims=True)
        g_w = 1.0 / jnp.sum(jnp.exp(gl - gmax), axis=-1, keepdims=True)
        g_idx = jnp.min(jnp.where(gl == gmax, lane, 1e9), axis=-1, keepdims=True)
        e_lo = N_GROUPS + EXPERTS_PER_GROUP * g_idx
        el = jnp.where((lane >= e_lo) & (lane < e_lo + EXPERTS_PER_GROUP), lg, NEG_INF)
        e1 = jnp.max(el, axis=-1, keepdims=True)
        i1 = jnp.min(jnp.where(el == e1, lane, 1e9), axis=-1, keepdims=True)
        el2 = jnp.where(lane == i1, NEG_INF, el)
        e2 = jnp.max(el2, axis=-1, keepdims=True)
        i2 = jnp.min(jnp.where(el2 == e2, lane, 1e9), axis=-1, keepdims=True)
        t = jnp.exp(e2 - e1)
        p1 = 1.0 / (1.0 + t)
        p2 = t * p1

        sel1 = i1 - N_GROUPS
        sel2 = i2 - N_GROUPS
        hot1 = lane == sel1
        hot2 = lane == sel2
        hot = jnp.where(hot1 | hot2, 1.0, 0.0)
        row_id = lax.broadcasted_iota(I32, (nb, nb), 0)
        col_id = lax.broadcasted_iota(I32, (nb, nb), 1)
        earlier = jnp.where(col_id < row_id, 1.0, 0.0).astype(BF16)
        before = jnp.dot(earlier, hot.astype(BF16), preferred_element_type=F32) + cnt_scr[...]
        rank1 = jnp.sum(jnp.where(hot1, before, 0.0), axis=-1, keepdims=True)
        rank2 = jnp.sum(jnp.where(hot2, before, 0.0), axis=-1, keepdims=True)
        cnt_scr[...] = cnt_scr[...] + jnp.sum(hot, axis=0, keepdims=True)

        route_ref[rows, :] = jnp.where(lane == 0, sel1,
                                       jnp.where(lane == 1, sel2,
                                                 jnp.where(lane == 2, g_w * p1,
                                                           jnp.where(lane == 3, g_w * p2,
                                                                     jnp.where(lane == 4, rank1,
                                                                               jnp.where(lane == 5, rank2, 0.0))))))

    mix(0)
    for blk in range(MIXER_SPLIT):
        lg = project(blk)
        if blk + 1 < MIXER_SPLIT:
            mix(blk + 1)
        route_rows(blk, lg)

    counts_ref[...] = cnt_scr[...]


def _mixer(attn, rest, kvm, x2d, w_out_bf16, out_norm_g, norm2_g, conv_w, w_router, b_router, seq):
    T = x2d.shape[0]
    tm = MIXER_TM
    tiles_per_seq = seq // tm
    full = lambda shape: pl.BlockSpec(shape, lambda i: tuple(0 for _ in shape))
    return pl.pallas_call(
        functools.partial(_mixer_kernel, tiles_per_seq=tiles_per_seq),
        grid=(T // tm,),
        in_specs=[
            pl.BlockSpec((N_ATTN_HEADS, tm, HEAD_DIM), lambda i: (0, i, 0)),
            pl.BlockSpec((16, tm, HEAD_DIM), lambda i: (0, i, 0)),
            pl.BlockSpec((16, HALO, HEAD_DIM), lambda i: (0, jnp.maximum(i * (tm // HALO) - 1, 0), 0)),
            pl.BlockSpec((2 * N_MEM_HEADS, N_MEM, HEAD_DIM), lambda i: (0, i // tiles_per_seq, 0)),
            pl.BlockSpec((tm, D_MODEL), lambda i: (i, 0)),
            pl.BlockSpec((D_MODEL, D_MODEL), lambda i: (0, 0), pipeline_mode=pl.Buffered(1)),
            full((1, D_MODEL)),
            full((1, D_MODEL)),
            full((3, CONV_W)),
            full((D_MODEL, 128)),
            full((1, 128)),
        ],
        out_specs=[
            pl.BlockSpec((tm, D_MODEL), lambda i: (i, 0)),
            pl.BlockSpec((tm, D_MODEL), lambda i: (i, 0)),
            pl.BlockSpec((tm, 128), lambda i: (i, 0)),
            pl.BlockSpec((1, 128), lambda i: (0, 0)),
        ],
        out_shape=[
            jax.ShapeDtypeStruct((T, D_MODEL), F32),
            jax.ShapeDtypeStruct((T, D_MODEL), F32),
            jax.ShapeDtypeStruct((T, 128), F32),
            jax.ShapeDtypeStruct((1, 128), F32),
        ],
        scratch_shapes=[
            pltpu.VMEM((tm, D_MODEL), BF16),
            pltpu.VMEM((CONV_GROUPS, HALO + tm, HEAD_DIM), F32),
            pltpu.VMEM((1, 128), F32),
        ],
        compiler_params=_params(("arbitrary",), 56),
        name="mixer",
    )(attn, rest, rest, kvm, x2d, w_out_bf16, out_norm_g, norm2_g, conv_w, w_router, b_router)


def _dispatch_kernel(dest_ref, h_ref, xbuf_ref, sem):
    i = pl.program_id(0)
    tm = DISPATCH_TM
    base = i * tm

    def body(t, carry):
        a = 2 * (base + t)
        src = h_ref.at[pl.ds(t, 1)]
        pltpu.make_async_copy(src, xbuf_ref.at[pl.ds(dest_ref[a], 1)], sem).start()
        pltpu.make_async_copy(src, xbuf_ref.at[pl.ds(dest_ref[a + 1], 1)], sem).start()
        return carry

    lax.fori_loop(0, tm, body, 0, unroll=DMA_UNROLL)
    pltpu.make_async_copy(xbuf_ref.at[pl.ds(0, 2 * tm)], xbuf_ref.at[pl.ds(0, 2 * tm)], sem).wait()


def _dispatch(dest, h2, n_rows):
    T = h2.shape[0]
    tm = DISPATCH_TM
    grid_spec = pltpu.PrefetchScalarGridSpec(
        num_scalar_prefetch=1,
        grid=(T // tm,),
        in_specs=[pl.BlockSpec((tm, D_MODEL), lambda i, d: (i, 0))],
        out_specs=pl.BlockSpec(memory_space=pl.ANY),
        scratch_shapes=[pltpu.SemaphoreType.DMA],
    )
    return pl.pallas_call(
        _dispatch_kernel,
        grid_spec=grid_spec,
        out_shape=jax.ShapeDtypeStruct((n_rows, D_MODEL), F32),
        compiler_params=_params(("arbitrary",), 32),
        name="dispatch",
    )(dest, h2)


def _ffn_kernel(ie_ref, rows_ref, x_ref, wg_ref, wu_ref, wd_ref, y_ref, xg, xb, acc, xsem, ysem):
    w = pl.program_id(0)
    c = pl.program_id(1)

    def subs(v):
        return (v + FFN_SUB - 1) // FFN_SUB

    rows = rows_ref[w]
    ns = subs(rows)
    ns_next = subs(rows_ref[w + 1])
    ns_prev = jnp.where(w > 0, subs(rows_ref[jnp.maximum(w - 1, 0)]), 0)
    ns_prev2 = jnp.where(w > 1, subs(rows_ref[jnp.maximum(w - 2, 0)]), 0)
    p = w % 2

    def x_copy(item, s):
        lo, n = FFN_PIECES[s]
        return pltpu.make_async_copy(x_ref.at[pl.ds(item * ITEM_ROWS + lo, n)], xg.at[pl.ds(lo, n)], xsem)

    def y_copy(item, slot, s):
        lo, n = FFN_PIECES[s]
        return pltpu.make_async_copy(acc.at[slot, pl.ds(lo, n)], y_ref.at[pl.ds(item * ITEM_ROWS + lo, n)],
                                     ysem.at[slot])

    def for_pieces(pieces, count, fn):
        for s in pieces:
            @pl.when(s < count)
            def _(s=s):
                fn(s)

    all_pieces = range(len(FFN_PIECES))

    @pl.when(c == 0)
    def _():
        @pl.when(w == 0)
        def _():
            for_pieces(all_pieces, ns, lambda s: x_copy(0, s).start())

        for_pieces(all_pieces, ns, lambda s: x_copy(w, s).wait())
        for_pieces(all_pieces, ns_prev2, lambda s: y_copy(w - 2, p, s).wait())

    for step, pieces in Y_PIECES_AT_STEP.items():
        @pl.when(c == step)
        def _(pieces=pieces):
            for_pieces(pieces, ns_prev, lambda s: y_copy(w - 1, 1 - p, s).start())

    for step, pieces in X_PIECES_AT_STEP.items():
        @pl.when(c == step)
        def _(pieces=pieces):
            for_pieces(pieces, ns_next, lambda s: x_copy(w + 1, s).start())

    for ns_static in range(1, len(FFN_PIECES) + 1):
        nr = FFN_PIECES[ns_static - 1][0] + FFN_PIECES[ns_static - 1][1]

        @pl.when(ns == ns_static)
        def _(nr=nr):
            @pl.when(c == 0)
            def _():
                ok = lax.broadcasted_iota(I32, (nr, 1), 0) < rows
                xb[0:nr, :] = jnp.where(ok, xg[0:nr, :], 0.0).astype(BF16)
                acc[p, 0:nr, :] = jnp.zeros((nr, D_MODEL), F32)

            wg = wg_ref[0].astype(BF16)
            wu = wu_ref[0].astype(BF16)
            wd = wd_ref[0].astype(BF16)
            x = xb[0:nr, :]
            a = jnp.dot(x, wg, preferred_element_type=F32)
            u = jnp.dot(x, wu, preferred_element_type=F32)
            hm = (a / (1.0 + jnp.exp(-a)) * u).astype(BF16)
            acc[p, 0:nr, :] += jnp.dot(hm, wd, preferred_element_type=F32)

def _ffn(item_expert, item_rows, xbuf, w_gate, w_up, w_down):
    n_chunks = D_EXPERT // FFN_CHUNK
    grid_items = item_rows.shape[0] - 1

    def chunk_of(w, c, ir):
        return jnp.where(ir[w] > 0, c, n_chunks - 1)

    grid_spec = pltpu.PrefetchScalarGridSpec(
        num_scalar_prefetch=2,
        grid=(grid_items, n_chunks),
        in_specs=[
            pl.BlockSpec(memory_space=pl.ANY),
            pl.BlockSpec((1, D_MODEL, FFN_CHUNK), lambda w, c, ie, ir: (ie[w], 0, chunk_of(w, c, ir))),
            pl.BlockSpec((1, D_MODEL, FFN_CHUNK), lambda w, c, ie, ir: (ie[w], 0, chunk_of(w, c, ir))),
            pl.BlockSpec((1, FFN_CHUNK, D_MODEL), lambda w, c, ie, ir: (ie[w], chunk_of(w, c, ir), 0)),
        ],
        out_specs=pl.BlockSpec(memory_space=pl.ANY),
        scratch_shapes=[
            pltpu.VMEM((ITEM_ROWS, D_MODEL), F32),
            pltpu.VMEM((ITEM_ROWS, D_MODEL), BF16),
            pltpu.VMEM((2, ITEM_ROWS, D_MODEL), F32),
            pltpu.SemaphoreType.DMA,
            pltpu.SemaphoreType.DMA((2,)),
        ],
    )
    return pl.pallas_call(
        _ffn_kernel,
        grid_spec=grid_spec,
        out_shape=jax.ShapeDtypeStruct(xbuf.shape, F32),
        compiler_params=_params(("arbitrary", "arbitrary"), 58),
        name="ffn",
    )(item_expert, item_rows, xbuf, w_gate, w_up, w_down)


def _combine_kernel(dest_ref, x2_ref, route_ref, ybuf_ref, o_ref, rows_scr, sem):
    i = pl.program_id(0)
    n = pl.num_programs(0)
    tm = COMBINE_TM

    def gather_tile(tile, slot):
        base = tile * tm

        def body(t, carry):
            a = 2 * (base + t)
            pltpu.make_async_copy(ybuf_ref.at[pl.ds(dest_ref[a], 1)], rows_scr.at[slot, 0, pl.ds(t, 1)],
                                  sem.at[slot]).start()
            pltpu.make_async_copy(ybuf_ref.at[pl.ds(dest_ref[a + 1], 1)], rows_scr.at[slot, 1, pl.ds(t, 1)],
                                  sem.at[slot]).start()
            return carry

        lax.fori_loop(0, tm, body, 0, unroll=DMA_UNROLL)

    slot = i % 2

    @pl.when(i == 0)
    def _():
        gather_tile(0, 0)

    @pl.when(i + 1 < n)
    def _():
        gather_tile(i + 1, 1 - slot)

    for k in range(2):
        pltpu.make_async_copy(ybuf_ref.at[pl.ds(0, tm)], rows_scr.at[slot, k], sem.at[slot]).wait()

    o_ref[...] = x2_ref[...] + route_ref[:, 2:3] * rows_scr[slot, 0] + route_ref[:, 3:4] * rows_scr[slot, 1]


def _combine(dest, x2, route, ybuf):
    T = x2.shape[0]
    tm = COMBINE_TM
    grid_spec = pltpu.PrefetchScalarGridSpec(
        num_scalar_prefetch=1,
        grid=(T // tm,),
        in_specs=[
            pl.BlockSpec((tm, D_MODEL), lambda i, d: (i, 0)),
            pl.BlockSpec((tm, 128), lambda i, d: (i, 0)),
            pl.BlockSpec(memory_space=pl.ANY),
        ],
        out_specs=pl.BlockSpec((tm, D_MODEL), lambda i, d: (i, 0)),
        scratch_shapes=[pltpu.VMEM((2, 2, tm, D_MODEL), F32), pltpu.SemaphoreType.DMA((2,))],
    )
    return pl.pallas_call(
        _combine_kernel,
        grid_spec=grid_spec,
        out_shape=jax.ShapeDtypeStruct((T, D_MODEL), F32),
        compiler_params=_params(("arbitrary",), 48),
        name="combine",
    )(dest, x2, route, ybuf)


def _dest_kernel(route_ref, base_ref, o_ref):
    r = route_ref[...]
    lane = lax.broadcasted_iota(I32, r.shape, 1).astype(F32)
    base = base_ref[...]
    d1 = jnp.sum(jnp.where(lane == r[:, 0:1], base, 0.0), axis=-1, keepdims=True) + r[:, 4:5]
    d2 = jnp.sum(jnp.where(lane == r[:, 1:2], base, 0.0), axis=-1, keepdims=True) + r[:, 5:6]
    o_ref[...] = jnp.where(lane == 0, d1, jnp.where(lane == 1, d2, 0.0))


def _dest(route, base):
    T = route.shape[0]
    tm = DEST_TM
    return pl.pallas_call(
        _dest_kernel,
        grid=(T // tm,),
        in_specs=[pl.BlockSpec((tm, 128), lambda i: (i, 0)), pl.BlockSpec((1, 128), lambda i: (0, 0))],
        out_specs=pl.BlockSpec((tm, 128), lambda i: (i, 0)),
        out_shape=jax.ShapeDtypeStruct((T, 128), F32),
        compiler_params=_params(("parallel",), 32),
        name="dest",
    )(route, base)


def _moe_plan(route, counts, max_items):
    counts = counts[0, :N_EXPERTS].astype(I32)
    n_it = (counts + ITEM_ROWS - 1) // ITEM_ROWS
    it_end = jnp.cumsum(n_it)
    it_start = it_end - n_it
    base = jnp.zeros((1, 128), F32).at[0, :N_EXPERTS].set((it_start * ITEM_ROWS).astype(F32))
    dest = _dest(route, base)[:, :2].astype(I32).reshape(-1)
    n_items = it_end[-1]
    w = jnp.arange(max_items + 3, dtype=I32)
    ie = jnp.minimum(jnp.sum((it_end[None, :] <= w[:, None]).astype(I32), axis=1), N_EXPERTS - 1)
    used = w < n_items
    rows = jnp.where(used, jnp.minimum(ITEM_ROWS, counts[ie] - (w - it_start[ie]) * ITEM_ROWS), 0)
    ie = jnp.where(used, ie, ie[jnp.maximum(n_items - 1, 0)])
    return dest.astype(I32), ie.astype(I32), rows.astype(I32)


def _layer(x, mem, norm1_g, w_in, q_norm_g, k_norm_g, conv_w, mem_norm_g, w_mem_kv, mem_q_norm_g, mem_k_norm_g,
           out_norm_g, w_out, norm2_g, w_rg, b_rg, w_re, b_re, w_gate, w_up, w_down):
    B, S, D = x.shape
    T = B * S
    x2d = x.reshape(T, D)
    row = lambda v: v.reshape(1, -1).astype(F32)

    kvm = _memkv(mem.reshape(B * N_MEM, D), row(mem_norm_g), w_mem_kv, row(mem_k_norm_g))
    qkv, rest = _inproj(x2d, row(norm1_g), w_in.astype(BF16), row(q_norm_g), row(k_norm_g), row(mem_q_norm_g))

    slopes = 2.0 ** (-8.0 * jnp.arange(1, N_ATTN_HEADS + 1, dtype=F32) / N_ATTN_HEADS)
    attn = _attention(qkv, slopes, B, S)

    w_router = jnp.zeros((D, 128), F32).at[:, :N_GROUPS].set(w_rg).at[:, N_GROUPS:N_GROUPS + N_EXPERTS].set(w_re)
    b_router = jnp.zeros((1, 128), F32).at[0, :N_GROUPS].set(b_rg).at[0, N_GROUPS:N_GROUPS + N_EXPERTS].set(b_re)
    x2, h2, route, counts = _mixer(attn, rest, kvm, x2d, w_out.astype(BF16), row(out_norm_g), row(norm2_g),
                                   conv_w.astype(F32), w_router.astype(BF16), b_router, S)

    max_items = N_EXPERTS + (2 * T) // ITEM_ROWS
    dest, item_expert, item_rows = _moe_plan(route, counts, max_items)
    xbuf = _dispatch(dest, h2, max_items * ITEM_ROWS)
    ybuf = _ffn(item_expert, item_rows, xbuf, w_gate, w_up, w_down)
    out = _combine(dest, x2, route, ybuf)
    return out.reshape(B, S, D)


def kernel(x, mem, norm1_g, w_in, q_norm_g, k_norm_g, conv_w, mem_norm_g, w_mem_kv, mem_q_norm_g, mem_k_norm_g,
           out_norm_g, w_out, norm2_g, w_router_group, b_router_group, w_router_expert, b_router_expert,
           w_gate, w_up, w_down):
    for l in range(norm1_g.shape[0]):
        x = _layer(x, mem, norm1_g[l], w_in[l], q_norm_g[l], k_norm_g[l], conv_w[l], mem_norm_g[l], w_mem_kv[l],
                   mem_q_norm_g[l], mem_k_norm_g[l], out_norm_g[l], w_out[l], norm2_g[l],
                   w_router_group[l], b_router_group[l], w_router_expert[l], b_router_expert[l],
                   w_gate[l], w_up[l], w_down[l])
    return x
```

```python
import functools

import jax
import jax.numpy as jnp
from jax import lax
from jax.experimental import pallas as pl
from jax.experimental.pallas import tpu as pltpu

F32 = jnp.float32
BF16 = jnp.bfloat16
I32 = jnp.int32

D_MODEL = 2048
HEAD_DIM = 128
N_ATTN_HEADS = 8
ATTN_W = N_ATTN_HEADS * HEAD_DIM
N_MEM_HEADS = 4
MEM_W = N_MEM_HEADS * HEAD_DIM
CONV_W = D_MODEL - ATTN_W - MEM_W
CONV_GROUPS = CONV_W // HEAD_DIM
N_MEM = 256
DILATIONS = (1, 4, 16)
SPAN = 128
PROJ_W = 3 * ATTN_W + 3 * CONV_W + MEM_W
N_GROUPS = 8
EXPERTS_PER_GROUP = 8
N_EXPERTS = N_GROUPS * EXPERTS_PER_GROUP
D_EXPERT = D_MODEL // 2
EPS = 1e-6
NEG_INF = -1e30
QK_SCALE = HEAD_DIM ** -0.5
LOG2E = 1.4426950408889634

MIB = 1024 * 1024
INPROJ_TM = 512
INPROJ_TN = 1024
ATTN_CHUNK = 2048
MIXER_TM = 512
MIXER_SPLIT = 2
DEST_TM = 2048
DISPATCH_TM = 1024
COMBINE_TM = 512
ITEM_ROWS = 1152
FFN_PIECES = ((0, 256), (256, 256), (512, 256), (768, 256), (1024, 64), (1088, 64))
assert FFN_PIECES[-1][0] + FFN_PIECES[-1][1] == ITEM_ROWS
X_PIECES_AT_STEP = {1: (0,), 2: (1, 2), 3: (3, 4, 5)}
Y_PIECES_AT_STEP = {0: (0, 1, 2), 1: (3, 4, 5)}
FFN_CHUNK = 256
HALO = 16
DMA_UNROLL = 8
ATTN_UNROLL = 16


def _params(semantics, vmem_mib):
    return pltpu.CompilerParams(dimension_semantics=semantics, vmem_limit_bytes=vmem_mib * MIB)


def _rms(v, width):
    return lax.rsqrt(jnp.sum(v * v, axis=-1, keepdims=True) * (1.0 / width) + EPS)


def _memkv_kernel(mem_ref, g_ref, w_ref, kg_ref, o_ref):
    j = pl.program_id(0)
    m = mem_ref[...]
    h = (m * _rms(m, D_MODEL) * g_ref[...]).astype(BF16)
    kv = jnp.dot(h, w_ref[...].astype(BF16), preferred_element_type=F32)
    is_key = j == 0
    for hh in range(N_MEM_HEADS):
        t = kv[:, hh * HEAD_DIM:(hh + 1) * HEAD_DIM]
        tn = t * _rms(t, HEAD_DIM) * kg_ref[...]
        o_ref[hh] = jnp.where(is_key, tn, t).astype(BF16)


def _memkv(mem2d, mem_norm_g, w_mem_kv, mem_k_norm_g):
    rows = mem2d.shape[0]
    return pl.pallas_call(
        _memkv_kernel,
        grid=(2,),
        in_specs=[
            pl.BlockSpec((rows, D_MODEL), lambda j: (0, 0)),
            pl.BlockSpec((1, D_MODEL), lambda j: (0, 0)),
            pl.BlockSpec((D_MODEL, MEM_W), lambda j: (0, j)),
            pl.BlockSpec((1, HEAD_DIM), lambda j: (0, 0)),
        ],
        out_specs=pl.BlockSpec((N_MEM_HEADS, rows, HEAD_DIM), lambda j: (j, 0, 0)),
        out_shape=jax.ShapeDtypeStruct((2 * N_MEM_HEADS, rows, HEAD_DIM), BF16),
        compiler_params=_params(("arbitrary",), 40),
        name="memkv",
    )(mem2d, mem_norm_g, w_mem_kv, mem_k_norm_g)


def _inproj_kernel(x_ref, g1_ref, w_ref, qg_ref, kg_ref, mqg_ref, qkv_ref, rest_ref):
    x = x_ref[...]
    h = (x * _rms(x, D_MODEL) * g1_ref[...]).astype(BF16)

    def normed(t, g):
        return t * _rms(t, HEAD_DIM) * g

    gq = qg_ref[...] * (QK_SCALE * LOG2E)
    gk = kg_ref[...]
    gmq = mqg_ref[...] * QK_SCALE
    groups_per_tile = INPROJ_TN // HEAD_DIM
    for j in range(PROJ_W // INPROJ_TN):
        y = jnp.dot(h, w_ref[:, j * INPROJ_TN:(j + 1) * INPROJ_TN], preferred_element_type=F32)
        for hh in range(groups_per_tile):
            t = y[:, hh * HEAD_DIM:(hh + 1) * HEAD_DIM]
            col = j * groups_per_tile + hh
            if col < 8:
                qkv_ref[col] = normed(t, gq)
            elif col < 16:
                qkv_ref[col] = normed(t, gk)
            elif col < 24:
                qkv_ref[col] = t
            elif col < 36:
                rest_ref[col - 24] = t.astype(BF16)
            else:
                rest_ref[col - 24] = normed(t, gmq).astype(BF16)


def _inproj(x2d, norm1_g, w_in_bf16, q_norm_g, k_norm_g, mem_q_norm_g):
    T = x2d.shape[0]
    tm = INPROJ_TM
    small = lambda n: pl.BlockSpec((1, n), lambda i: (0, 0))
    return pl.pallas_call(
        _inproj_kernel,
        grid=(T // tm,),
        in_specs=[
            pl.BlockSpec((tm, D_MODEL), lambda i: (i, 0)),
            small(D_MODEL),
            pl.BlockSpec((D_MODEL, PROJ_W), lambda i: (0, 0), pipeline_mode=pl.Buffered(1)),
            small(HEAD_DIM), small(HEAD_DIM), small(HEAD_DIM),
        ],
        out_specs=[
            pl.BlockSpec((24, tm, HEAD_DIM), lambda i: (0, i, 0)),
            pl.BlockSpec((16, tm, HEAD_DIM), lambda i: (0, i, 0)),
        ],
        out_shape=[
            jax.ShapeDtypeStruct((24, T, HEAD_DIM), F32),
            jax.ShapeDtypeStruct((16, T, HEAD_DIM), BF16),
        ],
        compiler_params=_params(("parallel",), 52),
        name="inproj",
    )(x2d, norm1_g, w_in_bf16, q_norm_g, k_norm_g, mem_q_norm_g)


def _attn_kernel(slopes_ref, q_ref, k_ref, v_ref, o_ref,
                 kx, vx, q4, k4, v4, o1, l1, o4, l4, o16, l16, out_scr, bias_scr):
    h = pl.program_id(1)
    c = pl.program_id(2)
    C = ATTN_CHUNK

    @pl.when(c == 0)
    def _():
        kx[0:C, :] = jnp.zeros((C, HEAD_DIM), F32)
        vx[0:C, :] = jnp.zeros((C, HEAD_DIM), F32)

    @pl.when(c > 0)
    def _():
        kx[0:C, :] = kx[C:2 * C, :]
        vx[0:C, :] = vx[C:2 * C, :]

    kx[C:2 * C, :] = k_ref[0]
    vx[C:2 * C, :] = v_ref[0]

    for b in range(4):
        q4[b] = q_ref[0, pl.ds(b, C // 4, stride=4), :]
        k4[b] = kx[pl.ds(b, 2 * C // 4, stride=4), :]
        v4[b] = vx[pl.ds(b, 2 * C // 4, stride=4), :]

    slope = slopes_ref[h]
    ii = lax.broadcasted_iota(I32, (SPAN, 2 * SPAN), 0)
    jj = lax.broadcasted_iota(I32, (SPAN, 2 * SPAN), 1)
    dist = ii - jj + SPAN
    band = (dist >= 0) & (dist <= SPAN)
    distf = dist.astype(F32)

    for bi, d in enumerate(DILATIONS):
        neg = (-slope * float(d) * LOG2E) * distf
        bias_scr[2 * bi] = jnp.where(band, neg, NEG_INF)
        bias_scr[2 * bi + 1] = jnp.where(band & (jj >= SPAN), neg, NEG_INF)

    ones = jnp.ones((2 * SPAN, HEAD_DIM), BF16)

    def attend(q, k, v, bias):
        s = lax.dot_general(q.astype(BF16), k.astype(BF16), (((1,), (1,)), ((), ())),
                            preferred_element_type=F32) + bias
        m = jnp.max(s, axis=-1, keepdims=True)
        p = jnp.exp2(s - m).astype(BF16)
        ol = jnp.dot(p, jnp.concatenate([v.astype(BF16), ones], axis=1), preferred_element_type=F32)
        l = ol[:, HEAD_DIM:]
        return ol[:, :HEAD_DIM] * (1.0 / l), m + jnp.log2(l)

    def unit1(u, carry):
        sq = pl.multiple_of(u * SPAN, SPAN)
        sk = pl.multiple_of(C + u * SPAN - SPAN, SPAN)
        first = jnp.logical_and(c == 0, u == 0).astype(I32)
        o, lse = attend(q_ref[0, pl.ds(sq, SPAN), :], kx[pl.ds(sk, 2 * SPAN), :], vx[pl.ds(sk, 2 * SPAN), :],
                        bias_scr[first])
        o1[pl.ds(sq, SPAN), :] = o
        l1[pl.ds(sq, SPAN), :] = lse
        return carry

    def unit4(u, carry):
        b = u % 4
        qb = u // 4
        sq = pl.multiple_of(qb * SPAN, SPAN)
        sk = pl.multiple_of(C // 4 + qb * SPAN - SPAN, SPAN)
        first = jnp.logical_and(c == 0, qb == 0).astype(I32)
        o, lse = attend(q4[b, pl.ds(sq, SPAN), :], k4[b, pl.ds(sk, 2 * SPAN), :], v4[b, pl.ds(sk, 2 * SPAN), :],
                        bias_scr[2 + first])
        o4[b, pl.ds(sq, SPAN), :] = o
        l4[b, pl.ds(sq, SPAN), :] = lse
        return carry

    def unit16(u, carry):
        b = u % 4
        a = u // 4
        first = (c == 0).astype(I32)
        qi = pl.ds(a, SPAN, stride=4)
        ki = pl.ds(a, 2 * SPAN, stride=4)
        o, lse = attend(q4[b, qi, :], k4[b, ki, :], v4[b, ki, :], bias_scr[4 + first])
        o16[b, qi, :] = o
        l16[b, qi, :] = lse
        return carry

    for unit in (unit1, unit4, unit16):
        lax.fori_loop(0, C // SPAN, unit, 0, unroll=ATTN_UNROLL)

    for b in range(4):
        rows_b = pl.ds(b, C // 4, stride=4)
        la, lb, lc = l1[rows_b, :], l4[b], l16[b]
        mx = jnp.maximum(jnp.maximum(la, lb), lc)
        ea, eb, ec = jnp.exp2(la - mx), jnp.exp2(lb - mx), jnp.exp2(lc - mx)
        out_scr[rows_b, :] = (ea * o1[rows_b, :] + eb * o4[b] + ec * o16[b]) * (1.0 / (ea + eb + ec))
    o_ref[0] = out_scr[...].astype(BF16)


def _attention(qkv, slopes, batch, seq):
    T = qkv.shape[1]
    C = ATTN_CHUNK
    nc = seq // C
    blk = lambda off: pl.BlockSpec((1, C, HEAD_DIM), lambda b, h, c, s: (h + off, b * nc + c, 0))
    grid_spec = pltpu.PrefetchScalarGridSpec(
        num_scalar_prefetch=1,
        grid=(batch, N_ATTN_HEADS, nc),
        in_specs=[blk(0), blk(8), blk(16)],
        out_specs=pl.BlockSpec((1, C, HEAD_DIM), lambda b, h, c, s: (h, b * nc + c, 0)),
        scratch_shapes=[
            pltpu.VMEM((2 * C, HEAD_DIM), F32),
            pltpu.VMEM((2 * C, HEAD_DIM), F32),
            pltpu.VMEM((4, C // 4, HEAD_DIM), F32),
            pltpu.VMEM((4, 2 * C // 4, HEAD_DIM), F32),
            pltpu.VMEM((4, 2 * C // 4, HEAD_DIM), F32),
            pltpu.VMEM((C, HEAD_DIM), F32),
            pltpu.VMEM((C, HEAD_DIM), F32),
            pltpu.VMEM((4, C // 4, HEAD_DIM), F32),
            pltpu.VMEM((4, C // 4, HEAD_DIM), F32),
            pltpu.VMEM((4, C // 4, HEAD_DIM), F32),
            pltpu.VMEM((4, C // 4, HEAD_DIM), F32),
            pltpu.VMEM((C, HEAD_DIM), F32),
            pltpu.VMEM((2 * len(DILATIONS), SPAN, 2 * SPAN), F32),
        ],
    )
    return pl.pallas_call(
        _attn_kernel,
        grid_spec=grid_spec,
        out_shape=jax.ShapeDtypeStruct((N_ATTN_HEADS, T, HEAD_DIM), BF16),
        compiler_params=_params(("parallel", "parallel", "arbitrary"), 40),
        name="attn",
    )(slopes, qkv, qkv, qkv)


def _mixer_kernel(attn_ref, rest_ref, halo_ref, kvm_ref, x_ref, wout_ref, og_ref, g2_ref, cw_ref, wr_ref, br_ref,
                  x2_ref, h2_ref, route_ref, counts_ref, y_scr, z_scr, cnt_scr, *, tiles_per_seq):
    i = pl.program_id(0)
    tm = MIXER_TM
    og = og_ref[...]

    @pl.when(i == 0)
    def _():
        cnt_scr[...] = jnp.zeros((1, 128), F32)

    seq_start = (i % tiles_per_seq) == 0
    for g in range(CONV_GROUPS):
        z = rest_ref[CONV_GROUPS + g].astype(F32) * rest_ref[2 * CONV_GROUPS + g].astype(F32)
        zh = halo_ref[CONV_GROUPS + g].astype(F32) * halo_ref[2 * CONV_GROUPS + g].astype(F32)
        z_scr[g, 0:HALO, :] = jnp.where(seq_start, 0.0, zh)
        z_scr[g, HALO:HALO + tm, :] = z

    nb = tm // MIXER_SPLIT
    def mix(blk):
        r0 = blk * nb
        rows = slice(r0, r0 + nb)

        ss = jnp.zeros((nb, 1), F32)
        for hh in range(N_ATTN_HEADS):
            a = attn_ref[hh, rows, :].astype(F32)
            ss = ss + jnp.sum(a * a, axis=-1, keepdims=True)
        r = lax.rsqrt(ss * (1.0 / ATTN_W) + EPS)
        for hh in range(N_ATTN_HEADS):
            lo = hh * HEAD_DIM
            y_scr[rows, lo:lo + HEAD_DIM] = (attn_ref[hh, rows, :].astype(F32) * r * og[:, lo:lo + HEAD_DIM]).astype(BF16)

        convs = []
        ss = jnp.zeros((nb, 1), F32)
        for g in range(CONV_GROUPS):
            bg = rest_ref[g, rows, :].astype(F32)
            z0 = z_scr[g, HALO + r0:HALO + r0 + nb, :]
            z1 = z_scr[g, HALO - 1 + r0:HALO - 1 + r0 + nb, :]
            z2 = z_scr[g, HALO - 2 + r0:HALO - 2 + r0 + nb, :]
            lo = g * HEAD_DIM
            w0 = cw_ref[0:1, lo:lo + HEAD_DIM]
            w1 = cw_ref[1:2, lo:lo + HEAD_DIM]
            w2 = cw_ref[2:3, lo:lo + HEAD_DIM]
            cv = bg * (w2 * z0 + w1 * z1 + w0 * z2)
            convs.append(cv)
            ss = ss + jnp.sum(cv * cv, axis=-1, keepdims=True)
        r = lax.rsqrt(ss * (1.0 / CONV_W) + EPS)
        for g in range(CONV_GROUPS):
            lo = ATTN_W + g * HEAD_DIM
            y_scr[rows, lo:lo + HEAD_DIM] = (convs[g] * r * og[:, lo:lo + HEAD_DIM]).astype(BF16)

        mems = []
        ss = jnp.zeros((nb, 1), F32)
        for hh in range(N_MEM_HEADS):
            mq = rest_ref[3 * CONV_GROUPS + hh, rows, :]
            s = lax.dot_general(mq, kvm_ref[hh], (((1,), (1,)), ((), ())), preferred_element_type=F32)
            m = jnp.max(s, axis=-1, keepdims=True)
            p = jnp.exp(s - m)
            l = jnp.sum(p, axis=-1, keepdims=True)
            o = jnp.dot(p.astype(BF16), kvm_ref[N_MEM_HEADS + hh], preferred_element_type=F32) * (1.0 / l)
            mems.append(o)
            ss = ss + jnp.sum(o * o, axis=-1, keepdims=True)
        r = lax.rsqrt(ss * (1.0 / MEM_W) + EPS)
        for hh in range(N_MEM_HEADS):
            lo = ATTN_W + CONV_W + hh * HEAD_DIM
            y_scr[rows, lo:lo + HEAD_DIM] = (mems[hh] * r * og[:, lo:lo + HEAD_DIM]).astype(BF16)

    def project(blk):
        rows = slice(blk * nb, (blk + 1) * nb)
        x2 = x_ref[rows, :] + jnp.dot(y_scr[rows, :], wout_ref[...], preferred_element_type=F32)
        x2_ref[rows, :] = x2
        h2 = x2 * _rms(x2, D_MODEL) * g2_ref[...]
        h2_ref[rows, :] = h2
        return jnp.dot(h2.astype(BF16), wr_ref[...], preferred_element_type=F32) + br_ref[...]

    def route_rows(blk, lg):
        rows = slice(blk * nb, (blk + 1) * nb)
        lane = lax.broadcasted_iota(I32, (nb, 128), 1).astype(F32)
        gl = jnp.where(lane < N_GROUPS, lg, NEG_INF)
        gmax = jnp.max(gl, axis=-1, keepdims=True)
        g_w = 1.0 / jnp.sum(jnp.exp(gl - gmax), axis=-1, keepdims=True)
        g_idx = jnp.min(jnp.where(gl == gmax, lane, 1e9), axis=-1, keepdims=True)
        e_lo = N_GROUPS + EXPERTS_PER_GROUP * g_idx
        el = jnp.where((lane >= e_lo) & (lane < e_lo + EXPERTS_PER_GROUP), lg, NEG_INF)
        e1 = jnp.max(el, axis=-1, keepdims=True)
        i1 = jnp.min(jnp.where(el == e1, lane, 1e9), axis=-1, keepdims=True)
        el2 = jnp.where(lane == i1, NEG_INF, el)
        e2 = jnp.max(el2, axis=-1, keepdims=True)
        i2 = jnp.min(jnp.where(el2 == e2, lane, 1e9), axis=-1, keepdims=True)
        t = jnp.exp(e2 - e1)
        p1 = 1.0 / (1.0 + t)
        p2 = t * p1

        sel1 = i1 - N_GROUPS
        sel2 = i2 - N_GROUPS
        hot1 = lane == sel1
        hot2 = lane == sel2
        hot = jnp.where(hot1 | hot2, 1.0, 0.0)
        row_id = lax.broadcasted_iota(I32, (nb, nb), 0)
        col_id = lax.broadcasted_iota(I32, (nb, nb), 1)
        earlier = jnp.where(col_id < row_id, 1.0, 0.0).astype(BF16)
        before = jnp.dot(earlier, hot.astype(BF16), preferred_element_type=F32) + cnt_scr[...]
        rank1 = jnp.sum(jnp.where(hot1, before, 0.0), axis=-1, keepdims=True)
        rank2 = jnp.sum(jnp.where(hot2, before, 0.0), axis=-1, keepdims=True)
        cnt_scr[...] = cnt_scr[...] + jnp.sum(hot, axis=0, keepdims=True)

        route_ref[rows, :] = jnp.where(lane == 0, sel1,
                                       jnp.where(lane == 1, sel2,
                                                 jnp.where(lane == 2, g_w * p1,
                                                           jnp.where(lane == 3, g_w * p2,
                                                                     jnp.where(lane == 4, rank1,
                                                                               jnp.where(lane == 5, rank2, 0.0))))))

    mix(0)
    for blk in range(MIXER_SPLIT):
        lg = project(blk)
        if blk + 1 < MIXER_SPLIT:
            mix(blk + 1)
        route_rows(blk, lg)

    counts_ref[...] = cnt_scr[...]


def _mixer(attn, rest, kvm, x2d, w_out_bf16, out_norm_g, norm2_g, conv_w, w_router, b_router, seq):
    T = x2d.shape[0]
    tm = MIXER_TM
    tiles_per_seq = seq // tm
    full = lambda shape: pl.BlockSpec(shape, lambda i: tuple(0 for _ in shape))
    return pl.pallas_call(
        functools.partial(_mixer_kernel, tiles_per_seq=tiles_per_seq),
        grid=(T // tm,),
        in_specs=[
            pl.BlockSpec((N_ATTN_HEADS, tm, HEAD_DIM), lambda i: (0, i, 0)),
            pl.BlockSpec((16, tm, HEAD_DIM), lambda i: (0, i, 0)),
            pl.BlockSpec((16, HALO, HEAD_DIM), lambda i: (0, jnp.maximum(i * (tm // HALO) - 1, 0), 0)),
            pl.BlockSpec((2 * N_MEM_HEADS, N_MEM, HEAD_DIM), lambda i: (0, i // tiles_per_seq, 0)),
            pl.BlockSpec((tm, D_MODEL), lambda i: (i, 0)),
            pl.BlockSpec((D_MODEL, D_MODEL), lambda i: (0, 0), pipeline_mode=pl.Buffered(1)),
            full((1, D_MODEL)),
            full((1, D_MODEL)),
            full((3, CONV_W)),
            full((D_MODEL, 128)),
            full((1, 128)),
        ],
        out_specs=[
            pl.BlockSpec((tm, D_MODEL), lambda i: (i, 0)),
            pl.BlockSpec((tm, D_MODEL), lambda i: (i, 0)),
            pl.BlockSpec((tm, 128), lambda i: (i, 0)),
            pl.BlockSpec((1, 128), lambda i: (0, 0)),
        ],
        out_shape=[
            jax.ShapeDtypeStruct((T, D_MODEL), F32),
            jax.ShapeDtypeStruct((T, D_MODEL), F32),
            jax.ShapeDtypeStruct((T, 128), F32),
            jax.ShapeDtypeStruct((1, 128), F32),
        ],
        scratch_shapes=[
            pltpu.VMEM((tm, D_MODEL), BF16),
            pltpu.VMEM((CONV_GROUPS, HALO + tm, HEAD_DIM), F32),
            pltpu.VMEM((1, 128), F32),
        ],
        compiler_params=_params(("arbitrary",), 56),
        name="mixer",
    )(attn, rest, rest, kvm, x2d, w_out_bf16, out_norm_g, norm2_g, conv_w, w_router, b_router)


def _dispatch_kernel(dest_ref, h_ref, xbuf_ref, sem):
    i = pl.program_id(0)
    tm = DISPATCH_TM
    base = i * tm

    def body(t, carry):
        a = 2 * (base + t)
        src = h_ref.at[pl.ds(t, 1)]
        pltpu.make_async_copy(src, xbuf_ref.at[pl.ds(dest_ref[a], 1)], sem).start()
        pltpu.make_async_copy(src, xbuf_ref.at[pl.ds(dest_ref[a + 1], 1)], sem).start()
        return carry

    lax.fori_loop(0, tm, body, 0, unroll=DMA_UNROLL)
    pltpu.make_async_copy(xbuf_ref.at[pl.ds(0, 2 * tm)], xbuf_ref.at[pl.ds(0, 2 * tm)], sem).wait()


def _dispatch(dest, h2, n_rows):
    T = h2.shape[0]
    tm = DISPATCH_TM
    grid_spec = pltpu.PrefetchScalarGridSpec(
        num_scalar_prefetch=1,
        grid=(T // tm,),
        in_specs=[pl.BlockSpec((tm, D_MODEL), lambda i, d: (i, 0))],
        out_specs=pl.BlockSpec(memory_space=pl.ANY),
        scratch_shapes=[pltpu.SemaphoreType.DMA],
    )
    return pl.pallas_call(
        _dispatch_kernel,
        grid_spec=grid_spec,
        out_shape=jax.ShapeDtypeStruct((n_rows, D_MODEL), F32),
        compiler_params=_params(("arbitrary",), 32),
        name="dispatch",
    )(dest, h2)


def _ffn_kernel(ie_ref, rows_ref, x_ref, wg_ref, wu_ref, wd_ref, y_ref, xg, xb, acc, xsem, ysem):
    w = pl.program_id(0)
    c = pl.program_id(1)

    def subs(v):
        return sum((v > lo).astype(I32) for lo, _ in FFN_PIECES)

    rows = rows_ref[w]
    ns = subs(rows)
    ns_next = subs(rows_ref[w + 1])
    ns_prev = jnp.where(w > 0, subs(rows_ref[jnp.maximum(w - 1, 0)]), 0)
    ns_prev2 = jnp.where(w > 1, subs(rows_ref[jnp.maximum(w - 2, 0)]), 0)
    p = w % 2

    def x_copy(item, s):
        lo, n = FFN_PIECES[s]
        return pltpu.make_async_copy(x_ref.at[pl.ds(item * ITEM_ROWS + lo, n)], xg.at[pl.ds(lo, n)], xsem)

    def y_copy(item, slot, s):
        lo, n = FFN_PIECES[s]
        return pltpu.make_async_copy(acc.at[slot, pl.ds(lo, n)], y_ref.at[pl.ds(item * ITEM_ROWS + lo, n)],
                                     ysem.at[slot])

    def for_pieces(pieces, count, fn):
        for s in pieces:
            @pl.when(s < count)
            def _(s=s):
                fn(s)

    all_pieces = range(len(FFN_PIECES))

    @pl.when(c == 0)
    def _():
        @pl.when(w == 0)
        def _():
            for_pieces(all_pieces, ns, lambda s: x_copy(0, s).start())

        for_pieces(all_pieces, ns, lambda s: x_copy(w, s).wait())
        for_pieces(all_pieces, ns_prev2, lambda s: y_copy(w - 2, p, s).wait())

    for step, pieces in Y_PIECES_AT_STEP.items():
        @pl.when(c == step)
        def _(pieces=pieces):
            for_pieces(pieces, ns_prev, lambda s: y_copy(w - 1, 1 - p, s).start())

    for step, pieces in X_PIECES_AT_STEP.items():
        @pl.when(c == step)
        def _(pieces=pieces):
            for_pieces(pieces, ns_next, lambda s: x_copy(w + 1, s).start())

    for ns_static in range(1, len(FFN_PIECES) + 1):
        nr = FFN_PIECES[ns_static - 1][0] + FFN_PIECES[ns_static - 1][1]

        @pl.when(ns == ns_static)
        def _(nr=nr):
            @pl.when(c == 0)
            def _():
                ok = lax.broadcasted_iota(I32, (nr, 1), 0) < rows
                xb[0:nr, :] = jnp.where(ok, xg[0:nr, :], 0.0).astype(BF16)
                acc[p, 0:nr, :] = jnp.zeros((nr, D_MODEL), F32)

            wg = wg_ref[0].astype(BF16)
            wu = wu_ref[0].astype(BF16)
            wd = wd_ref[0].astype(BF16)
            x = xb[0:nr, :]
            a = jnp.dot(x, wg, preferred_element_type=F32)
            u = jnp.dot(x, wu, preferred_element_type=F32)
            hm = (a / (1.0 + jnp.exp(-a)) * u).astype(BF16)
            acc[p, 0:nr, :] += jnp.dot(hm, wd, preferred_element_type=F32)

def _ffn(item_expert, item_rows, xbuf, w_gate, w_up, w_down):
    n_chunks = D_EXPERT // FFN_CHUNK
    grid_items = item_rows.shape[0] - 1

    def chunk_of(w, c, ir):
        return jnp.where(ir[w] > 0, c, n_chunks - 1)

    grid_spec = pltpu.PrefetchScalarGridSpec(
        num_scalar_prefetch=2,
        grid=(grid_items, n_chunks),
        in_specs=[
            pl.BlockSpec(memory_space=pl.ANY),
            pl.BlockSpec((1, D_MODEL, FFN_CHUNK), lambda w, c, ie, ir: (ie[w], 0, chunk_of(w, c, ir))),
            pl.BlockSpec((1, D_MODEL, FFN_CHUNK), lambda w, c, ie, ir: (ie[w], 0, chunk_of(w, c, ir))),
            pl.BlockSpec((1, FFN_CHUNK, D_MODEL), lambda w, c, ie, ir: (ie[w], chunk_of(w, c, ir), 0)),
        ],
        out_specs=pl.BlockSpec(memory_space=pl.ANY),
        scratch_shapes=[
            pltpu.VMEM((ITEM_ROWS, D_MODEL), F32),
            pltpu.VMEM((ITEM_ROWS, D_MODEL), BF16),
            pltpu.VMEM((2, ITEM_ROWS, D_MODEL), F32),
            pltpu.SemaphoreType.DMA,
            pltpu.SemaphoreType.DMA((2,)),
        ],
    )
    return pl.pallas_call(
        _ffn_kernel,
        grid_spec=grid_spec,
        out_shape=jax.ShapeDtypeStruct(xbuf.shape, F32),
        compiler_params=_params(("arbitrary", "arbitrary"), 58),
        name="ffn",
    )(item_expert, item_rows, xbuf, w_gate, w_up, w_down)


def _combine_kernel(dest_ref, x2_ref, route_ref, ybuf_ref, o_ref, rows_scr, sem):
    i = pl.program_id(0)
    n = pl.num_programs(0)
    tm = COMBINE_TM

    def gather_tile(tile, slot):
        base = tile * tm

        def body(t, carry):
            a = 2 * (base + t)
            pltpu.make_async_copy(ybuf_ref.at[pl.ds(dest_ref[a], 1)], rows_scr.at[slot, 0, pl.ds(t, 1)],
                                  sem.at[slot]).start()
            pltpu.make_async_copy(ybuf_ref.at[pl.ds(dest_ref[a + 1], 1)], rows_scr.at[slot, 1, pl.ds(t, 1)],
                                  sem.at[slot]).start()
            return carry

        lax.fori_loop(0, tm, body, 0, unroll=DMA_UNROLL)

    slot = i % 2

    @pl.when(i == 0)
    def _():
        gather_tile(0, 0)

    @pl.when(i + 1 < n)
    def _():
        gather_tile(i + 1, 1 - slot)

    for k in range(2):
        pltpu.make_async_copy(ybuf_ref.at[pl.ds(0, tm)], rows_scr.at[slot, k], sem.at[slot]).wait()

    o_ref[...] = x2_ref[...] + route_ref[:, 2:3] * rows_scr[slot, 0] + route_ref[:, 3:4] * rows_scr[slot, 1]


def _combine(dest, x2, route, ybuf):
    T = x2.shape[0]
    tm = COMBINE_TM
    grid_spec = pltpu.PrefetchScalarGridSpec(
        num_scalar_prefetch=1,
        grid=(T // tm,),
        in_specs=[
            pl.BlockSpec((tm, D_MODEL), lambda i, d: (i, 0)),
            pl.BlockSpec((tm, 128), lambda i, d: (i, 0)),
            pl.BlockSpec(memory_space=pl.ANY),
        ],
        out_specs=pl.BlockSpec((tm, D_MODEL), lambda i, d: (i, 0)),
        scratch_shapes=[pltpu.VMEM((2, 2, tm, D_MODEL), F32), pltpu.SemaphoreType.DMA((2,))],
    )
    return pl.pallas_call(
        _combine_kernel,
        grid_spec=grid_spec,
        out_shape=jax.ShapeDtypeStruct((T, D_MODEL), F32),
        compiler_params=_params(("arbitrary",), 48),
        name="combine",
    )(dest, x2, route, ybuf)


def _dest_kernel(route_ref, base_ref, o_ref):
    r = route_ref[...]
    lane = lax.broadcasted_iota(I32, r.shape, 1).astype(F32)
    base = base_ref[...]
    d1 = jnp.sum(jnp.where(lane == r[:, 0:1], base, 0.0), axis=-1, keepdims=True) + r[:, 4:5]
    d2 = jnp.sum(jnp.where(lane == r[:, 1:2], base, 0.0), axis=-1, keepdims=True) + r[:, 5:6]
    o_ref[...] = jnp.where(lane == 0, d1, jnp.where(lane == 1, d2, 0.0))


def _dest(route, base):
    T = route.shape[0]
    tm = DEST_TM
    return pl.pallas_call(
        _dest_kernel,
        grid=(T // tm,),
        in_specs=[pl.BlockSpec((tm, 128), lambda i: (i, 0)), pl.BlockSpec((1, 128), lambda i: (0, 0))],
        out_specs=pl.BlockSpec((tm, 128), lambda i: (i, 0)),
        out_shape=jax.ShapeDtypeStruct((T, 128), F32),
        compiler_params=_params(("parallel",), 32),
        name="dest",
    )(route, base)


def _moe_plan(route, counts, max_items):
    counts = counts[0, :N_EXPERTS].astype(I32)
    n_it = (counts + ITEM_ROWS - 1) // ITEM_ROWS
    it_end = jnp.cumsum(n_it)
    it_start = it_end - n_it
    base = jnp.zeros((1, 128), F32).at[0, :N_EXPERTS].set((it_start * ITEM_ROWS).astype(F32))
    dest = _dest(route, base)[:, :2].astype(I32).reshape(-1)
    n_items = it_end[-1]
    w = jnp.arange(max_items + 3, dtype=I32)
    ie = jnp.minimum(jnp.sum((it_end[None, :] <= w[:, None]).astype(I32), axis=1), N_EXPERTS - 1)
    used = w < n_items
    rows = jnp.where(used, jnp.minimum(ITEM_ROWS, counts[ie] - (w - it_start[ie]) * ITEM_ROWS), 0)
    ie = jnp.where(used, ie, ie[jnp.maximum(n_items - 1, 0)])
    return dest.astype(I32), ie.astype(I32), rows.astype(I32)


def _layer(x, mem, norm1_g, w_in, q_norm_g, k_norm_g, conv_w, mem_norm_g, w_mem_kv, mem_q_norm_g, mem_k_norm_g,
           out_norm_g, w_out, norm2_g, w_rg, b_rg, w_re, b_re, w_gate, w_up, w_down):
    B, S, D = x.shape
    T = B * S
    x2d = x.reshape(T, D)
    row = lambda v: v.reshape(1, -1).astype(F32)

    kvm = _memkv(mem.reshape(B * N_MEM, D), row(mem_norm_g), w_mem_kv, row(mem_k_norm_g))
    qkv, rest = _inproj(x2d, row(norm1_g), w_in.astype(BF16), row(q_norm_g), row(k_norm_g), row(mem_q_norm_g))

    slopes = 2.0 ** (-8.0 * jnp.arange(1, N_ATTN_HEADS + 1, dtype=F32) / N_ATTN_HEADS)
    attn = _attention(qkv, slopes, B, S)

    w_router = jnp.zeros((D, 128), F32).at[:, :N_GROUPS].set(w_rg).at[:, N_GROUPS:N_GROUPS + N_EXPERTS].set(w_re)
    b_router = jnp.zeros((1, 128), F32).at[0, :N_GROUPS].set(b_rg).at[0, N_GROUPS:N_GROUPS + N_EXPERTS].set(b_re)
    x2, h2, route, counts = _mixer(attn, rest, kvm, x2d, w_out.astype(BF16), row(out_norm_g), row(norm2_g),
                                   conv_w.astype(F32), w_router.astype(BF16), b_router, S)

    max_items = N_EXPERTS + (2 * T) // ITEM_ROWS
    dest, item_expert, item_rows = _moe_plan(route, counts, max_items)
    xbuf = _dispatch(dest, h2, max_items * ITEM_ROWS)
    ybuf = _ffn(item_expert, item_rows, xbuf, w_gate, w_up, w_down)
    out = _combine(dest, x2, route, ybuf)
    return out.reshape(B, S, D)


def kernel(x, mem, norm1_g, w_in, q_norm_g, k_norm_g, conv_w, mem_norm_g, w_mem_kv, mem_q_norm_g, mem_k_norm_g,
           out_norm_g, w_out, norm2_g, w_router_group, b_router_group, w_router_expert, b_router_expert,
           w_gate, w_up, w_down):
    for l in range(norm1_g.shape[0]):
        x = _layer(x, mem, norm1_g[l], w_in[l], q_norm_g[l], k_norm_g[l], conv_w[l], mem_norm_g[l], w_mem_kv[l],
                   mem_q_norm_g[l], mem_k_norm_g[l], out_norm_g[l], w_out[l], norm2_g[l],
                   w_router_group[l], b_router_group[l], w_router_expert[l], b_router_expert[l],
                   w_gate[l], w_up[l], w_down[l])
    return x
```

```python
import functools

import jax
import jax.numpy as jnp
from jax import lax
from jax.experimental import pallas as pl
from jax.experimental.pallas import tpu as pltpu

F32 = jnp.float32
BF16 = jnp.bfloat16
I32 = jnp.int32

D_MODEL = 2048
HEAD_DIM = 128
N_ATTN_HEADS = 8
ATTN_W = N_ATTN_HEADS * HEAD_DIM
N_MEM_HEADS = 4
MEM_W = N_MEM_HEADS * HEAD_DIM
CONV_W = D_MODEL - ATTN_W - MEM_W
CONV_GROUPS = CONV_W // HEAD_DIM
N_MEM = 256
DILATIONS = (1, 4, 16)
SPAN = 128
PROJ_W = 3 * ATTN_W + 3 * CONV_W + MEM_W
N_GROUPS = 8
EXPERTS_PER_GROUP = 8
N_EXPERTS = N_GROUPS * EXPERTS_PER_GROUP
D_EXPERT = D_MODEL // 2
EPS = 1e-6
NEG_INF = -1e30
QK_SCALE = HEAD_DIM ** -0.5
LOG2E = 1.4426950408889634

MIB = 1024 * 1024
INPROJ_TM = 512
INPROJ_TN = 1024
ATTN_CHUNK = 2048
MIXER_TM = 512
MIXER_SPLIT = 2
DEST_TM = 2048
DISPATCH_TM = 1024
COMBINE_TM = 512
ITEM_ROWS = 1152
FFN_PIECES = ((0, 256), (256, 256), (512, 256), (768, 256), (1024, 64), (1088, 64))
assert FFN_PIECES[-1][0] + FFN_PIECES[-1][1] == ITEM_ROWS
X_PIECES_AT_STEP = {1: (0,), 2: (1, 2), 3: (3, 4, 5)}
Y_PIECES_AT_STEP = {0: (0, 1, 2), 1: (3, 4, 5)}
FFN_CHUNK = 256
HALO = 16
DMA_UNROLL = 8
ATTN_UNROLL = 16


def _params(semantics, vmem_mib):
    return pltpu.CompilerParams(dimension_semantics=semantics, vmem_limit_bytes=vmem_mib * MIB)


def _rms(v, width):
    return lax.rsqrt(jnp.sum(v * v, axis=-1, keepdims=True) * (1.0 / width) + EPS)


def _memkv_kernel(mem_ref, g_ref, w_ref, kg_ref, o_ref):
    j = pl.program_id(0)
    m = mem_ref[...]
    h = (m * _rms(m, D_MODEL) * g_ref[...]).astype(BF16)
    kv = jnp.dot(h, w_ref[...].astype(BF16), preferred_element_type=F32)
    is_key = j == 0
    for hh in range(N_MEM_HEADS):
        t = kv[:, hh * HEAD_DIM:(hh + 1) * HEAD_DIM]
        tn = t * _rms(t, HEAD_DIM) * kg_ref[...]
        o_ref[hh] = jnp.where(is_key, tn, t).astype(BF16)


def _memkv(mem2d, mem_norm_g, w_mem_kv, mem_k_norm_g):
    rows = mem2d.shape[0]
    return pl.pallas_call(
        _memkv_kernel,
        grid=(2,),
        in_specs=[
            pl.BlockSpec((rows, D_MODEL), lambda j: (0, 0)),
            pl.BlockSpec((1, D_MODEL), lambda j: (0, 0)),
            pl.BlockSpec((D_MODEL, MEM_W), lambda j: (0, j)),
            pl.BlockSpec((1, HEAD_DIM), lambda j: (0, 0)),
        ],
        out_specs=pl.BlockSpec((N_MEM_HEADS, rows, HEAD_DIM), lambda j: (j, 0, 0)),
        out_shape=jax.ShapeDtypeStruct((2 * N_MEM_HEADS, rows, HEAD_DIM), BF16),
        compiler_params=_params(("arbitrary",), 40),
        name="memkv",
    )(mem2d, mem_norm_g, w_mem_kv, mem_k_norm_g)


def _inproj_kernel(x_ref, g1_ref, w_ref, qg_ref, kg_ref, mqg_ref, qkv_ref, rest_ref):
    x = x_ref[...]
    h = (x * _rms(x, D_MODEL) * g1_ref[...]).astype(BF16)

    def normed(t, g):
        return t * _rms(t, HEAD_DIM) * g

    gq = qg_ref[...] * (QK_SCALE * LOG2E)
    gk = kg_ref[...]
    gmq = mqg_ref[...] * QK_SCALE
    groups_per_tile = INPROJ_TN // HEAD_DIM
    for j in range(PROJ_W // INPROJ_TN):
        y = jnp.dot(h, w_ref[:, j * INPROJ_TN:(j + 1) * INPROJ_TN], preferred_element_type=F32)
        for hh in range(groups_per_tile):
            t = y[:, hh * HEAD_DIM:(hh + 1) * HEAD_DIM]
            col = j * groups_per_tile + hh
            if col < 8:
                qkv_ref[col] = normed(t, gq)
            elif col < 16:
                qkv_ref[col] = normed(t, gk)
            elif col < 24:
                qkv_ref[col] = t
            elif col < 36:
                rest_ref[col - 24] = t.astype(BF16)
            else:
                rest_ref[col - 24] = normed(t, gmq).astype(BF16)


def _inproj(x2d, norm1_g, w_in_bf16, q_norm_g, k_norm_g, mem_q_norm_g):
    T = x2d.shape[0]
    tm = INPROJ_TM
    small = lambda n: pl.BlockSpec((1, n), lambda i: (0, 0))
    return pl.pallas_call(
        _inproj_kernel,
        grid=(T // tm,),
        in_specs=[
            pl.BlockSpec((tm, D_MODEL), lambda i: (i, 0)),
            small(D_MODEL),
            pl.BlockSpec((D_MODEL, PROJ_W), lambda i: (0, 0), pipeline_mode=pl.Buffered(1)),
            small(HEAD_DIM), small(HEAD_DIM), small(HEAD_DIM),
        ],
        out_specs=[
            pl.BlockSpec((24, tm, HEAD_DIM), lambda i: (0, i, 0)),
            pl.BlockSpec((16, tm, HEAD_DIM), lambda i: (0, i, 0)),
        ],
        out_shape=[
            jax.ShapeDtypeStruct((24, T, HEAD_DIM), F32),
            jax.ShapeDtypeStruct((16, T, HEAD_DIM), BF16),
        ],
        compiler_params=_params(("parallel",), 52),
        name="inproj",
    )(x2d, norm1_g, w_in_bf16, q_norm_g, k_norm_g, mem_q_norm_g)


def _attn_kernel(slopes_ref, q_ref, k_ref, v_ref, o_ref,
                 kx, vx, q4, k4, v4, o1, l1, o4, l4, o16, l16, out_scr, bias_scr):
    h = pl.program_id(1)
    c = pl.program_id(2)
    C = ATTN_CHUNK

    @pl.when(c == 0)
    def _():
        kx[0:C, :] = jnp.zeros((C, HEAD_DIM), F32)
        vx[0:C, :] = jnp.zeros((C, HEAD_DIM), F32)

    @pl.when(c > 0)
    def _():
        kx[0:C, :] = kx[C:2 * C, :]
        vx[0:C, :] = vx[C:2 * C, :]

    kx[C:2 * C, :] = k_ref[0]
    vx[C:2 * C, :] = v_ref[0]

    for b in range(4):
        q4[b] = q_ref[0, pl.ds(b, C // 4, stride=4), :]
        k4[b] = kx[pl.ds(b, 2 * C // 4, stride=4), :]
        v4[b] = vx[pl.ds(b, 2 * C // 4, stride=4), :]

    slope = slopes_ref[h]
    ii = lax.broadcasted_iota(I32, (SPAN, 2 * SPAN), 0)
    jj = lax.broadcasted_iota(I32, (SPAN, 2 * SPAN), 1)
    dist = ii - jj + SPAN
    band = (dist >= 0) & (dist <= SPAN)
    distf = dist.astype(F32)

    for bi, d in enumerate(DILATIONS):
        neg = (-slope * float(d) * LOG2E) * distf
        bias_scr[2 * bi] = jnp.where(band, neg, NEG_INF)
        bias_scr[2 * bi + 1] = jnp.where(band & (jj >= SPAN), neg, NEG_INF)

    ones = jnp.ones((2 * SPAN, HEAD_DIM), BF16)

    def attend(q, k, v, bias):
        s = lax.dot_general(q.astype(BF16), k.astype(BF16), (((1,), (1,)), ((), ())),
                            preferred_element_type=F32) + bias
        m = jnp.max(s, axis=-1, keepdims=True)
        p = jnp.exp2(s - m).astype(BF16)
        ol = jnp.dot(p, jnp.concatenate([v.astype(BF16), ones], axis=1), preferred_element_type=F32)
        l = ol[:, HEAD_DIM:]
        return ol[:, :HEAD_DIM] * (1.0 / l), m + jnp.log2(l)

    def unit1(u, carry):
        sq = pl.multiple_of(u * SPAN, SPAN)
        sk = pl.multiple_of(C + u * SPAN - SPAN, SPAN)
        first = jnp.logical_and(c == 0, u == 0).astype(I32)
        o, lse = attend(q_ref[0, pl.ds(sq, SPAN), :], kx[pl.ds(sk, 2 * SPAN), :], vx[pl.ds(sk, 2 * SPAN), :],
                        bias_scr[first])
        o1[pl.ds(sq, SPAN), :] = o
        l1[pl.ds(sq, SPAN), :] = lse
        return carry

    def unit4(u, carry):
        b = u % 4
        qb = u // 4
        sq = pl.multiple_of(qb * SPAN, SPAN)
        sk = pl.multiple_of(C // 4 + qb * SPAN - SPAN, SPAN)
        first = jnp.logical_and(c == 0, qb == 0).astype(I32)
        o, lse = attend(q4[b, pl.ds(sq, SPAN), :], k4[b, pl.ds(sk, 2 * SPAN), :], v4[b, pl.ds(sk, 2 * SPAN), :],
                        bias_scr[2 + first])
        o4[b, pl.ds(sq, SPAN), :] = o
        l4[b, pl.ds(sq, SPAN), :] = lse
        return carry

    def unit16(u, carry):
        b = u % 4
        a = u // 4
        first = (c == 0).astype(I32)
        qi = pl.ds(a, SPAN, stride=4)
        ki = pl.ds(a, 2 * SPAN, stride=4)
        o, lse = attend(q4[b, qi, :], k4[b, ki, :], v4[b, ki, :], bias_scr[4 + first])
        o16[b, qi, :] = o
        l16[b, qi, :] = lse
        return carry

    for unit in (unit1, unit4, unit16):
        lax.fori_loop(0, C // SPAN, unit, 0, unroll=ATTN_UNROLL)

    for b in range(4):
        rows_b = pl.ds(b, C // 4, stride=4)
        la, lb, lc = l1[rows_b, :], l4[b], l16[b]
        mx = jnp.maximum(jnp.maximum(la, lb), lc)
        ea, eb, ec = jnp.exp2(la - mx), jnp.exp2(lb - mx), jnp.exp2(lc - mx)
        out_scr[rows_b, :] = (ea * o1[rows_b, :] + eb * o4[b] + ec * o16[b]) * (1.0 / (ea + eb + ec))
    o_ref[0] = out_scr[...].astype(BF16)


def _attention(qkv, slopes, batch, seq):
    T = qkv.shape[1]
    C = ATTN_CHUNK
    nc = seq // C
    blk = lambda off: pl.BlockSpec((1, C, HEAD_DIM), lambda b, h, c, s: (h + off, b * nc + c, 0))
    grid_spec = pltpu.PrefetchScalarGridSpec(
        num_scalar_prefetch=1,
        grid=(batch, N_ATTN_HEADS, nc),
        in_specs=[blk(0), blk(8), blk(16)],
        out_specs=pl.BlockSpec((1, C, HEAD_DIM), lambda b, h, c, s: (h, b * nc + c, 0)),
        scratch_shapes=[
            pltpu.VMEM((2 * C, HEAD_DIM), F32),
            pltpu.VMEM((2 * C, HEAD_DIM), F32),
            pltpu.VMEM((4, C // 4, HEAD_DIM), F32),
            pltpu.VMEM((4, 2 * C // 4, HEAD_DIM), F32),
            pltpu.VMEM((4, 2 * C // 4, HEAD_DIM), F32),
            pltpu.VMEM((C, HEAD_DIM), F32),
            pltpu.VMEM((C, HEAD_DIM), F32),
            pltpu.VMEM((4, C // 4, HEAD_DIM), F32),
            pltpu.VMEM((4, C // 4, HEAD_DIM), F32),
            pltpu.VMEM((4, C // 4, HEAD_DIM), F32),
            pltpu.VMEM((4, C // 4, HEAD_DIM), F32),
            pltpu.VMEM((C, HEAD_DIM), F32),
            pltpu.VMEM((2 * len(DILATIONS), SPAN, 2 * SPAN), F32),
        ],
    )
    return pl.pallas_call(
        _attn_kernel,
        grid_spec=grid_spec,
        out_shape=jax.ShapeDtypeStruct((N_ATTN_HEADS, T, HEAD_DIM), BF16),
        compiler_params=_params(("parallel", "parallel", "arbitrary"), 40),
        name="attn",
    )(slopes, qkv, qkv, qkv)


def _mixer_kernel(attn_ref, rest_ref, halo_ref, kvm_ref, x_ref, wout_ref, og_ref, g2_ref, cw_ref, wr_ref, br_ref,
                  x2_ref, h2_ref, route_ref, counts_ref, y_scr, z_scr, cnt_scr, *, tiles_per_seq):
    i = pl.program_id(0)
    tm = MIXER_TM
    og = og_ref[...]

    @pl.when(i == 0)
    def _():
        cnt_scr[...] = jnp.zeros((1, 128), F32)

    seq_start = (i % tiles_per_seq) == 0
    for g in range(CONV_GROUPS):
        z = rest_ref[CONV_GROUPS + g].astype(F32) * rest_ref[2 * CONV_GROUPS + g].astype(F32)
        zh = halo_ref[CONV_GROUPS + g].astype(F32) * halo_ref[2 * CONV_GROUPS + g].astype(F32)
        z_scr[g, 0:HALO, :] = jnp.where(seq_start, 0.0, zh)
        z_scr[g, HALO:HALO + tm, :] = z

    nb = tm // MIXER_SPLIT
    def mix(blk):
        r0 = blk * nb
        rows = slice(r0, r0 + nb)

        ss = jnp.zeros((nb, 1), F32)
        for hh in range(N_ATTN_HEADS):
            a = attn_ref[hh, rows, :].astype(F32)
            ss = ss + jnp.sum(a * a, axis=-1, keepdims=True)
        r = lax.rsqrt(ss * (1.0 / ATTN_W) + EPS)
        for hh in range(N_ATTN_HEADS):
            lo = hh * HEAD_DIM
            y_scr[rows, lo:lo + HEAD_DIM] = (attn_ref[hh, rows, :].astype(F32) * r * og[:, lo:lo + HEAD_DIM]).astype(BF16)

        convs = []
        ss = jnp.zeros((nb, 1), F32)
        for g in range(CONV_GROUPS):
            bg = rest_ref[g, rows, :].astype(F32)
            z0 = z_scr[g, HALO + r0:HALO + r0 + nb, :]
            z1 = z_scr[g, HALO - 1 + r0:HALO - 1 + r0 + nb, :]
            z2 = z_scr[g, HALO - 2 + r0:HALO - 2 + r0 + nb, :]
            lo = g * HEAD_DIM
            w0 = cw_ref[0:1, lo:lo + HEAD_DIM]
            w1 = cw_ref[1:2, lo:lo + HEAD_DIM]
            w2 = cw_ref[2:3, lo:lo + HEAD_DIM]
            cv = bg * (w2 * z0 + w1 * z1 + w0 * z2)
            convs.append(cv)
            ss = ss + jnp.sum(cv * cv, axis=-1, keepdims=True)
        r = lax.rsqrt(ss * (1.0 / CONV_W) + EPS)
        for g in range(CONV_GROUPS):
            lo = ATTN_W + g * HEAD_DIM
            y_scr[rows, lo:lo + HEAD_DIM] = (convs[g] * r * og[:, lo:lo + HEAD_DIM]).astype(BF16)

        mems = []
        ss = jnp.zeros((nb, 1), F32)
        for hh in range(N_MEM_HEADS):
            mq = rest_ref[3 * CONV_GROUPS + hh, rows, :]
            s = lax.dot_general(mq, kvm_ref[hh], (((1,), (1,)), ((), ())), preferred_element_type=F32)
            m = jnp.max(s, axis=-1, keepdims=True)
            p = jnp.exp(s - m)
            l = jnp.sum(p, axis=-1, keepdims=True)
            o = jnp.dot(p.astype(BF16), kvm_ref[N_MEM_HEADS + hh], preferred_element_type=F32) * (1.0 / l)
            mems.append(o)
            ss = ss + jnp.sum(o * o, axis=-1, keepdims=True)
        r = lax.rsqrt(ss * (1.0 / MEM_W) + EPS)
        for hh in range(N_MEM_HEADS):
            lo = ATTN_W + CONV_W + hh * HEAD_DIM
            y_scr[rows, lo:lo + HEAD_DIM] = (mems[hh] * r * og[:, lo:lo + HEAD_DIM]).astype(BF16)

    def project(blk):
        rows = slice(blk * nb, (blk + 1) * nb)
        x2 = x_ref[rows, :] + jnp.dot(y_scr[rows, :], wout_ref[...], preferred_element_type=F32)
        x2_ref[rows, :] = x2
        h2 = x2 * _rms(x2, D_MODEL) * g2_ref[...]
        h2_ref[rows, :] = h2
        return jnp.dot(h2.astype(BF16), wr_ref[...], preferred_element_type=F32) + br_ref[...]

    def route_rows(blk, lg):
        rows = slice(blk * nb, (blk + 1) * nb)
        lane = lax.broadcasted_iota(I32, (nb, 128), 1).astype(F32)
        gl = jnp.where(lane < N_GROUPS, lg, NEG_INF)
        gmax = jnp.max(gl, axis=-1, keepdims=True)
        g_w = 1.0 / jnp.sum(jnp.exp(gl - gmax), axis=-1, keepdims=True)
        g_idx = jnp.min(jnp.where(gl == gmax, lane, 1e9), axis=-1, keepdims=True)
        e_lo = N_GROUPS + EXPERTS_PER_GROUP * g_idx
        el = jnp.where((lane >= e_lo) & (lane < e_lo + EXPERTS_PER_GROUP), lg, NEG_INF)
        e1 = jnp.max(el, axis=-1, keepdims=True)
        i1 = jnp.min(jnp.where(el == e1, lane, 1e9), axis=-1, keepdims=True)
        el2 = jnp.where(lane == i1, NEG_INF, el)
        e2 = jnp.max(el2, axis=-1, keepdims=True)
        i2 = jnp.min(jnp.where(el2 == e2, lane, 1e9), axis=-1, keepdims=True)
        t = jnp.exp(e2 - e1)
        p1 = 1.0 / (1.0 + t)
        p2 = t * p1

        sel1 = i1 - N_GROUPS
        sel2 = i2 - N_GROUPS
        hot1 = lane == sel1
        hot2 = lane == sel2
        hot = jnp.where(hot1 | hot2, 1.0, 0.0)
        row_id = lax.broadcasted_iota(I32, (nb, nb), 0)
        col_id = lax.broadcasted_iota(I32, (nb, nb), 1)
        earlier = jnp.where(col_id < row_id, 1.0, 0.0).astype(BF16)
        before = jnp.dot(earlier, hot.astype(BF16), preferred_element_type=F32) + cnt_scr[...]
        rank1 = jnp.sum(jnp.where(hot1, before, 0.0), axis=-1, keepdims=True)
        rank2 = jnp.sum(jnp.where(hot2, before, 0.0), axis=-1, keepdims=True)
        cnt_scr[...] = cnt_scr[...] + jnp.sum(hot, axis=0, keepdims=True)

        route_ref[rows, :] = jnp.where(lane == 0, sel1,
                                       jnp.where(lane == 1, sel2,
                                                 jnp.where(lane == 2, g_w * p1,
                                                           jnp.where(lane == 3, g_w * p2,
                                                                     jnp.where(lane == 4, rank1,
                                                                               jnp.where(lane == 5, rank2, 0.0))))))

    mix(0)
    for blk in range(MIXER_SPLIT):
        lg = project(blk)
        if blk + 1 < MIXER_SPLIT:
            mix(blk + 1)
        route_rows(blk, lg)

    counts_ref[...] = cnt_scr[...]


def _mixer(attn, rest, kvm, x2d, w_out_bf16, out_norm_g, norm2_g, conv_w, w_router, b_router, seq):
    T = x2d.shape[0]
    tm = MIXER_TM
    tiles_per_seq = seq // tm
    full = lambda shape: pl.BlockSpec(shape, lambda i: tuple(0 for _ in shape))
    return pl.pallas_call(
        functools.partial(_mixer_kernel, tiles_per_seq=tiles_per_seq),
        grid=(T // tm,),
        in_specs=[
            pl.BlockSpec((N_ATTN_HEADS, tm, HEAD_DIM), lambda i: (0, i, 0)),
            pl.BlockSpec((16, tm, HEAD_DIM), lambda i: (0, i, 0)),
            pl.BlockSpec((16, HALO, HEAD_DIM), lambda i: (0, jnp.maximum(i * (tm // HALO) - 1, 0), 0)),
            pl.BlockSpec((2 * N_MEM_HEADS, N_MEM, HEAD_DIM), lambda i: (0, i // tiles_per_seq, 0)),
            pl.BlockSpec((tm, D_MODEL), lambda i: (i, 0)),
            pl.BlockSpec((D_MODEL, D_MODEL), lambda i: (0, 0), pipeline_mode=pl.Buffered(1)),
            full((1, D_MODEL)),
            full((1, D_MODEL)),
            full((3, CONV_W)),
            full((D_MODEL, 128)),
            full((1, 128)),
        ],
        out_specs=[
            pl.BlockSpec((tm, D_MODEL), lambda i: (i, 0)),
            pl.BlockSpec((tm, D_MODEL), lambda i: (i, 0)),
            pl.BlockSpec((tm, 128), lambda i: (i, 0)),
            pl.BlockSpec((1, 128), lambda i: (0, 0)),
        ],
        out_shape=[
            jax.ShapeDtypeStruct((T, D_MODEL), F32),
            jax.ShapeDtypeStruct((T, D_MODEL), F32),
            jax.ShapeDtypeStruct((T, 128), F32),
            jax.ShapeDtypeStruct((1, 128), F32),
        ],
        scratch_shapes=[
            pltpu.VMEM((tm, D_MODEL), BF16),
            pltpu.VMEM((CONV_GROUPS, HALO + tm, HEAD_DIM), F32),
            pltpu.VMEM((1, 128), F32),
        ],
        compiler_params=_params(("arbitrary",), 56),
        name="mixer",
    )(attn, rest, rest, kvm, x2d, w_out_bf16, out_norm_g, norm2_g, conv_w, w_router, b_router)


def _dispatch_kernel(dest_ref, h_ref, xbuf_ref, sem):
    i = pl.program_id(0)
    tm = DISPATCH_TM
    base = i * tm

    def body(t, carry):
        a = 2 * (base + t)
        src = h_ref.at[pl.ds(t, 1)]
        pltpu.make_async_copy(src, xbuf_ref.at[pl.ds(dest_ref[a], 1)], sem).start(priority=0)
        pltpu.make_async_copy(src, xbuf_ref.at[pl.ds(dest_ref[a + 1], 1)], sem).start(priority=1)
        return carry

    lax.fori_loop(0, tm, body, 0, unroll=DMA_UNROLL)
    pltpu.make_async_copy(xbuf_ref.at[pl.ds(0, 2 * tm)], xbuf_ref.at[pl.ds(0, 2 * tm)], sem).wait()


def _dispatch(dest, h2, n_rows):
    T = h2.shape[0]
    tm = DISPATCH_TM
    grid_spec = pltpu.PrefetchScalarGridSpec(
        num_scalar_prefetch=1,
        grid=(T // tm,),
        in_specs=[pl.BlockSpec((tm, D_MODEL), lambda i, d: (i, 0))],
        out_specs=pl.BlockSpec(memory_space=pl.ANY),
        scratch_shapes=[pltpu.SemaphoreType.DMA],
    )
    return pl.pallas_call(
        _dispatch_kernel,
        grid_spec=grid_spec,
        out_shape=jax.ShapeDtypeStruct((n_rows, D_MODEL), F32),
        compiler_params=_params(("arbitrary",), 32),
        name="dispatch",
    )(dest, h2)


def _ffn_kernel(ie_ref, rows_ref, x_ref, wg_ref, wu_ref, wd_ref, y_ref, xg, xb, acc, xsem, ysem):
    w = pl.program_id(0)
    c = pl.program_id(1)

    def subs(v):
        return sum((v > lo).astype(I32) for lo, _ in FFN_PIECES)

    rows = rows_ref[w]
    ns = subs(rows)
    ns_next = subs(rows_ref[w + 1])
    ns_prev = jnp.where(w > 0, subs(rows_ref[jnp.maximum(w - 1, 0)]), 0)
    ns_prev2 = jnp.where(w > 1, subs(rows_ref[jnp.maximum(w - 2, 0)]), 0)
    p = w % 2

    def x_copy(item, s):
        lo, n = FFN_PIECES[s]
        return pltpu.make_async_copy(x_ref.at[pl.ds(item * ITEM_ROWS + lo, n)], xg.at[pl.ds(lo, n)], xsem)

    def y_copy(item, slot, s):
        lo, n = FFN_PIECES[s]
        return pltpu.make_async_copy(acc.at[slot, pl.ds(lo, n)], y_ref.at[pl.ds(item * ITEM_ROWS + lo, n)],
                                     ysem.at[slot])

    def for_pieces(pieces, count, fn):
        for s in pieces:
            @pl.when(s < count)
            def _(s=s):
                fn(s)

    all_pieces = range(len(FFN_PIECES))

    @pl.when(c == 0)
    def _():
        @pl.when(w == 0)
        def _():
            for_pieces(all_pieces, ns, lambda s: x_copy(0, s).start())

        for_pieces(all_pieces, ns, lambda s: x_copy(w, s).wait())
        for_pieces(all_pieces, ns_prev2, lambda s: y_copy(w - 2, p, s).wait())

    for step, pieces in Y_PIECES_AT_STEP.items():
        @pl.when(c == step)
        def _(pieces=pieces):
            for_pieces(pieces, ns_prev, lambda s: y_copy(w - 1, 1 - p, s).start())

    for step, pieces in X_PIECES_AT_STEP.items():
        @pl.when(c == step)
        def _(pieces=pieces):
            for_pieces(pieces, ns_next, lambda s: x_copy(w + 1, s).start())

    for ns_static in range(1, len(FFN_PIECES) + 1):
        nr = FFN_PIECES[ns_static - 1][0] + FFN_PIECES[ns_static - 1][1]

        @pl.when(ns == ns_static)
        def _(nr=nr):
            @pl.when(c == 0)
            def _():
                ok = lax.broadcasted_iota(I32, (nr, 1), 0) < rows
                xb[0:nr, :] = jnp.where(ok, xg[0:nr, :], 0.0).astype(BF16)
                acc[p, 0:nr, :] = jnp.zeros((nr, D_MODEL), F32)

            wg = wg_ref[0].astype(BF16)
            wu = wu_ref[0].astype(BF16)
            wd = wd_ref[0].astype(BF16)
            x = xb[0:nr, :]
            a = jnp.dot(x, wg, preferred_element_type=F32)
            u = jnp.dot(x, wu, preferred_element_type=F32)
            hm = (a / (1.0 + jnp.exp(-a)) * u).astype(BF16)
            acc[p, 0:nr, :] += jnp.dot(hm, wd, preferred_element_type=F32)

def _ffn(item_expert, item_rows, xbuf, w_gate, w_up, w_down):
    n_chunks = D_EXPERT // FFN_CHUNK
    grid_items = item_rows.shape[0] - 1

    def chunk_of(w, c, ir):
        return jnp.where(ir[w] > 0, c, n_chunks - 1)

    grid_spec = pltpu.PrefetchScalarGridSpec(
        num_scalar_prefetch=2,
        grid=(grid_items, n_chunks),
        in_specs=[
            pl.BlockSpec(memory_space=pl.ANY),
            pl.BlockSpec((1, D_MODEL, FFN_CHUNK), lambda w, c, ie, ir: (ie[w], 0, chunk_of(w, c, ir))),
            pl.BlockSpec((1, D_MODEL, FFN_CHUNK), lambda w, c, ie, ir: (ie[w], 0, chunk_of(w, c, ir))),
            pl.BlockSpec((1, FFN_CHUNK, D_MODEL), lambda w, c, ie, ir: (ie[w], chunk_of(w, c, ir), 0)),
        ],
        out_specs=pl.BlockSpec(memory_space=pl.ANY),
        scratch_shapes=[
            pltpu.VMEM((ITEM_ROWS, D_MODEL), F32),
            pltpu.VMEM((ITEM_ROWS, D_MODEL), BF16),
            pltpu.VMEM((2, ITEM_ROWS, D_MODEL), F32),
            pltpu.SemaphoreType.DMA,
            pltpu.SemaphoreType.DMA((2,)),
        ],
    )
    return pl.pallas_call(
        _ffn_kernel,
        grid_spec=grid_spec,
        out_shape=jax.ShapeDtypeStruct(xbuf.shape, F32),
        compiler_params=_params(("arbitrary", "arbitrary"), 58),
        name="ffn",
    )(item_expert, item_rows, xbuf, w_gate, w_up, w_down)


def _combine_kernel(dest_ref, x2_ref, route_ref, ybuf_ref, o_ref, rows_scr, sem):
    i = pl.program_id(0)
    n = pl.num_programs(0)
    tm = COMBINE_TM

    def gather_tile(tile, slot):
        base = tile * tm

        def body(t, carry):
            a = 2 * (base + t)
            pltpu.make_async_copy(ybuf_ref.at[pl.ds(dest_ref[a], 1)], rows_scr.at[slot, 0, pl.ds(t, 1)],
                                  sem.at[slot]).start(priority=0)
            pltpu.make_async_copy(ybuf_ref.at[pl.ds(dest_ref[a + 1], 1)], rows_scr.at[slot, 1, pl.ds(t, 1)],
                                  sem.at[slot]).start(priority=1)
            return carry

        lax.fori_loop(0, tm, body, 0, unroll=DMA_UNROLL)

    slot = i % 2

    @pl.when(i == 0)
    def _():
        gather_tile(0, 0)

    @pl.when(i + 1 < n)
    def _():
        gather_tile(i + 1, 1 - slot)

    for k in range(2):
        pltpu.make_async_copy(ybuf_ref.at[pl.ds(0, tm)], rows_scr.at[slot, k], sem.at[slot]).wait()

    o_ref[...] = x2_ref[...] + route_ref[:, 2:3] * rows_scr[slot, 0] + route_ref[:, 3:4] * rows_scr[slot, 1]


def _combine(dest, x2, route, ybuf):
    T = x2.shape[0]
    tm = COMBINE_TM
    grid_spec = pltpu.PrefetchScalarGridSpec(
        num_scalar_prefetch=1,
        grid=(T // tm,),
        in_specs=[
            pl.BlockSpec((tm, D_MODEL), lambda i, d: (i, 0)),
            pl.BlockSpec((tm, 128), lambda i, d: (i, 0)),
            pl.BlockSpec(memory_space=pl.ANY),
        ],
        out_specs=pl.BlockSpec((tm, D_MODEL), lambda i, d: (i, 0)),
        scratch_shapes=[pltpu.VMEM((2, 2, tm, D_MODEL), F32), pltpu.SemaphoreType.DMA((2,))],
    )
    return pl.pallas_call(
        _combine_kernel,
        grid_spec=grid_spec,
        out_shape=jax.ShapeDtypeStruct((T, D_MODEL), F32),
        compiler_params=_params(("arbitrary",), 48),
        name="combine",
    )(dest, x2, route, ybuf)


def _dest_kernel(route_ref, base_ref, o_ref):
    r = route_ref[...]
    lane = lax.broadcasted_iota(I32, r.shape, 1).astype(F32)
    base = base_ref[...]
    d1 = jnp.sum(jnp.where(lane == r[:, 0:1], base, 0.0), axis=-1, keepdims=True) + r[:, 4:5]
    d2 = jnp.sum(jnp.where(lane == r[:, 1:2], base, 0.0), axis=-1, keepdims=True) + r[:, 5:6]
    o_ref[...] = jnp.where(lane == 0, d1, jnp.where(lane == 1, d2, 0.0))


def _dest(route, base):
    T = route.shape[0]
    tm = DEST_TM
    return pl.pallas_call(
        _dest_kernel,
        grid=(T // tm,),
        in_specs=[pl.BlockSpec((tm, 128), lambda i: (i, 0)), pl.BlockSpec((1, 128), lambda i: (0, 0))],
        out_specs=pl.BlockSpec((tm, 128), lambda i: (i, 0)),
        out_shape=jax.ShapeDtypeStruct((T, 128), F32),
        compiler_params=_params(("parallel",), 32),
        name="dest",
    )(route, base)


def _moe_plan(route, counts, max_items):
    counts = counts[0, :N_EXPERTS].astype(I32)
    n_it = (counts + ITEM_ROWS - 1) // ITEM_ROWS
    it_end = jnp.cumsum(n_it)
    it_start = it_end - n_it
    base = jnp.zeros((1, 128), F32).at[0, :N_EXPERTS].set((it_start * ITEM_ROWS).astype(F32))
    dest = _dest(route, base)[:, :2].astype(I32).reshape(-1)
    n_items = it_end[-1]
    w = jnp.arange(max_items + 3, dtype=I32)
    ie = jnp.minimum(jnp.sum((it_end[None, :] <= w[:, None]).astype(I32), axis=1), N_EXPERTS - 1)
    used = w < n_items
    rows = jnp.where(used, jnp.minimum(ITEM_ROWS, counts[ie] - (w - it_start[ie]) * ITEM_ROWS), 0)
    ie = jnp.where(used, ie, ie[jnp.maximum(n_items - 1, 0)])
    return dest.astype(I32), ie.astype(I32), rows.astype(I32)


def _layer(x, mem, norm1_g, w_in, q_norm_g, k_norm_g, conv_w, mem_norm_g, w_mem_kv, mem_q_norm_g, mem_k_norm_g,
           out_norm_g, w_out, norm2_g, w_rg, b_rg, w_re, b_re, w_gate, w_up, w_down):
    B, S, D = x.shape
    T = B * S
    x2d = x.reshape(T, D)
    row = lambda v: v.reshape(1, -1).astype(F32)

    kvm = _memkv(mem.reshape(B * N_MEM, D), row(mem_norm_g), w_mem_kv, row(mem_k_norm_g))
    qkv, rest = _inproj(x2d, row(norm1_g), w_in.astype(BF16), row(q_norm_g), row(k_norm_g), row(mem_q_norm_g))

    slopes = 2.0 ** (-8.0 * jnp.arange(1, N_ATTN_HEADS + 1, dtype=F32) / N_ATTN_HEADS)
    attn = _attention(qkv, slopes, B, S)

    w_router = jnp.zeros((D, 128), F32).at[:, :N_GROUPS].set(w_rg).at[:, N_GROUPS:N_GROUPS + N_EXPERTS].set(w_re)
    b_router = jnp.zeros((1, 128), F32).at[0, :N_GROUPS].set(b_rg).at[0, N_GROUPS:N_GROUPS + N_EXPERTS].set(b_re)
    x2, h2, route, counts = _mixer(attn, rest, kvm, x2d, w_out.astype(BF16), row(out_norm_g), row(norm2_g),
                                   conv_w.astype(F32), w_router.astype(BF16), b_router, S)

    max_items = N_EXPERTS + (2 * T) // ITEM_ROWS
    dest, item_expert, item_rows = _moe_plan(route, counts, max_items)
    xbuf = _dispatch(dest, h2, max_items * ITEM_ROWS)
    ybuf = _ffn(item_expert, item_rows, xbuf, w_gate, w_up, w_down)
    out = _combine(dest, x2, route, ybuf)
    return out.reshape(B, S, D)


def kernel(x, mem, norm1_g, w_in, q_norm_g, k_norm_g, conv_w, mem_norm_g, w_mem_kv, mem_q_norm_g, mem_k_norm_g,
           out_norm_g, w_out, norm2_g, w_router_group, b_router_group, w_router_expert, b_router_expert,
           w_gate, w_up, w_down):
    for l in range(norm1_g.shape[0]):
        x = _layer(x, mem, norm1_g[l], w_in[l], q_norm_g[l], k_norm_g[l], conv_w[l], mem_norm_g[l], w_mem_kv[l],
                   mem_q_norm_g[l], mem_k_norm_g[l], out_norm_g[l], w_out[l], norm2_g[l],
                   w_router_group[l], b_router_group[l], w_router_expert[l], b_router_expert[l],
                   w_gate[l], w_up[l], w_down[l])
    return x
```
